```python
import jax, jax.numpy as jnp
from jax import lax
import numpy as np

D_MODEL = 2048
BATCH = 4
SEQ = 2048
DEPTH = 1
DEC_BATCH = 8
DEC_SEQ = 4
PAST_LEN = 16384
PAGE_SIZE = 128

N_HEADS = 8
HEAD_DIM = D_MODEL // 16
ATTN_WIDTH = N_HEADS * HEAD_DIM
MOBA_BLOCK = 256
MOBA_TOPK = 3
Q_BLOCK = 16
ROPE_THETA = 500000.0
ROPE_DIM = HEAD_DIM // 4
GMLP_GROUPS = 8
GMLP_CHUNK = 128
GMLP_WIDTH = D_MODEL // 2
GMLP_GROUP_DIM = GMLP_WIDTH // GMLP_GROUPS
IN_WIDTH = 3 * ATTN_WIDTH + 2 * GMLP_WIDTH
FFN_HIDDEN = -(-8 * D_MODEL // (3 * 256)) * 256
PLE_DIM = 256
NORM_EPS = 1e-6
NEG_INF = -1e30

kernel_name = "moba_gmlp_gated_hybrid_step"


def rms_norm(x, g):
    xf = x.astype(jnp.float32)
    y = xf * lax.rsqrt(jnp.mean(xf * xf, axis=-1, keepdims=True) + NORM_EPS)
    return (y * g.astype(jnp.float32)).astype(x.dtype)


def layer_norm(x, g):
    xf = x.astype(jnp.float32)
    xc = xf - jnp.mean(xf, axis=-1, keepdims=True)
    y = xc * lax.rsqrt(jnp.mean(xc * xc, axis=-1, keepdims=True) + NORM_EPS)
    return (y * g.astype(jnp.float32)).astype(x.dtype)


def partial_rope(x, pos):
    half = ROPE_DIM // 2
    freqs = jnp.power(jnp.float32(ROPE_THETA), -2.0 * jnp.arange(half, dtype=jnp.float32) / ROPE_DIM)
    ang = pos.astype(jnp.float32)[:, None] * freqs[None, :]
    cos = jnp.cos(ang)[None, :, None, :]
    sin = jnp.sin(ang)[None, :, None, :]
    xf = x.astype(jnp.float32)
    x1 = xf[..., :half]
    x2 = xf[..., half:ROPE_DIM]
    out = jnp.concatenate([x1 * cos - x2 * sin, x2 * cos + x1 * sin, xf[..., ROPE_DIM:]], axis=-1)
    return out.astype(x.dtype)


def to_blocks(parts):
    B, _, H, hd = parts[0].shape
    L = sum(p.shape[1] for p in parts)
    nb = -(-L // MOBA_BLOCK)
    pad = nb * MOBA_BLOCK - L
    full = jnp.concatenate(list(parts) + [jnp.zeros((B, pad, H, hd), parts[0].dtype)], axis=1)
    return full.reshape(B, nb, MOBA_BLOCK, H, hd)


def moba_block(q, pos, kb, vb, kmean):
    B, Qb, H, hd = q.shape
    NB = kb.shape[1]
    qh = q.transpose(0, 2, 1, 3)
    gate = jnp.einsum('bhqd,bnhd->bhqn', qh.astype(jnp.float32), kmean)
    qblk = pos // MOBA_BLOCK
    past_ok = jnp.arange(NB)[None, :] < qblk[:, None]
    gate = jnp.where(past_ok[None, None], gate, NEG_INF)
    ksel = min(MOBA_TOPK, NB)
    _, sel = lax.top_k(gate, ksel)
    sel_ok = sel < qblk[None, None, :, None]
    own = jnp.broadcast_to(qblk[None, None, :, None], (B, H, Qb, 1)).astype(sel.dtype)
    idx = jnp.concatenate([sel, own], axis=-1)
    slot_ok = jnp.concatenate([sel_ok, jnp.ones((B, H, Qb, 1), bool)], axis=-1)
    bi = jnp.arange(B)[:, None, None, None]
    hi = jnp.arange(H)[None, :, None, None]
    kg = kb[bi, idx, :, hi]
    vg = vb[bi, idx, :, hi]
    kpos = idx[..., None] * MOBA_BLOCK + jnp.arange(MOBA_BLOCK)
    mask = slot_ok[..., None] & (kpos <= pos[None, None, :, None, None])
    s = jnp.einsum('bhqd,bhqskd->bhqsk', qh, kg, preferred_element_type=jnp.float32) * (HEAD_DIM ** -0.5)
    s = jnp.where(mask, s, NEG_INF)
    nslot = idx.shape[-1]
    p = jax.nn.softmax(s.reshape(B, H, Qb, nslot * MOBA_BLOCK), axis=-1).reshape(s.shape)
    o = jnp.einsum('bhqsk,bhqskd->bhqd', p.astype(vg.dtype), vg)
    return o.transpose(0, 2, 1, 3)


def moba_prompt(q, k, v, pos):
    B, S, H, hd = q.shape
    kb = to_blocks([k])
    vb = to_blocks([v])
    kmean = jnp.mean(kb.astype(jnp.float32), axis=2)
    nqb = S // Q_BLOCK
    qs = q.reshape(B, nqb, Q_BLOCK, H, hd).transpose(1, 0, 2, 3, 4)
    ps = pos.reshape(nqb, Q_BLOCK)
    out = lax.map(lambda a: moba_block(a[0], a[1], kb, vb, kmean), (qs, ps))
    return out.transpose(1, 0, 2, 3, 4).reshape(B, S, H, hd)


def moba_sample(q, past_k, k, past_v, v, pos):
    kb = to_blocks([past_k, k])
    vb = to_blocks([past_v, v])
    kmean = jnp.mean(kb.astype(jnp.float32), axis=2)
    return moba_block(q, pos, kb, vb, kmean)


def spatial_gating(u, vn, w_s, b_s):
    B, S, W = vn.shape
    n = min(S, GMLP_CHUNK)
    nc = S // n
    wm = jnp.tril(w_s[:, :n, :n])
    vr = vn.reshape(B, nc, n, GMLP_GROUPS, GMLP_GROUP_DIM)
    s = jnp.einsum('gts,bcsgd->bctgd', wm, vr) + b_s[:, :n].T[None, None, :, :, None]
    return u * s.reshape(B, S, W)


def mixer_inputs(x, pos, g_pre, w_in, g_vn):
    B, S, _ = x.shape
    h = rms_norm(x, g_pre)
    z = h @ w_in
    q, k, v, u, vg = jnp.split(z, [ATTN_WIDTH, 2 * ATTN_WIDTH, 3 * ATTN_WIDTH, 3 * ATTN_WIDTH + GMLP_WIDTH], axis=-1)
    q = partial_rope(q.reshape(B, S, N_HEADS, HEAD_DIM), pos)
    k = partial_rope(k.reshape(B, S, N_HEADS, HEAD_DIM), pos)
    v = v.reshape(B, S, N_HEADS, HEAD_DIM)
    u = jax.nn.gelu(u)
    vn = layer_norm(jax.nn.gelu(vg), g_vn)
    return h, q, k, v, u, vn


def layer_tail(x, h, attn_o, gm_o, p, w_a_out, w_b_out, w_gate, w_o, g_post_mix,
               g_pre_ffn, w_ffn_in, w_ffn_out, g_post_ffn, g_ple, w_ple_gate, w_ple):
    B, S, _ = x.shape
    a = attn_o.reshape(B, S, ATTN_WIDTH) @ w_a_out
    b = gm_o @ w_b_out
    ga, gb = jnp.split(jax.nn.sigmoid(h @ w_gate), 2, axis=-1)
    mix = (ga * a + gb * b) @ w_o
    x = x + rms_norm(mix, g_post_mix)
    a1, g1 = jnp.split(rms_norm(x, g_pre_ffn) @ w_ffn_in, 2, axis=-1)
    f = (jax.nn.silu(a1) * g1) @ w_ffn_out
    x = x + rms_norm(f, g_post_ffn)
    gate = jax.nn.sigmoid(rms_norm(x, g_ple) @ w_ple_gate)
    return x + (p @ w_ple) * gate


def setup_inputs(seed: int = 0) -> dict:
    key = jax.random.key(seed)
    ks = jax.random.split(key, 32)
    f32 = jnp.float32
    n_pages = PAST_LEN // PAGE_SIZE
    n_used = DEC_BATCH * n_pages
    n_pool = n_used + max(1, n_used // 4)

    def nrm(k, shape, scale):
        return jax.random.normal(k, shape, f32) * scale

    def gain(k, shape):
        return 1.0 + 0.05 * jax.random.normal(k, shape, f32)

    page_table = jax.random.permutation(ks[0], n_pool)[:n_used].reshape(DEC_BATCH, n_pages).astype(jnp.int32)
    return {
        "x_prompt": nrm(ks[1], (BATCH, SEQ, D_MODEL), 1.0),
        "x_sample": nrm(ks[2], (DEC_BATCH, DEC_SEQ, D_MODEL), 1.0),
        "cache_k": nrm(ks[3], (DEPTH, n_pool, PAGE_SIZE, N_HEADS, HEAD_DIM), 1.0),
        "cache_v": nrm(ks[4], (DEPTH, n_pool, PAGE_SIZE, N_HEADS, HEAD_DIM), 1.0),
        "page_table": page_table,
        "p_prompt": nrm(ks[5], (DEPTH, BATCH, SEQ, PLE_DIM), 1.0),
        "p_sample": nrm(ks[6], (DEPTH, DEC_BATCH, DEC_SEQ, PLE_DIM), 1.0),
        "g_pre_mix": gain(ks[7], (DEPTH, D_MODEL)),
        "w_in": nrm(ks[8], (DEPTH, D_MODEL, IN_WIDTH), D_MODEL ** -0.5),
        "g_vnorm": gain(ks[9], (DEPTH, GMLP_WIDTH)),
        "w_spatial": nrm(ks[10], (DEPTH, GMLP_GROUPS, GMLP_CHUNK, GMLP_CHUNK), GMLP_CHUNK ** -0.5),
        "b_spatial": 1.0 + 0.1 * jax.random.normal(ks[11], (DEPTH, GMLP_GROUPS, GMLP_CHUNK), f32),
        "w_a_out": nrm(ks[12], (DEPTH, ATTN_WIDTH, D_MODEL), ATTN_WIDTH ** -0.5),
        "w_b_out": nrm(ks[13], (DEPTH, GMLP_WIDTH, D_MODEL), GMLP_WIDTH ** -0.5),
        "w_gate": nrm(ks[14], (DEPTH, D_MODEL, 2 * D_MODEL), D_MODEL ** -0.5),
        "w_o": nrm(ks[15], (DEPTH, D_MODEL, D_MODEL), D_MODEL ** -0.5),
        "g_post_mix": gain(ks[16], (DEPTH, D_MODEL)),
        "g_pre_ffn": gain(ks[17], (DEPTH, D_MODEL)),
        "w_ffn_in": nrm(ks[18], (DEPTH, D_MODEL, 2 * FFN_HIDDEN), D_MODEL ** -0.5),
        "w_ffn_out": nrm(ks[19], (DEPTH, FFN_HIDDEN, D_MODEL), FFN_HIDDEN ** -0.5),
        "g_post_ffn": gain(ks[20], (DEPTH, D_MODEL)),
        "g_ple": gain(ks[21], (DEPTH, D_MODEL)),
        "w_ple_gate": nrm(ks[22], (DEPTH, D_MODEL, D_MODEL), D_MODEL ** -0.5),
        "w_ple": nrm(ks[23], (DEPTH, PLE_DIM, D_MODEL), PLE_DIM ** -0.5),
    }


def reference(x_prompt, x_sample, cache_k, cache_v, page_table, p_prompt, p_sample,
              g_pre_mix, w_in, g_vnorm, w_spatial, b_spatial, w_a_out, w_b_out, w_gate, w_o,
              g_post_mix, g_pre_ffn, w_ffn_in, w_ffn_out, g_post_ffn, g_ple, w_ple_gate, w_ple):
    n_seq = x_prompt.shape[1]
    dec_b, dec_s = x_sample.shape[0], x_sample.shape[1]
    past_len = page_table.shape[1] * cache_k.shape[2]
    pos_p = jnp.arange(n_seq, dtype=jnp.int32)
    pos_s = past_len + jnp.arange(dec_s, dtype=jnp.int32)
    xp, xs = x_prompt, x_sample
    kp_rows, vp_rows, ks_rows, vs_rows, gv_rows = [], [], [], [], []
    for l in range(DEPTH):
        hp, qp, kp, vp, up, vnp = mixer_inputs(xp, pos_p, g_pre_mix[l], w_in[l], g_vnorm[l])
        ap = moba_prompt(qp, kp, vp, pos_p)
        gp = spatial_gating(up, vnp, w_spatial[l], b_spatial[l])
        xp = layer_tail(xp, hp, ap, gp, p_prompt[l], w_a_out[l], w_b_out[l], w_gate[l], w_o[l],
                        g_post_mix[l], g_pre_ffn[l], w_ffn_in[l], w_ffn_out[l], g_post_ffn[l],
                        g_ple[l], w_ple_gate[l], w_ple[l])
        hs, qs, ks_, vs_, us, vns = mixer_inputs(xs, pos_s, g_pre_mix[l], w_in[l], g_vnorm[l])
        past_k = cache_k[l][page_table].reshape(dec_b, past_len, N_HEADS, HEAD_DIM)
        past_v = cache_v[l][page_table].reshape(dec_b, past_len, N_HEADS, HEAD_DIM)
        as_ = moba_sample(qs, past_k, ks_, past_v, vs_, pos_s)
        gs = spatial_gating(us, vns, w_spatial[l], b_spatial[l])
        xs = layer_tail(xs, hs, as_, gs, p_sample[l], w_a_out[l], w_b_out[l], w_gate[l], w_o[l],
                        g_post_mix[l], g_pre_ffn[l], w_ffn_in[l], w_ffn_out[l], g_post_ffn[l],
                        g_ple[l], w_ple_gate[l], w_ple[l])
        kp_rows.append(kp)
        vp_rows.append(vp)
        ks_rows.append(ks_)
        vs_rows.append(vs_)
        gv_rows.append(vns)
    return (xp, xs, jnp.stack(kp_rows), jnp.stack(vp_rows), jnp.stack(ks_rows), jnp.stack(vs_rows), jnp.stack(gv_rows))
```

```python
import functools

import jax
import jax.numpy as jnp
from jax import lax
from jax.experimental import pallas as pl
from jax.experimental.pallas import tpu as pltpu

F32 = jnp.float32
BF16 = jnp.bfloat16

N_HEADS = 8
HEAD_DIM = 128
ATTN_WIDTH = N_HEADS * HEAD_DIM
MOBA_BLOCK = 256
MOBA_TOPK = 3
ROPE_THETA = 500000.0
ROPE_DIM = HEAD_DIM // 4
ROPE_HALF = ROPE_DIM // 2
GMLP_GROUPS = 8
GMLP_CHUNK = 128
GMLP_WIDTH = 1024
GMLP_GROUP_DIM = GMLP_WIDTH // GMLP_GROUPS
NORM_EPS = 1e-6
NEG_INF = -1e30
TAKEN = -3e38
SAMPLE_ROWS = 16

VMEM_LIMIT_BYTES = 56 * 1024 * 1024


def _params(*semantics):
    return pltpu.CompilerParams(dimension_semantics=semantics, vmem_limit_bytes=VMEM_LIMIT_BYTES)


def _rms(x, g):
    return x * lax.rsqrt(jnp.mean(x * x, axis=-1, keepdims=True) + NORM_EPS) * g


def _dot(a, b):
    return jnp.dot(a, b, preferred_element_type=F32)


def _dot_nt(a, b):
    return lax.dot_general(a, b, (((1,), (1,)), ((), ())), preferred_element_type=F32)


def _mixer_kernel(x_ref, g_ref, w_ref, cos_ref, sa_ref, sb_ref, gvn_ref, wsp_ref, bsp_ref,
                  h_out, q_out, k_out, v_out, gm_out, *rest, chunk, with_vn):
    if with_vn:
        vn_out, u_scr = rest
    else:
        (u_scr,) = rest
    j = pl.program_id(1)
    tm = x_ref.shape[0]

    @pl.when(j == 0)
    def _():
        h_out[...] = _rms(x_ref[...], g_ref[...]).astype(BF16)

    z = _dot(h_out[...], w_ref[...])

    def rope_to(out_ref):
        c, sa, sb = cos_ref[...], sa_ref[...], sb_ref[...]
        for hd in range(N_HEADS):
            sl = slice(hd * HEAD_DIM, (hd + 1) * HEAD_DIM)
            zs = z[:, sl]
            out_ref[:, sl] = (zs * c + pltpu.roll(zs, HEAD_DIM - ROPE_HALF, 1) * sa
                              + pltpu.roll(zs, ROPE_HALF, 1) * sb)

    @pl.when(j == 0)
    def _():
        rope_to(q_out)

    @pl.when(j == 1)
    def _():
        rope_to(k_out)

    @pl.when(j == 2)
    def _():
        v_out[...] = z

    @pl.when(j == 3)
    def _():
        u_scr[...] = jax.nn.gelu(z)

    @pl.when(j == 4)
    def _():
        vg = jax.nn.gelu(z)
        xc = vg - jnp.mean(vg, axis=-1, keepdims=True)
        vn = xc * lax.rsqrt(jnp.mean(xc * xc, axis=-1, keepdims=True) + NORM_EPS) * gvn_ref[...]
        if with_vn:
            vn_out[...] = vn
        vnb = vn.astype(BF16)
        for c in range(tm // chunk):
            rows = slice(c * chunk, (c + 1) * chunk)
            for g in range(GMLP_GROUPS):
                cols = slice(g * GMLP_GROUP_DIM, (g + 1) * GMLP_GROUP_DIM)
                s = _dot(wsp_ref[g], vnb[rows, cols]) + bsp_ref[:, g:g + 1]
                gm_out[rows, cols] = (u_scr[rows, cols] * s).astype(BF16)


def _mixer(x, g_pre, w_in, cos_t, sa_t, sb_t, g_vn, wsp, bsp, *, tm, chunk, with_vn):
    m, d = x.shape
    n_seg = w_in.shape[1] // ATTN_WIDTH
    t_blocks = cos_t.shape[0] // tm
    tab_spec = pl.BlockSpec((tm, HEAD_DIM), lambda i, j: (i % t_blocks, 0))
    row_spec = lambda w: pl.BlockSpec((tm, w), lambda i, j: (i, 0))
    out_shape = [jax.ShapeDtypeStruct((m, d), BF16),
                 jax.ShapeDtypeStruct((m, ATTN_WIDTH), F32),
                 jax.ShapeDtypeStruct((m, ATTN_WIDTH), F32),
                 jax.ShapeDtypeStruct((m, ATTN_WIDTH), F32),
                 jax.ShapeDtypeStruct((m, GMLP_WIDTH), BF16)]
    out_specs = [row_spec(d), row_spec(ATTN_WIDTH), row_spec(ATTN_WIDTH), row_spec(ATTN_WIDTH),
                 row_spec(GMLP_WIDTH)]
    if with_vn:
        out_shape.append(jax.ShapeDtypeStruct((m, GMLP_WIDTH), F32))
        out_specs.append(row_spec(GMLP_WIDTH))
    return pl.pallas_call(
        functools.partial(_mixer_kernel, chunk=chunk, with_vn=with_vn),
        grid=(m // tm, n_seg),
        in_specs=[row_spec(d),
                  pl.BlockSpec((1, d), lambda i, j: (0, 0)),
                  pl.BlockSpec((d, ATTN_WIDTH), lambda i, j: (0, j)),
                  tab_spec, tab_spec, tab_spec,
                  pl.BlockSpec((1, GMLP_WIDTH), lambda i, j: (0, 0)),
                  pl.BlockSpec((GMLP_GROUPS, chunk, chunk), lambda i, j: (0, 0, 0)),
                  pl.BlockSpec((chunk, GMLP_GROUPS), lambda i, j: (0, 0))],
        out_specs=out_specs,
        out_shape=out_shape,
        scratch_shapes=[pltpu.VMEM((tm, GMLP_WIDTH), F32)],
        compiler_params=_params("parallel", "arbitrary"),
        name="mixer",
    )(x, g_pre, w_in, cos_t, sa_t, sb_t, g_vn, wsp, bsp)


def _moba_prompt_kernel(q_ref, k_ref, v_ref, o_ref, kb_scr, vb_scr, s_scr):
    seq = q_ref.shape[0]
    nblk = seq // MOBA_BLOCK
    blk = MOBA_BLOCK
    scale = HEAD_DIM ** -0.5
    kb_scr[...] = k_ref[...].astype(BF16)
    vb_scr[...] = v_ref[...].astype(BF16)
    kmean = [jnp.mean(k_ref[n * blk:(n + 1) * blk, :], axis=0, keepdims=True) for n in range(nblk)]
    lane = lax.broadcasted_iota(jnp.int32, (blk, HEAD_DIM), 1)
    row_i = lax.broadcasted_iota(jnp.int32, (blk, blk), 0)
    col_i = lax.broadcasted_iota(jnp.int32, (blk, blk), 1)
    causal = col_i <= row_i

    for j in range(nblk):
        rows = slice(j * blk, (j + 1) * blk)
        qj = q_ref[rows, :]
        qb = qj.astype(BF16)
        gates = [jnp.sum(qj * kmean[n], axis=-1, keepdims=True) for n in range(j)]
        gate_l = jnp.zeros((blk, HEAD_DIM), F32)
        for n in range(j):
            gate_l = jnp.where(lane == n, gates[n], gate_l)
        rank = jnp.zeros((blk, HEAD_DIM), jnp.int32)
        for n in range(j):
            beats = (gates[n] > gate_l) | ((gates[n] == gate_l) & (lane > n))
            rank = rank + beats.astype(jnp.int32)
        keep = jnp.where(rank < MOBA_TOPK, 1.0, 0.0)

        m_run = None
        for n in range(j + 1):
            s = _dot_nt(qb, kb_scr[n * blk:(n + 1) * blk, :]) * scale
            if n == j:
                s = jnp.where(causal, s, NEG_INF)
            else:
                s = jnp.where(keep[:, n:n + 1] > 0.5, s, NEG_INF)
            s_scr[:, n * blk:(n + 1) * blk] = s
            m_n = jnp.max(s, axis=-1, keepdims=True)
            m_run = m_n if m_run is None else jnp.maximum(m_run, m_n)
        l_run = jnp.zeros((blk, 1), F32)
        acc = jnp.zeros((blk, HEAD_DIM), F32)
        for n in range(j + 1):
            p = jnp.exp(s_scr[:, n * blk:(n + 1) * blk] - m_run)
            l_run = l_run + jnp.sum(p, axis=-1, keepdims=True)
            acc = acc + _dot(p.astype(BF16), vb_scr[n * blk:(n + 1) * blk, :])
        o_ref[rows, :] = (acc / l_run).astype(BF16)


def _moba_prompt(q, k, v, *, batch, seq):
    spec = pl.BlockSpec((seq, HEAD_DIM), lambda b, h: (b, h))
    return pl.pallas_call(
        _moba_prompt_kernel,
        grid=(batch, N_HEADS),
        in_specs=[spec, spec, spec],
        out_specs=spec,
        out_shape=jax.ShapeDtypeStruct(q.shape, BF16),
        scratch_shapes=[pltpu.VMEM((seq, HEAD_DIM), BF16), pltpu.VMEM((seq, HEAD_DIM), BF16),
                        pltpu.VMEM((MOBA_BLOCK, seq), F32)],
        compiler_params=_params("parallel", "parallel"),
        name="moba_prompt",
    )(q, k, v)


def _kmean_kernel(pt_ref, p0_ref, p1_ref, o_ref):
    n = pl.program_id(1)
    tot = jnp.sum(p0_ref[...], axis=0, keepdims=True) + jnp.sum(p1_ref[...], axis=0, keepdims=True)
    o_ref[pl.ds(n, 1), :] = tot * (1.0 / MOBA_BLOCK)


def _past_kmean(cache_k2, page_table, *, n_past_blocks):
    dec_b = page_table.shape[0]
    page = cache_k2.shape[1]
    width = cache_k2.shape[2]
    grid_spec = pltpu.PrefetchScalarGridSpec(
        num_scalar_prefetch=1,
        grid=(dec_b, n_past_blocks),
        in_specs=[pl.BlockSpec((None, page, width), lambda b, n, pt: (pt[b, 2 * n], 0, 0)),
                  pl.BlockSpec((None, page, width), lambda b, n, pt: (pt[b, 2 * n + 1], 0, 0))],
        out_specs=pl.BlockSpec((None, n_past_blocks, width), lambda b, n, pt: (b, 0, 0)),
    )
    return pl.pallas_call(
        _kmean_kernel,
        grid_spec=grid_spec,
        out_shape=jax.ShapeDtypeStruct((dec_b, n_past_blocks, width), F32),
        compiler_params=_params("parallel", "arbitrary"),
        name="past_kmean",
    )(page_table, cache_k2, cache_k2)


def _select_kernel(q_ref, km_ref, kn_ref, sel_ref, gate_scr, *, dec_s, n_past):
    dec_b = q_ref.shape[0]
    gate_scr[...] = jnp.full(gate_scr.shape, NEG_INF, F32)
    for b in range(dec_b):
        own_mean = jnp.sum(kn_ref[b], axis=0, keepdims=True) * (1.0 / MOBA_BLOCK)
        for h in range(N_HEADS):
            cols = slice(h * HEAD_DIM, (h + 1) * HEAD_DIM)
            km = km_ref[b, :, cols]
            for i in range(dec_s):
                col = (b * N_HEADS + h) * dec_s + i
                qi = q_ref[b, i:i + 1, cols]
                gate_scr[0:n_past, col:col + 1] = jnp.sum(km * qi, axis=-1, keepdims=True)
                gate_scr[n_past:n_past + 1, col:col + 1] = jnp.sum(qi * own_mean[:, cols], axis=-1,
                                                                   keepdims=True)
    gate = gate_scr[...]
    blk = lax.broadcasted_iota(jnp.int32, gate.shape, 0)
    gate = jnp.where(blk < n_past, gate, NEG_INF)
    out_row = lax.broadcasted_iota(jnp.int32, sel_ref.shape, 0)
    sel = jnp.zeros(sel_ref.shape, jnp.int32)
    for t in range(MOBA_TOPK):
        best = jnp.max(gate, axis=0, keepdims=True)
        idx = jnp.min(jnp.where(gate == best, blk, gate.shape[0]), axis=0, keepdims=True)
        sel = jnp.where(out_row == t, idx, sel)
        gate = jnp.where(blk == idx, TAKEN, gate)
    sel_ref[...] = sel


def _select_blocks(q3, kmean, kn3, *, n_past):
    dec_b, dec_s, width = q3.shape
    cols = dec_b * N_HEADS * dec_s
    gate_rows = 8 * (-(-(n_past + 1) // 8))
    return pl.pallas_call(
        functools.partial(_select_kernel, dec_s=dec_s, n_past=n_past),
        out_shape=jax.ShapeDtypeStruct((8, cols), jnp.int32),
        scratch_shapes=[pltpu.VMEM((gate_rows, cols), F32)],
        compiler_params=pltpu.CompilerParams(vmem_limit_bytes=VMEM_LIMIT_BYTES),
        name="select_blocks",
    )(q3, kmean, kn3)


def _moba_sample_kernel(pt_ref, sel_ref, q_ref, kn_ref, vn_ref, *refs, dec_s, n_slots):
    n_pages = 2 * n_slots
    k_refs = refs[:n_pages]
    v_refs = refs[n_pages:2 * n_pages]
    o_ref = refs[2 * n_pages]
    page = k_refs[0].shape[0]
    scale = HEAD_DIM ** -0.5
    rows = q_ref.shape[0]
    qb = q_ref[...].astype(BF16)
    row = lax.broadcasted_iota(jnp.int32, (rows, page), 0)
    s_list = []
    m_run = None
    for c in range(n_pages):
        s = _dot_nt(qb, k_refs[c][...].astype(BF16)) * scale
        s = jnp.where(row == c // (2 * MOBA_TOPK), s, NEG_INF)
        s_list.append(s)
        m_c = jnp.max(s, axis=-1, keepdims=True)
        m_run = m_c if m_run is None else jnp.maximum(m_run, m_c)
    s_own = _dot_nt(qb, kn_ref[...].astype(BF16)) * scale
    r_o = lax.broadcasted_iota(jnp.int32, (rows, rows), 0)
    c_o = lax.broadcasted_iota(jnp.int32, (rows, rows), 1)
    s_own = jnp.where((c_o <= r_o) & (c_o < dec_s), s_own, NEG_INF)
    m_run = jnp.maximum(m_run, jnp.max(s_own, axis=-1, keepdims=True))
    p_own = jnp.exp(s_own - m_run)
    l_run = jnp.sum(p_own, axis=-1, keepdims=True)
    acc = _dot(p_own.astype(BF16), vn_ref[...].astype(BF16))
    for c in range(n_pages):
        p = jnp.exp(s_list[c] - m_run)
        l_run = l_run + jnp.sum(p, axis=-1, keepdims=True)
        acc = acc + _dot(p.astype(BF16), v_refs[c][...].astype(BF16))
    o_ref[...] = (acc / l_run).astype(BF16)


def _moba_sample(q3, kn3, vn3, cache_k2, cache_v2, page_table, sel_flat, *, dec_s):
    dec_b, rows, width = q3.shape
    page = cache_k2.shape[1]
    n_slots = dec_s * MOBA_TOPK

    def page_spec(c):
        slot, half = c // 2, c % 2

        def idx(b, h, pt, sel):
            blk = sel[(b * N_HEADS + h) * n_slots + slot]
            return (pt[b, 2 * blk + half], 0, h)
        return pl.BlockSpec((None, page, HEAD_DIM), idx)

    new_spec = pl.BlockSpec((None, rows, HEAD_DIM), lambda b, h, pt, sel: (b, 0, h))
    page_specs = [page_spec(c) for c in range(2 * n_slots)]
    grid_spec = pltpu.PrefetchScalarGridSpec(
        num_scalar_prefetch=2,
        grid=(dec_b, N_HEADS),
        in_specs=[new_spec, new_spec, new_spec] + page_specs + page_specs,
        out_specs=new_spec,
    )
    return pl.pallas_call(
        functools.partial(_moba_sample_kernel, dec_s=dec_s, n_slots=n_slots),
        grid_spec=grid_spec,
        out_shape=jax.ShapeDtypeStruct(q3.shape, BF16),
        compiler_params=_params("parallel", "parallel"),
        name="moba_sample",
    )(page_table, sel_flat, q3, kn3, vn3, *([cache_k2] * (2 * n_slots)), *([cache_v2] * (2 * n_slots)))


def _gated_mix_kernel(a_ref, b_ref, h_ref, wa_ref, wb_ref, wga_ref, wgb_ref, o_ref):
    h = h_ref[...]
    ga = jax.nn.sigmoid(_dot(h, wga_ref[...]))
    gb = jax.nn.sigmoid(_dot(h, wgb_ref[...]))
    a = _dot(a_ref[...], wa_ref[...])
    b = _dot(b_ref[...], wb_ref[...])
    o_ref[...] = (ga * a + gb * b).astype(BF16)


def _gated_mix(attn, gm, h, w_a, w_b, w_gate, *, tm, tn):
    m, d = h.shape
    n_col = d // tn
    return pl.pallas_call(
        _gated_mix_kernel,
        grid=(n_col, m // tm),
        in_specs=[pl.BlockSpec((tm, attn.shape[1]), lambda j, i: (i, 0)),
                  pl.BlockSpec((tm, gm.shape[1]), lambda j, i: (i, 0)),
                  pl.BlockSpec((tm, d), lambda j, i: (i, 0)),
                  pl.BlockSpec((w_a.shape[0], tn), lambda j, i: (0, j)),
                  pl.BlockSpec((w_b.shape[0], tn), lambda j, i: (0, j)),
                  pl.BlockSpec((d, tn), lambda j, i: (0, j)),
                  pl.BlockSpec((d, tn), lambda j, i: (0, j + n_col))],
        out_specs=pl.BlockSpec((tm, tn), lambda j, i: (i, j)),
        out_shape=jax.ShapeDtypeStruct((m, d), BF16),
        compiler_params=_params("parallel", "parallel"),
        name="gated_mix",
    )(attn, gm, h, w_a, w_b, w_gate, w_gate)


def _mix_out_kernel(mix_ref, x_ref, wo_ref, gpost_ref, gpre_ref, x1_ref, xn_ref):
    mix = _dot(mix_ref[...], wo_ref[...])
    x1 = x_ref[...] + _rms(mix, gpost_ref[...])
    x1_ref[...] = x1
    xn_ref[...] = _rms(x1, gpre_ref[...]).astype(BF16)


def _mix_out(mixin, x, w_o, g_post, g_pre_ffn, *, tm):
    m, d = x.shape
    row = pl.BlockSpec((tm, d), lambda i: (i, 0))
    vec = pl.BlockSpec((1, d), lambda i: (0, 0))
    return pl.pallas_call(
        _mix_out_kernel,
        grid=(m // tm,),
        in_specs=[row, row, pl.BlockSpec((d, d), lambda i: (0, 0)), vec, vec],
        out_specs=[row, row],
        out_shape=[jax.ShapeDtypeStruct((m, d), F32), jax.ShapeDtypeStruct((m, d), BF16)],
        compiler_params=_params("parallel"),
        name="mix_out",
    )(mixin, x, w_o, g_post, g_pre_ffn)


def _ffn_in_kernel(x_ref, wa_ref, wg_ref, o_ref):
    x = x_ref[...]
    a = _dot(x, wa_ref[...])
    g = _dot(x, wg_ref[...])
    o_ref[...] = (jax.nn.silu(a) * g).astype(BF16)


def _ffn_in(xn, w_ffn_in, *, tm, tn):
    m, d = xn.shape
    hidden = w_ffn_in.shape[1] // 2
    n_col = hidden // tn
    return pl.pallas_call(
        _ffn_in_kernel,
        grid=(n_col, m // tm),
        in_specs=[pl.BlockSpec((tm, d), lambda j, i: (i, 0)),
                  pl.BlockSpec((d, tn), lambda j, i: (0, j)),
                  pl.BlockSpec((d, tn), lambda j, i: (0, j + n_col))],
        out_specs=pl.BlockSpec((tm, tn), lambda j, i: (i, j)),
        out_shape=jax.ShapeDtypeStruct((m, hidden), BF16),
        compiler_params=_params("parallel", "parallel"),
        name="ffn_in",
    )(xn, w_ffn_in, w_ffn_in)


def _ffn_out_kernel(hm_ref, w_ref, x1_ref, gpost_ref, gple_ref, x2_ref, xg_ref, acc_scr):
    k = pl.program_id(1)

    @pl.when(k == 0)
    def _():
        acc_scr[...] = jnp.zeros(acc_scr.shape, F32)

    acc_scr[...] += _dot(hm_ref[...], w_ref[...])

    @pl.when(k == pl.num_programs(1) - 1)
    def _():
        x2 = x1_ref[...] + _rms(acc_scr[...], gpost_ref[...])
        x2_ref[...] = x2
        xg_ref[...] = _rms(x2, gple_ref[...]).astype(BF16)


def _ffn_out(hmid, w_ffn_out, x1, g_post_ffn, g_ple, *, tm, tk):
    m, d = x1.shape
    hidden = hmid.shape[1]
    row = pl.BlockSpec((tm, d), lambda i, k: (i, 0))
    vec = pl.BlockSpec((1, d), lambda i, k: (0, 0))
    return pl.pallas_call(
        _ffn_out_kernel,
        grid=(m // tm, hidden // tk),
        in_specs=[pl.BlockSpec((tm, tk), lambda i, k: (i, k)),
                  pl.BlockSpec((tk, d), lambda i, k: (k, 0)),
                  row, vec, vec],
        out_specs=[row, row],
        out_shape=[jax.ShapeDtypeStruct((m, d), F32), jax.ShapeDtypeStruct((m, d), BF16)],
        scratch_shapes=[pltpu.VMEM((tm, d), F32)],
        compiler_params=_params("parallel", "arbitrary"),
        name="ffn_out",
    )(hmid, w_ffn_out, x1, g_post_ffn, g_ple)


def _ple_kernel(x2_ref, xg_ref, p_ref, wg_ref, wp_ref, o_ref):
    gate = jax.nn.sigmoid(_dot(xg_ref[...], wg_ref[...]))
    o_ref[...] = x2_ref[...] + _dot(p_ref[...].astype(BF16), wp_ref[...]) * gate


def _ple(x2, xg, p, w_ple_gate, w_ple, *, tm, tn):
    m, d = x2.shape
    return pl.pallas_call(
        _ple_kernel,
        grid=(d // tn, m // tm),
        in_specs=[pl.BlockSpec((tm, tn), lambda j, i: (i, j)),
                  pl.BlockSpec((tm, d), lambda j, i: (i, 0)),
                  pl.BlockSpec((tm, p.shape[1]), lambda j, i: (i, 0)),
                  pl.BlockSpec((d, tn), lambda j, i: (0, j)),
                  pl.BlockSpec((w_ple.shape[0], tn), lambda j, i: (0, j))],
        out_specs=pl.BlockSpec((tm, tn), lambda j, i: (i, j)),
        out_shape=jax.ShapeDtypeStruct((m, d), F32),
        compiler_params=_params("parallel", "parallel"),
        name="ple",
    )(x2, xg, p, w_ple_gate, w_ple)


def _layer_tail(x, h, attn, gm, p, w, *, tm):
    m, d = x.shape
    hidden = w["w_ffn_out"].shape[0]
    mixin = _gated_mix(attn, gm, h, w["w_a_out"], w["w_b_out"], w["w_gate"], tm=min(m, 1024), tn=512)
    x1, xn = _mix_out(mixin, x, w["w_o"], w["g_post_mix"], w["g_pre_ffn"], tm=min(m, 256))
    hmid = _ffn_in(xn, w["w_ffn_in"], tm=tm, tn=hidden // 4)
    x2, xg = _ffn_out(hmid, w["w_ffn_out"], x1, w["g_post_ffn"], w["g_ple"], tm=tm, tk=hidden // 4)
    return _ple(x2, xg, p, w["w_ple_gate"], w["w_ple"], tm=min(m, 1024), tn=512)


def _rope_tables(pos):
    freqs = jnp.power(jnp.float32(ROPE_THETA), -2.0 * jnp.arange(ROPE_HALF, dtype=F32) / ROPE_DIM)
    ang = pos.astype(F32)[:, None] * freqs[None, :]
    cos, sin = jnp.cos(ang), jnp.sin(ang)
    n = pos.shape[0]
    c = jnp.concatenate([cos, cos, jnp.ones((n, HEAD_DIM - ROPE_DIM), F32)], axis=1)
    sa = jnp.concatenate([-sin, jnp.zeros((n, HEAD_DIM - ROPE_HALF), F32)], axis=1)
    sb = jnp.concatenate([jnp.zeros((n, ROPE_HALF), F32), sin, jnp.zeros((n, HEAD_DIM - ROPE_DIM), F32)], axis=1)
    return c, sa, sb


def kernel(x_prompt, x_sample, cache_k, cache_v, page_table, p_prompt, p_sample, g_pre_mix, w_in, g_vnorm, w_spatial, b_spatial, w_a_out, w_b_out, w_gate, w_o, g_post_mix, g_pre_ffn, w_ffn_in, w_ffn_out, g_post_ffn, g_ple, w_ple_gate, w_ple):
    batch, seq, d = x_prompt.shape
    dec_b, dec_s, _ = x_sample.shape
    depth = w_in.shape[0]
    page = cache_k.shape[2]
    past_len = page_table.shape[1] * page
    n_past = past_len // MOBA_BLOCK
    assert depth == 1 and seq % MOBA_BLOCK == 0 and past_len % MOBA_BLOCK == 0 and MOBA_BLOCK == 2 * page
    assert dec_s <= GMLP_CHUNK and n_past >= MOBA_TOPK
    l = 0

    w = {
        "w_a_out": w_a_out[l].astype(BF16), "w_b_out": w_b_out[l].astype(BF16),
        "w_gate": w_gate[l].astype(BF16), "w_o": w_o[l].astype(BF16),
        "w_ffn_in": w_ffn_in[l].astype(BF16), "w_ffn_out": w_ffn_out[l].astype(BF16),
        "w_ple_gate": w_ple_gate[l].astype(BF16), "w_ple": w_ple[l].astype(BF16),
        "g_post_mix": g_post_mix[l][None], "g_pre_ffn": g_pre_ffn[l][None],
        "g_post_ffn": g_post_ffn[l][None], "g_ple": g_ple[l][None],
    }
    w_in_b = w_in[l].astype(BF16)
    g_pre = g_pre_mix[l][None]
    g_vn = g_vnorm[l][None]

    w_tril = jnp.tril(w_spatial[l])
    wsp_p = w_tril.astype(BF16)
    bsp_p = b_spatial[l].T
    eye_b = jnp.eye(dec_b, dtype=F32)
    wsp_s = jnp.einsum("ab,gts->gatbs", eye_b, w_tril[:, :dec_s, :dec_s]).reshape(
        GMLP_GROUPS, dec_b * dec_s, dec_b * dec_s).astype(BF16)
    bsp_s = jnp.tile(b_spatial[l][:, :dec_s].T, (dec_b, 1))

    mp = batch * seq
    xp = x_prompt.reshape(mp, d)
    cp, sap, sbp = _rope_tables(jnp.arange(seq, dtype=jnp.int32))
    hp, qp, kp, vp, gmp = _mixer(xp, g_pre, w_in_b, cp, sap, sbp, g_vn, wsp_p, bsp_p,
                                 tm=512, chunk=GMLP_CHUNK, with_vn=False)
    ap = _moba_prompt(qp, kp, vp, batch=batch, seq=seq)
    yp = _layer_tail(xp, hp, ap, gmp, p_prompt[l].reshape(mp, -1), w, tm=512)

    ms = dec_b * dec_s
    xs = x_sample.reshape(ms, d)
    pos_s = past_len + jnp.arange(dec_s, dtype=jnp.int32)
    cs, sas, sbs = (jnp.tile(t, (dec_b, 1)) for t in _rope_tables(pos_s))
    hs, qs, ks, vs, gms, vns = _mixer(xs, g_pre, w_in_b, cs, sas, sbs, g_vn, wsp_s, bsp_s,
                                      tm=ms, chunk=ms, with_vn=True)
    cache_k2 = cache_k[l].reshape(cache_k.shape[1], page, ATTN_WIDTH)
    cache_v2 = cache_v[l].reshape(cache_v.shape[1], page, ATTN_WIDTH)
    q3 = qs.reshape(dec_b, dec_s, ATTN_WIDTH)
    kn3 = ks.reshape(dec_b, dec_s, ATTN_WIDTH)
    vn3 = vs.reshape(dec_b, dec_s, ATTN_WIDTH)
    kmean = _past_kmean(cache_k2, page_table, n_past_blocks=n_past)
    sel = _select_blocks(q3, kmean, kn3, n_past=n_past)
    sel_flat = sel[:MOBA_TOPK].T.reshape(-1)
    pad = lambda t: jnp.pad(t, ((0, 0), (0, SAMPLE_ROWS - dec_s), (0, 0)))
    a_s = _moba_sample(pad(q3), pad(kn3), pad(vn3), cache_k2, cache_v2, page_table, sel_flat, dec_s=dec_s)
    a_s = a_s[:, :dec_s].reshape(ms, ATTN_WIDTH)
    ys = _layer_tail(xs, hs, a_s, gms, p_sample[l].reshape(ms, -1), w, tm=ms)

    return (yp.reshape(batch, seq, d), ys.reshape(dec_b, dec_s, d),
            kp.reshape(1, batch, seq, N_HEADS, HEAD_DIM), vp.reshape(1, batch, seq, N_HEADS, HEAD_DIM),
            ks.reshape(1, dec_b, dec_s, N_HEADS, HEAD_DIM), vs.reshape(1, dec_b, dec_s, N_HEADS, HEAD_DIM),
            vns.reshape(1, dec_b, dec_s, GMLP_WIDTH))
```

```python
import functools

import jax
import jax.numpy as jnp
from jax import lax
from jax.experimental import pallas as pl
from jax.experimental.pallas import tpu as pltpu

F32 = jnp.float32
BF16 = jnp.bfloat16

N_HEADS = 8
HEAD_DIM = 128
ATTN_WIDTH = N_HEADS * HEAD_DIM
MOBA_BLOCK = 256
MOBA_TOPK = 3
ROPE_THETA = 500000.0
ROPE_DIM = HEAD_DIM // 4
ROPE_HALF = ROPE_DIM // 2
GMLP_GROUPS = 8
GMLP_CHUNK = 128
GMLP_WIDTH = 1024
GMLP_GROUP_DIM = GMLP_WIDTH // GMLP_GROUPS
NORM_EPS = 1e-6
NEG_INF = -1e30
TAKEN = -3e38
SAMPLE_ROWS = 16

VMEM_LIMIT_BYTES = 56 * 1024 * 1024


def _params(*semantics):
    return pltpu.CompilerParams(dimension_semantics=semantics, vmem_limit_bytes=VMEM_LIMIT_BYTES)


def _rms(x, g):
    return x * lax.rsqrt(jnp.mean(x * x, axis=-1, keepdims=True) + NORM_EPS) * g


def _dot(a, b):
    return jnp.dot(a, b, preferred_element_type=F32)


def _dot_nt(a, b):
    return lax.dot_general(a, b, (((1,), (1,)), ((), ())), preferred_element_type=F32)


def _mixer_kernel(x_ref, g_ref, w_ref, cos_ref, sa_ref, sb_ref, gvn_ref, wsp_ref, bsp_ref,
                  h_out, q_out, k_out, v_out, gm_out, *rest, chunk, with_vn):
    if with_vn:
        vn_out, u_scr = rest
    else:
        (u_scr,) = rest
    j = pl.program_id(1)
    tm = x_ref.shape[0]

    @pl.when(j == 0)
    def _():
        h_out[...] = _rms(x_ref[...], g_ref[...]).astype(BF16)

    z = _dot(h_out[...], w_ref[...])

    def rope_to(out_ref):
        c, sa, sb = cos_ref[...], sa_ref[...], sb_ref[...]
        for hd in range(N_HEADS):
            sl = slice(hd * HEAD_DIM, (hd + 1) * HEAD_DIM)
            zs = z[:, sl]
            out_ref[:, sl] = (zs * c + pltpu.roll(zs, HEAD_DIM - ROPE_HALF, 1) * sa
                              + pltpu.roll(zs, ROPE_HALF, 1) * sb)

    @pl.when(j == 0)
    def _():
        rope_to(q_out)

    @pl.when(j == 1)
    def _():
        rope_to(k_out)

    @pl.when(j == 2)
    def _():
        v_out[...] = z

    @pl.when(j == 3)
    def _():
        u_scr[...] = jax.nn.gelu(z)

    @pl.when(j == 4)
    def _():
        vg = jax.nn.gelu(z)
        xc = vg - jnp.mean(vg, axis=-1, keepdims=True)
        vn = xc * lax.rsqrt(jnp.mean(xc * xc, axis=-1, keepdims=True) + NORM_EPS) * gvn_ref[...]
        if with_vn:
            vn_out[...] = vn
        vnb = vn.astype(BF16)
        for c in range(tm // chunk):
            rows = slice(c * chunk, (c + 1) * chunk)
            for g in range(GMLP_GROUPS):
                cols = slice(g * GMLP_GROUP_DIM, (g + 1) * GMLP_GROUP_DIM)
                s = _dot(wsp_ref[g], vnb[rows, cols]) + bsp_ref[:, g:g + 1]
                gm_out[rows, cols] = (u_scr[rows, cols] * s).astype(BF16)


def _mixer(x, g_pre, w_in, cos_t, sa_t, sb_t, g_vn, wsp, bsp, *, tm, chunk, with_vn):
    m, d = x.shape
    n_seg = w_in.shape[1] // ATTN_WIDTH
    t_blocks = cos_t.shape[0] // tm
    tab_spec = pl.BlockSpec((tm, HEAD_DIM), lambda i, j: (i % t_blocks, 0))
    row_spec = lambda w: pl.BlockSpec((tm, w), lambda i, j: (i, 0))
    out_shape = [jax.ShapeDtypeStruct((m, d), BF16),
                 jax.ShapeDtypeStruct((m, ATTN_WIDTH), F32),
                 jax.ShapeDtypeStruct((m, ATTN_WIDTH), F32),
                 jax.ShapeDtypeStruct((m, ATTN_WIDTH), F32),
                 jax.ShapeDtypeStruct((m, GMLP_WIDTH), BF16)]
    out_specs = [row_spec(d), row_spec(ATTN_WIDTH), row_spec(ATTN_WIDTH), row_spec(ATTN_WIDTH),
                 row_spec(GMLP_WIDTH)]
    if with_vn:
        out_shape.append(jax.ShapeDtypeStruct((m, GMLP_WIDTH), F32))
        out_specs.append(row_spec(GMLP_WIDTH))
    return pl.pallas_call(
        functools.partial(_mixer_kernel, chunk=chunk, with_vn=with_vn),
        grid=(m // tm, n_seg),
        in_specs=[row_spec(d),
                  pl.BlockSpec((1, d), lambda i, j: (0, 0)),
                  pl.BlockSpec((d, ATTN_WIDTH), lambda i, j: (0, j)),
                  tab_spec, tab_spec, tab_spec,
                  pl.BlockSpec((1, GMLP_WIDTH), lambda i, j: (0, 0)),
                  pl.BlockSpec((GMLP_GROUPS, chunk, chunk), lambda i, j: (0, 0, 0)),
                  pl.BlockSpec((chunk, GMLP_GROUPS), lambda i, j: (0, 0))],
        out_specs=out_specs,
        out_shape=out_shape,
        scratch_shapes=[pltpu.VMEM((tm, GMLP_WIDTH), F32)],
        compiler_params=_params("parallel", "arbitrary"),
        name="mixer",
    )(x, g_pre, w_in, cos_t, sa_t, sb_t, g_vn, wsp, bsp)


def _moba_prompt_kernel(q_ref, k_ref, v_ref, o_ref, kb_scr, vb_scr, s_scr):
    seq = q_ref.shape[0]
    nblk = seq // MOBA_BLOCK
    blk = MOBA_BLOCK
    scale = HEAD_DIM ** -0.5
    kb_scr[...] = k_ref[...].astype(BF16)
    vb_scr[...] = v_ref[...].astype(BF16)
    kmean = [jnp.mean(k_ref[n * blk:(n + 1) * blk, :], axis=0, keepdims=True) for n in range(nblk)]
    lane = lax.broadcasted_iota(jnp.int32, (blk, HEAD_DIM), 1)
    row_i = lax.broadcasted_iota(jnp.int32, (blk, blk), 0)
    col_i = lax.broadcasted_iota(jnp.int32, (blk, blk), 1)
    causal = col_i <= row_i

    for j in range(nblk):
        rows = slice(j * blk, (j + 1) * blk)
        qj = q_ref[rows, :]
        qb = qj.astype(BF16)
        gates = [jnp.sum(qj * kmean[n], axis=-1, keepdims=True) for n in range(j)]
        gate_l = jnp.zeros((blk, HEAD_DIM), F32)
        for n in range(j):
            gate_l = jnp.where(lane == n, gates[n], gate_l)
        rank = jnp.zeros((blk, HEAD_DIM), jnp.int32)
        for n in range(j):
            beats = (gates[n] > gate_l) | ((gates[n] == gate_l) & (lane > n))
            rank = rank + beats.astype(jnp.int32)
        keep = jnp.where(rank < MOBA_TOPK, 1.0, 0.0)

        m_run = None
        for n in range(j + 1):
            s = _dot_nt(qb, kb_scr[n * blk:(n + 1) * blk, :]) * scale
            if n == j:
                s = jnp.where(causal, s, NEG_INF)
            else:
                s = jnp.where(keep[:, n:n + 1] > 0.5, s, NEG_INF)
            s_scr[:, n * blk:(n + 1) * blk] = s
            m_n = jnp.max(s, axis=-1, keepdims=True)
            m_run = m_n if m_run is None else jnp.maximum(m_run, m_n)
        l_run = jnp.zeros((blk, 1), F32)
        acc = jnp.zeros((blk, HEAD_DIM), F32)
        for n in range(j + 1):
            p = jnp.exp(s_scr[:, n * blk:(n + 1) * blk] - m_run)
            l_run = l_run + jnp.sum(p, axis=-1, keepdims=True)
            acc = acc + _dot(p.astype(BF16), vb_scr[n * blk:(n + 1) * blk, :])
        o_ref[rows, :] = (acc / l_run).astype(BF16)


def _moba_prompt(q, k, v, *, batch, seq):
    spec = pl.BlockSpec((seq, HEAD_DIM), lambda b, h: (b, h))
    return pl.pallas_call(
        _moba_prompt_kernel,
        grid=(batch, N_HEADS),
        in_specs=[spec, spec, spec],
        out_specs=spec,
        out_shape=jax.ShapeDtypeStruct(q.shape, BF16),
        scratch_shapes=[pltpu.VMEM((seq, HEAD_DIM), BF16), pltpu.VMEM((seq, HEAD_DIM), BF16),
                        pltpu.VMEM((MOBA_BLOCK, seq), F32)],
        compiler_params=_params("parallel", "parallel"),
        name="moba_prompt",
    )(q, k, v)


KMEAN_PAGES_PER_STEP = 16


def _kmean_kernel(pt_ref, *refs):
    page_refs, o_ref = refs[:-1], refs[-1]
    n_blk = len(page_refs) // 2
    g = pl.program_id(1)
    page_rows = page_refs[0].shape[0] // N_HEADS

    def page_sum(ref):
        return jnp.sum(ref[...].reshape(page_rows, N_HEADS, HEAD_DIM), axis=0)

    for t in range(n_blk):
        tot = (page_sum(page_refs[2 * t]) + page_sum(page_refs[2 * t + 1])) * (1.0 / MOBA_BLOCK)
        for h in range(N_HEADS):
            o_ref[h, pl.ds(g * n_blk + t, 1), :] = tot[h:h + 1, :]


def _past_kmean(cache_k2, page_table, *, n_past_blocks):
    dec_b, n_pages = page_table.shape
    pps = KMEAN_PAGES_PER_STEP
    assert n_pages % pps == 0

    def page_spec(c):
        return pl.BlockSpec((None,) + cache_k2.shape[1:], lambda b, g, pt: (pt[b, g * pps + c], 0, 0))

    grid_spec = pltpu.PrefetchScalarGridSpec(
        num_scalar_prefetch=1,
        grid=(dec_b, n_pages // pps),
        in_specs=[page_spec(c) for c in range(pps)],
        out_specs=pl.BlockSpec((None, N_HEADS, n_past_blocks, HEAD_DIM), lambda b, g, pt: (b, 0, 0, 0)),
    )
    return pl.pallas_call(
        _kmean_kernel,
        grid_spec=grid_spec,
        out_shape=jax.ShapeDtypeStruct((dec_b, N_HEADS, n_past_blocks, HEAD_DIM), F32),
        compiler_params=_params("parallel", "arbitrary"),
        name="past_kmean",
    )(page_table, *([cache_k2] * pps))


def _select_kernel(q_ref, km_ref, kn_ref, sel_ref, gate_scr, *, dec_s, n_past):
    dec_b = q_ref.shape[0]
    gate_scr[...] = jnp.full(gate_scr.shape, NEG_INF, F32)
    for b in range(dec_b):
        own_mean = jnp.sum(kn_ref[b], axis=0, keepdims=True) * (1.0 / MOBA_BLOCK)
        for h in range(N_HEADS):
            cols = slice(h * HEAD_DIM, (h + 1) * HEAD_DIM)
            km = km_ref[b, h]
            for i in range(dec_s):
                col = (b * N_HEADS + h) * dec_s + i
                qi = q_ref[b, i:i + 1, cols]
                gate_scr[0:n_past, col:col + 1] = jnp.sum(km * qi, axis=-1, keepdims=True)
                gate_scr[n_past:n_past + 1, col:col + 1] = jnp.sum(qi * own_mean[:, cols], axis=-1,
                                                                   keepdims=True)
    gate = gate_scr[...]
    blk = lax.broadcasted_iota(jnp.int32, gate.shape, 0)
    gate = jnp.where(blk < n_past, gate, NEG_INF)
    out_row = lax.broadcasted_iota(jnp.int32, sel_ref.shape, 0)
    sel = jnp.zeros(sel_ref.shape, jnp.int32)
    for t in range(MOBA_TOPK):
        best = jnp.max(gate, axis=0, keepdims=True)
        idx = jnp.min(jnp.where(gate == best, blk, gate.shape[0]), axis=0, keepdims=True)
        sel = jnp.where(out_row == t, idx, sel)
        gate = jnp.where(blk == idx, TAKEN, gate)
    sel_ref[...] = sel


def _select_blocks(q3, kmean, kn3, *, n_past):
    dec_b, dec_s, width = q3.shape
    cols = dec_b * N_HEADS * dec_s
    gate_rows = 8 * (-(-(n_past + 1) // 8))
    return pl.pallas_call(
        functools.partial(_select_kernel, dec_s=dec_s, n_past=n_past),
        out_shape=jax.ShapeDtypeStruct((8, cols), jnp.int32),
        scratch_shapes=[pltpu.VMEM((gate_rows, cols), F32)],
        compiler_params=pltpu.CompilerParams(vmem_limit_bytes=VMEM_LIMIT_BYTES),
        name="select_blocks",
    )(q3, kmean, kn3)


def _moba_sample_kernel(pt_ref, sel_ref, q_ref, kn_ref, vn_ref, ck_hbm, cv_hbm, o_ref, kbuf, vbuf, sem,
                        *, dec_s, n_slots):
    n_pages = 2 * n_slots
    page = kbuf.shape[2]
    n_heads = pl.num_programs(1)
    step = pl.program_id(0) * n_heads + pl.program_id(1)
    n_steps = pl.num_programs(0) * n_heads

    def page_copies(step_idx, slot):
        bb, hh = step_idx // n_heads, step_idx % n_heads
        copies = []
        for c in range(n_pages):
            blk = sel_ref[step_idx * n_slots + c // 2]
            pg = pt_ref[bb, 2 * blk + c % 2]
            src = (pg, slice(None), hh, slice(None))
            copies.append(pltpu.make_async_copy(ck_hbm.at[src], kbuf.at[slot, c], sem.at[0, slot]))
            copies.append(pltpu.make_async_copy(cv_hbm.at[src], vbuf.at[slot, c], sem.at[1, slot]))
        return copies

    @pl.when(step == 0)
    def _():
        for cp in page_copies(step, 0):
            cp.start()

    @pl.when(step + 1 < n_steps)
    def _():
        for cp in page_copies(step + 1, (step + 1) % 2):
            cp.start()

    slot = step % 2
    for cp in page_copies(step, slot):
        cp.wait()
    k_refs = [kbuf.at[slot, c] for c in range(n_pages)]
    v_refs = [vbuf.at[slot, c] for c in range(n_pages)]
    scale = HEAD_DIM ** -0.5
    rows = q_ref.shape[0]
    qb = q_ref[...].astype(BF16)
    row = lax.broadcasted_iota(jnp.int32, (rows, page), 0)
    s_list = []
    m_run = None
    for c in range(n_pages):
        s = _dot_nt(qb, k_refs[c][...].astype(BF16)) * scale
        s = jnp.where(row == c // (2 * MOBA_TOPK), s, NEG_INF)
        s_list.append(s)
        m_c = jnp.max(s, axis=-1, keepdims=True)
        m_run = m_c if m_run is None else jnp.maximum(m_run, m_c)
    s_own = _dot_nt(qb, kn_ref[...].astype(BF16)) * scale
    r_o = lax.broadcasted_iota(jnp.int32, (rows, rows), 0)
    c_o = lax.broadcasted_iota(jnp.int32, (rows, rows), 1)
    s_own = jnp.where((c_o <= r_o) & (c_o < dec_s), s_own, NEG_INF)
    m_run = jnp.maximum(m_run, jnp.max(s_own, axis=-1, keepdims=True))
    p_own = jnp.exp(s_own - m_run)
    l_run = jnp.sum(p_own, axis=-1, keepdims=True)
    acc = _dot(p_own.astype(BF16), vn_ref[...].astype(BF16))
    for c in range(n_pages):
        p = jnp.exp(s_list[c] - m_run)
        l_run = l_run + jnp.sum(p, axis=-1, keepdims=True)
        acc = acc + _dot(p.astype(BF16), v_refs[c][...].astype(BF16))
    o_ref[...] = (acc / l_run).astype(BF16)


def _moba_sample(q3, kn3, vn3, cache_k2, cache_v2, page_table, sel_flat, *, dec_s):
    dec_b, rows, width = q3.shape
    page = cache_k2.shape[1] // N_HEADS
    n_slots = dec_s * MOBA_TOPK
    new_spec = pl.BlockSpec((None, rows, HEAD_DIM), lambda b, h, pt, sel: (b, 0, h))
    hbm_spec = pl.BlockSpec(memory_space=pl.ANY)
    grid_spec = pltpu.PrefetchScalarGridSpec(
        num_scalar_prefetch=2,
        grid=(dec_b, N_HEADS),
        in_specs=[new_spec, new_spec, new_spec, hbm_spec, hbm_spec],
        out_specs=new_spec,
        scratch_shapes=[pltpu.VMEM((2, 2 * n_slots, page, HEAD_DIM), F32),
                        pltpu.VMEM((2, 2 * n_slots, page, HEAD_DIM), F32),
                        pltpu.SemaphoreType.DMA((2, 2))],
    )
    return pl.pallas_call(
        functools.partial(_moba_sample_kernel, dec_s=dec_s, n_slots=n_slots),
        grid_spec=grid_spec,
        out_shape=jax.ShapeDtypeStruct(q3.shape, BF16),
        compiler_params=_params("arbitrary", "arbitrary"),
        name="moba_sample",
    )(page_table, sel_flat, q3, kn3, vn3,
      cache_k2.reshape(-1, page, N_HEADS, HEAD_DIM), cache_v2.reshape(-1, page, N_HEADS, HEAD_DIM))


def _gated_mix_kernel(a_ref, b_ref, h_ref, wa_ref, wb_ref, wga_ref, wgb_ref, o_ref):
    h = h_ref[...]
    ga = jax.nn.sigmoid(_dot(h, wga_ref[...]))
    gb = jax.nn.sigmoid(_dot(h, wgb_ref[...]))
    a = _dot(a_ref[...], wa_ref[...])
    b = _dot(b_ref[...], wb_ref[...])
    o_ref[...] = (ga * a + gb * b).astype(BF16)


def _gated_mix(attn, gm, h, w_a, w_b, w_gate, *, tm, tn):
    m, d = h.shape
    n_col = d // tn
    return pl.pallas_call(
        _gated_mix_kernel,
        grid=(n_col, m // tm),
        in_specs=[pl.BlockSpec((tm, attn.shape[1]), lambda j, i: (i, 0)),
                  pl.BlockSpec((tm, gm.shape[1]), lambda j, i: (i, 0)),
                  pl.BlockSpec((tm, d), lambda j, i: (i, 0)),
                  pl.BlockSpec((w_a.shape[0], tn), lambda j, i: (0, j)),
                  pl.BlockSpec((w_b.shape[0], tn), lambda j, i: (0, j)),
                  pl.BlockSpec((d, tn), lambda j, i: (0, j)),
                  pl.BlockSpec((d, tn), lambda j, i: (0, j + n_col))],
        out_specs=pl.BlockSpec((tm, tn), lambda j, i: (i, j)),
        out_shape=jax.ShapeDtypeStruct((m, d), BF16),
        compiler_params=_params("parallel", "parallel"),
        name="gated_mix",
    )(attn, gm, h, w_a, w_b, w_gate, w_gate)


def _mix_out_kernel(mix_ref, x_ref, wo_ref, gpost_ref, gpre_ref, x1_ref, xn_ref):
    mix = _dot(mix_ref[...], wo_ref[...])
    x1 = x_ref[...] + _rms(mix, gpost_ref[...])
    x1_ref[...] = x1
    xn_ref[...] = _rms(x1, gpre_ref[...]).astype(BF16)


def _mix_out(mixin, x, w_o, g_post, g_pre_ffn, *, tm):
    m, d = x.shape
    row = pl.BlockSpec((tm, d), lambda i: (i, 0))
    vec = pl.BlockSpec((1, d), lambda i: (0, 0))
    return pl.pallas_call(
        _mix_out_kernel,
        grid=(m // tm,),
        in_specs=[row, row, pl.BlockSpec((d, d), lambda i: (0, 0)), vec, vec],
        out_specs=[row, row],
        out_shape=[jax.ShapeDtypeStruct((m, d), F32), jax.ShapeDtypeStruct((m, d), BF16)],
        compiler_params=_params("parallel"),
        name="mix_out",
    )(mixin, x, w_o, g_post, g_pre_ffn)


def _ffn_in_kernel(x_ref, wa_ref, wg_ref, o_ref):
    x = x_ref[...]
    a = _dot(x, wa_ref[...])
    g = _dot(x, wg_ref[...])
    o_ref[...] = (jax.nn.silu(a) * g).astype(BF16)


def _ffn_in(xn, w_ffn_in, *, tm, tn):
    m, d = xn.shape
    hidden = w_ffn_in.shape[1] // 2
    n_col = hidden // tn
    return pl.pallas_call(
        _ffn_in_kernel,
        grid=(n_col, m // tm),
        in_specs=[pl.BlockSpec((tm, d), lambda j, i: (i, 0)),
                  pl.BlockSpec((d, tn), lambda j, i: (0, j)),
                  pl.BlockSpec((d, tn), lambda j, i: (0, j + n_col))],
        out_specs=pl.BlockSpec((tm, tn), lambda j, i: (i, j)),
        out_shape=jax.ShapeDtypeStruct((m, hidden), BF16),
        compiler_params=_params("parallel", "parallel"),
        name="ffn_in",
    )(xn, w_ffn_in, w_ffn_in)


def _ffn_out_kernel(hm_ref, w_ref, x1_ref, gpost_ref, gple_ref, x2_ref, xg_ref, acc_scr):
    k = pl.program_id(1)

    @pl.when(k == 0)
    def _():
        acc_scr[...] = jnp.zeros(acc_scr.shape, F32)

    acc_scr[...] += _dot(hm_ref[...], w_ref[...])

    @pl.when(k == pl.num_programs(1) - 1)
    def _():
        x2 = x1_ref[...] + _rms(acc_scr[...], gpost_ref[...])
        x2_ref[...] = x2
        xg_ref[...] = _rms(x2, gple_ref[...]).astype(BF16)


def _ffn_out(hmid, w_ffn_out, x1, g_post_ffn, g_ple, *, tm, tk):
    m, d = x1.shape
    hidden = hmid.shape[1]
    row = pl.BlockSpec((tm, d), lambda i, k: (i, 0))
    vec = pl.BlockSpec((1, d), lambda i, k: (0, 0))
    return pl.pallas_call(
        _ffn_out_kernel,
        grid=(m // tm, hidden // tk),
        in_specs=[pl.BlockSpec((tm, tk), lambda i, k: (i, k)),
                  pl.BlockSpec((tk, d), lambda i, k: (k, 0)),
                  row, vec, vec],
        out_specs=[row, row],
        out_shape=[jax.ShapeDtypeStruct((m, d), F32), jax.ShapeDtypeStruct((m, d), BF16)],
        scratch_shapes=[pltpu.VMEM((tm, d), F32)],
        compiler_params=_params("parallel", "arbitrary"),
        name="ffn_out",
    )(hmid, w_ffn_out, x1, g_post_ffn, g_ple)


def _ple_kernel(x2_ref, xg_ref, p_ref, wg_ref, wp_ref, o_ref):
    gate = jax.nn.sigmoid(_dot(xg_ref[...], wg_ref[...]))
    o_ref[...] = x2_ref[...] + _dot(p_ref[...].astype(BF16), wp_ref[...]) * gate


def _ple(x2, xg, p, w_ple_gate, w_ple, *, tm, tn):
    m, d = x2.shape
    return pl.pallas_call(
        _ple_kernel,
        grid=(d // tn, m // tm),
        in_specs=[pl.BlockSpec((tm, tn), lambda j, i: (i, j)),
                  pl.BlockSpec((tm, d), lambda j, i: (i, 0)),
                  pl.BlockSpec((tm, p.shape[1]), lambda j, i: (i, 0)),
                  pl.BlockSpec((d, tn), lambda j, i: (0, j)),
                  pl.BlockSpec((w_ple.shape[0], tn), lambda j, i: (0, j))],
        out_specs=pl.BlockSpec((tm, tn), lambda j, i: (i, j)),
        out_shape=jax.ShapeDtypeStruct((m, d), F32),
        compiler_params=_params("parallel", "parallel"),
        name="ple",
    )(x2, xg, p, w_ple_gate, w_ple)


def _layer_tail(x, h, attn, gm, p, w, *, tm):
    m, d = x.shape
    hidden = w["w_ffn_out"].shape[0]
    mixin = _gated_mix(attn, gm, h, w["w_a_out"], w["w_b_out"], w["w_gate"], tm=min(m, 1024), tn=512)
    x1, xn = _mix_out(mixin, x, w["w_o"], w["g_post_mix"], w["g_pre_ffn"], tm=min(m, 256))
    hmid = _ffn_in(xn, w["w_ffn_in"], tm=tm, tn=hidden // 4)
    x2, xg = _ffn_out(hmid, w["w_ffn_out"], x1, w["g_post_ffn"], w["g_ple"], tm=tm, tk=hidden // 4)
    return _ple(x2, xg, p, w["w_ple_gate"], w["w_ple"], tm=min(m, 1024), tn=512)


def _rope_tables(pos):
    freqs = jnp.power(jnp.float32(ROPE_THETA), -2.0 * jnp.arange(ROPE_HALF, dtype=F32) / ROPE_DIM)
    ang = pos.astype(F32)[:, None] * freqs[None, :]
    cos, sin = jnp.cos(ang), jnp.sin(ang)
    n = pos.shape[0]
    c = jnp.concatenate([cos, cos, jnp.ones((n, HEAD_DIM - ROPE_DIM), F32)], axis=1)
    sa = jnp.concatenate([-sin, jnp.zeros((n, HEAD_DIM - ROPE_HALF), F32)], axis=1)
    sb = jnp.concatenate([jnp.zeros((n, ROPE_HALF), F32), sin, jnp.zeros((n, HEAD_DIM - ROPE_DIM), F32)], axis=1)
    return c, sa, sb


def kernel(x_prompt, x_sample, cache_k, cache_v, page_table, p_prompt, p_sample, g_pre_mix, w_in, g_vnorm, w_spatial, b_spatial, w_a_out, w_b_out, w_gate, w_o, g_post_mix, g_pre_ffn, w_ffn_in, w_ffn_out, g_post_ffn, g_ple, w_ple_gate, w_ple):
    batch, seq, d = x_prompt.shape
    dec_b, dec_s, _ = x_sample.shape
    depth = w_in.shape[0]
    page = cache_k.shape[2]
    past_len = page_table.shape[1] * page
    n_past = past_len // MOBA_BLOCK
    assert depth == 1 and seq % MOBA_BLOCK == 0 and past_len % MOBA_BLOCK == 0 and MOBA_BLOCK == 2 * page
    assert dec_s <= GMLP_CHUNK and n_past >= MOBA_TOPK
    l = 0

    w = {
        "w_a_out": w_a_out[l].astype(BF16), "w_b_out": w_b_out[l].astype(BF16),
        "w_gate": w_gate[l].astype(BF16), "w_o": w_o[l].astype(BF16),
        "w_ffn_in": w_ffn_in[l].astype(BF16), "w_ffn_out": w_ffn_out[l].astype(BF16),
        "w_ple_gate": w_ple_gate[l].astype(BF16), "w_ple": w_ple[l].astype(BF16),
        "g_post_mix": g_post_mix[l][None], "g_pre_ffn": g_pre_ffn[l][None],
        "g_post_ffn": g_post_ffn[l][None], "g_ple": g_ple[l][None],
    }
    w_in_b = w_in[l].astype(BF16)
    g_pre = g_pre_mix[l][None]
    g_vn = g_vnorm[l][None]

    w_tril = jnp.tril(w_spatial[l])
    wsp_p = w_tril.astype(BF16)
    bsp_p = b_spatial[l].T
    eye_b = jnp.eye(dec_b, dtype=F32)
    wsp_s = jnp.einsum("ab,gts->gatbs", eye_b, w_tril[:, :dec_s, :dec_s]).reshape(
        GMLP_GROUPS, dec_b * dec_s, dec_b * dec_s).astype(BF16)
    bsp_s = jnp.tile(b_spatial[l][:, :dec_s].T, (dec_b, 1))

    mp = batch * seq
    xp = x_prompt.reshape(mp, d)
    cp, sap, sbp = _rope_tables(jnp.arange(seq, dtype=jnp.int32))
    hp, qp, kp, vp, gmp = _mixer(xp, g_pre, w_in_b, cp, sap, sbp, g_vn, wsp_p, bsp_p,
                                 tm=512, chunk=GMLP_CHUNK, with_vn=False)
    ap = _moba_prompt(qp, kp, vp, batch=batch, seq=seq)
    yp = _layer_tail(xp, hp, ap, gmp, p_prompt[l].reshape(mp, -1), w, tm=512)

    ms = dec_b * dec_s
    xs = x_sample.reshape(ms, d)
    pos_s = past_len + jnp.arange(dec_s, dtype=jnp.int32)
    cs, sas, sbs = (jnp.tile(t, (dec_b, 1)) for t in _rope_tables(pos_s))
    hs, qs, ks, vs, gms, vns = _mixer(xs, g_pre, w_in_b, cs, sas, sbs, g_vn, wsp_s, bsp_s,
                                      tm=ms, chunk=ms, with_vn=True)
    cache_k2 = cache_k.reshape(depth * cache_k.shape[1], page * N_HEADS, HEAD_DIM)
    cache_v2 = cache_v.reshape(depth * cache_v.shape[1], page * N_HEADS, HEAD_DIM)
    q3 = qs.reshape(dec_b, dec_s, ATTN_WIDTH)
    kn3 = ks.reshape(dec_b, dec_s, ATTN_WIDTH)
    vn3 = vs.reshape(dec_b, dec_s, ATTN_WIDTH)
    kmean = _past_kmean(cache_k2, page_table, n_past_blocks=n_past)
    sel = _select_blocks(q3, kmean, kn3, n_past=n_past)
    sel_flat = sel[:MOBA_TOPK].T.reshape(-1)
    pad = lambda t: jnp.pad(t, ((0, 0), (0, SAMPLE_ROWS - dec_s), (0, 0)))
    a_s = _moba_sample(pad(q3), pad(kn3), pad(vn3), cache_k2, cache_v2, page_table, sel_flat, dec_s=dec_s)
    a_s = a_s[:, :dec_s].reshape(ms, ATTN_WIDTH)
    ys = _layer_tail(xs, hs, a_s, gms, p_sample[l].reshape(ms, -1), w, tm=ms)

    return (yp.reshape(batch, seq, d), ys.reshape(dec_b, dec_s, d),
            kp.reshape(1, batch, seq, N_HEADS, HEAD_DIM), vp.reshape(1, batch, seq, N_HEADS, HEAD_DIM),
            ks.reshape(1, dec_b, dec_s, N_HEADS, HEAD_DIM), vs.reshape(1, dec_b, dec_s, N_HEADS, HEAD_DIM),
            vns.reshape(1, dec_b, dec_s, GMLP_WIDTH))
```

```python
import functools

import jax
import jax.numpy as jnp
from jax import lax
from jax.experimental import pallas as pl
from jax.experimental.pallas import tpu as pltpu

F32 = jnp.float32
BF16 = jnp.bfloat16

N_HEADS = 8
HEAD_DIM = 128
ATTN_WIDTH = N_HEADS * HEAD_DIM
MOBA_BLOCK = 256
MOBA_TOPK = 3
ROPE_THETA = 500000.0
ROPE_DIM = HEAD_DIM // 4
ROPE_HALF = ROPE_DIM // 2
GMLP_GROUPS = 8
GMLP_CHUNK = 128
GMLP_WIDTH = 1024
GMLP_GROUP_DIM = GMLP_WIDTH // GMLP_GROUPS
NORM_EPS = 1e-6
NEG_INF = -1e30
TAKEN = -3e38
SAMPLE_ROWS = 16
ROW_CHUNK = 256
NORM_ROW_CHUNK = 128

VMEM_LIMIT_BYTES = 56 * 1024 * 1024


def _params(*semantics):
    return pltpu.CompilerParams(dimension_semantics=semantics, vmem_limit_bytes=VMEM_LIMIT_BYTES)


def _rms(x, g):
    return x * lax.rsqrt(jnp.mean(x * x, axis=-1, keepdims=True) + NORM_EPS) * g


def _dot(a, b):
    return jnp.dot(a, b, preferred_element_type=F32)


def _dot_nt(a, b):
    return lax.dot_general(a, b, (((1,), (1,)), ((), ())), preferred_element_type=F32)


def _mixer_kernel(x_ref, g_ref, w_ref, cos_ref, sa_ref, sb_ref, gvn_ref, wsp_ref, bsp_ref,
                  h_out, q_out, k_out, v_out, gm_out, *rest, chunk, rc, with_vn):
    if with_vn:
        vn_out, u_scr = rest
    else:
        (u_scr,) = rest
    j = pl.program_id(1)
    tm = x_ref.shape[0]
    row_chunks = [slice(r, r + rc) for r in range(0, tm, rc)]

    def rope_to(out_ref, z, rows):
        c, sa, sb = cos_ref[rows, :], sa_ref[rows, :], sb_ref[rows, :]
        for hd in range(N_HEADS):
            sl = slice(hd * HEAD_DIM, (hd + 1) * HEAD_DIM)
            zs = z[:, sl]
            out_ref[rows, sl] = (zs * c + pltpu.roll(zs, HEAD_DIM - ROPE_HALF, 1) * sa
                                 + pltpu.roll(zs, ROPE_HALF, 1) * sb)

    @pl.when(j == 0)
    def _():
        for rows in row_chunks:
            hb = _rms(x_ref[rows, :], g_ref[...]).astype(BF16)
            h_out[rows, :] = hb
            rope_to(q_out, _dot(hb, w_ref[...]), rows)

    @pl.when(j == 1)
    def _():
        for rows in row_chunks:
            rope_to(k_out, _dot(h_out[rows, :], w_ref[...]), rows)

    @pl.when(j == 2)
    def _():
        for rows in row_chunks:
            v_out[rows, :] = _dot(h_out[rows, :], w_ref[...])

    @pl.when(j == 3)
    def _():
        for rows in row_chunks:
            u_scr[rows, :] = jax.nn.gelu(_dot(h_out[rows, :], w_ref[...]))

    @pl.when(j == 4)
    def _():
        for rows in row_chunks:
            vg = jax.nn.gelu(_dot(h_out[rows, :], w_ref[...]))
            xc = vg - jnp.mean(vg, axis=-1, keepdims=True)
            vn = xc * lax.rsqrt(jnp.mean(xc * xc, axis=-1, keepdims=True) + NORM_EPS) * gvn_ref[...]
            if with_vn:
                vn_out[rows, :] = vn
            vnb = vn.astype(BF16)
            for c in range(rc // chunk):
                sub = slice(c * chunk, (c + 1) * chunk)
                dst = slice(rows.start + c * chunk, rows.start + (c + 1) * chunk)
                for g in range(GMLP_GROUPS):
                    cols = slice(g * GMLP_GROUP_DIM, (g + 1) * GMLP_GROUP_DIM)
                    s = _dot(wsp_ref[g], vnb[sub, cols]) + bsp_ref[:, g:g + 1]
                    gm_out[dst, cols] = (u_scr[dst, cols] * s).astype(BF16)


def _mixer(x, g_pre, w_in, cos_t, sa_t, sb_t, g_vn, wsp, bsp, *, tm, chunk, with_vn):
    m, d = x.shape
    n_seg = w_in.shape[1] // ATTN_WIDTH
    t_blocks = cos_t.shape[0] // tm
    tab_spec = pl.BlockSpec((tm, HEAD_DIM), lambda i, j: (i % t_blocks, 0))
    row_spec = lambda w: pl.BlockSpec((tm, w), lambda i, j: (i, 0))
    out_shape = [jax.ShapeDtypeStruct((m, d), BF16),
                 jax.ShapeDtypeStruct((m, ATTN_WIDTH), F32),
                 jax.ShapeDtypeStruct((m, ATTN_WIDTH), F32),
                 jax.ShapeDtypeStruct((m, ATTN_WIDTH), F32),
                 jax.ShapeDtypeStruct((m, GMLP_WIDTH), BF16)]
    out_specs = [row_spec(d), row_spec(ATTN_WIDTH), row_spec(ATTN_WIDTH), row_spec(ATTN_WIDTH),
                 row_spec(GMLP_WIDTH)]
    if with_vn:
        out_shape.append(jax.ShapeDtypeStruct((m, GMLP_WIDTH), F32))
        out_specs.append(row_spec(GMLP_WIDTH))
    return pl.pallas_call(
        functools.partial(_mixer_kernel, chunk=chunk, rc=max(chunk, min(tm, ROW_CHUNK)), with_vn=with_vn),
        grid=(m // tm, n_seg),
        in_specs=[row_spec(d),
                  pl.BlockSpec((1, d), lambda i, j: (0, 0)),
                  pl.BlockSpec((d, ATTN_WIDTH), lambda i, j: (0, j)),
                  tab_spec, tab_spec, tab_spec,
                  pl.BlockSpec((1, GMLP_WIDTH), lambda i, j: (0, 0)),
                  pl.BlockSpec((GMLP_GROUPS, chunk, chunk), lambda i, j: (0, 0, 0)),
                  pl.BlockSpec((chunk, GMLP_GROUPS), lambda i, j: (0, 0))],
        out_specs=out_specs,
        out_shape=out_shape,
        scratch_shapes=[pltpu.VMEM((tm, GMLP_WIDTH), F32)],
        compiler_params=_params("parallel", "arbitrary"),
        name="mixer",
    )(x, g_pre, w_in, cos_t, sa_t, sb_t, g_vn, wsp, bsp)


def _moba_prompt_kernel(q_ref, k_ref, v_ref, o_ref, kb_scr, vb_scr, s_scr):
    seq = q_ref.shape[0]
    nblk = seq // MOBA_BLOCK
    blk = MOBA_BLOCK
    scale = HEAD_DIM ** -0.5
    kb_scr[...] = k_ref[...].astype(BF16)
    vb_scr[...] = v_ref[...].astype(BF16)
    kmean = [jnp.mean(k_ref[n * blk:(n + 1) * blk, :], axis=0, keepdims=True) for n in range(nblk)]
    lane = lax.broadcasted_iota(jnp.int32, (blk, HEAD_DIM), 1)
    row_i = lax.broadcasted_iota(jnp.int32, (blk, blk), 0)
    col_i = lax.broadcasted_iota(jnp.int32, (blk, blk), 1)
    causal = col_i <= row_i

    for j in range(nblk):
        rows = slice(j * blk, (j + 1) * blk)
        qj = q_ref[rows, :]
        qb = qj.astype(BF16)
        gates = [jnp.sum(qj * kmean[n], axis=-1, keepdims=True) for n in range(j)]
        gate_l = jnp.zeros((blk, HEAD_DIM), F32)
        for n in range(j):
            gate_l = jnp.where(lane == n, gates[n], gate_l)
        rank = jnp.zeros((blk, HEAD_DIM), jnp.int32)
        for n in range(j):
            beats = (gates[n] > gate_l) | ((gates[n] == gate_l) & (lane > n))
            rank = rank + beats.astype(jnp.int32)
        keep = jnp.where(rank < MOBA_TOPK, 1.0, 0.0)

        m_run = None
        for n in range(j + 1):
            s = _dot_nt(qb, kb_scr[n * blk:(n + 1) * blk, :]) * scale
            if n == j:
                s = jnp.where(causal, s, NEG_INF)
            else:
                s = jnp.where(keep[:, n:n + 1] > 0.5, s, NEG_INF)
            s_scr[:, n * blk:(n + 1) * blk] = s
            m_n = jnp.max(s, axis=-1, keepdims=True)
            m_run = m_n if m_run is None else jnp.maximum(m_run, m_n)
        l_run = jnp.zeros((blk, 1), F32)
        acc = jnp.zeros((blk, HEAD_DIM), F32)
        for n in range(j + 1):
            p = jnp.exp(s_scr[:, n * blk:(n + 1) * blk] - m_run)
            l_run = l_run + jnp.sum(p, axis=-1, keepdims=True)
            acc = acc + _dot(p.astype(BF16), vb_scr[n * blk:(n + 1) * blk, :])
        o_ref[rows, :] = (acc / l_run).astype(BF16)


def _moba_prompt(q, k, v, *, batch, seq):
    spec = pl.BlockSpec((seq, HEAD_DIM), lambda b, h: (b, h))
    return pl.pallas_call(
        _moba_prompt_kernel,
        grid=(batch, N_HEADS),
        in_specs=[spec, spec, spec],
        out_specs=spec,
        out_shape=jax.ShapeDtypeStruct(q.shape, BF16),
        scratch_shapes=[pltpu.VMEM((seq, HEAD_DIM), BF16), pltpu.VMEM((seq, HEAD_DIM), BF16),
                        pltpu.VMEM((MOBA_BLOCK, seq), F32)],
        compiler_params=_params("parallel", "parallel"),
        name="moba_prompt",
    )(q, k, v)


KMEAN_PAGES_PER_STEP = 16


def _kmean_kernel(pt_ref, *refs):
    page_refs, o_ref = refs[:-1], refs[-1]
    n_blk = len(page_refs) // 2
    g = pl.program_id(1)
    page_rows = page_refs[0].shape[0] // N_HEADS

    def page_sum(ref):
        return jnp.sum(ref[...].reshape(page_rows, N_HEADS, HEAD_DIM), axis=0)

    for t in range(n_blk):
        tot = (page_sum(page_refs[2 * t]) + page_sum(page_refs[2 * t + 1])) * (1.0 / MOBA_BLOCK)
        for h in range(N_HEADS):
            o_ref[h, pl.ds(g * n_blk + t, 1), :] = tot[h:h + 1, :]


def _past_kmean(cache_k2, page_table, *, n_past_blocks):
    dec_b, n_pages = page_table.shape
    pps = KMEAN_PAGES_PER_STEP
    assert n_pages % pps == 0

    def page_spec(c):
        return pl.BlockSpec((None,) + cache_k2.shape[1:], lambda b, g, pt: (pt[b, g * pps + c], 0, 0))

    grid_spec = pltpu.PrefetchScalarGridSpec(
        num_scalar_prefetch=1,
        grid=(dec_b, n_pages // pps),
        in_specs=[page_spec(c) for c in range(pps)],
        out_specs=pl.BlockSpec((None, N_HEADS, n_past_blocks, HEAD_DIM), lambda b, g, pt: (b, 0, 0, 0)),
    )
    return pl.pallas_call(
        _kmean_kernel,
        grid_spec=grid_spec,
        out_shape=jax.ShapeDtypeStruct((dec_b, N_HEADS, n_past_blocks, HEAD_DIM), F32),
        compiler_params=_params("parallel", "arbitrary"),
        name="past_kmean",
    )(page_table, *([cache_k2] * pps))


def _select_kernel(q_ref, km_ref, kn_ref, sel_ref, gate_scr, *, dec_s, n_past):
    dec_b = q_ref.shape[0]
    gate_scr[...] = jnp.full(gate_scr.shape, NEG_INF, F32)
    for b in range(dec_b):
        own_mean = jnp.sum(kn_ref[b], axis=0, keepdims=True) * (1.0 / MOBA_BLOCK)
        for h in range(N_HEADS):
            cols = slice(h * HEAD_DIM, (h + 1) * HEAD_DIM)
            km = km_ref[b, h]
            for i in range(dec_s):
                col = (b * N_HEADS + h) * dec_s + i
                qi = q_ref[b, i:i + 1, cols]
                gate_scr[0:n_past, col:col + 1] = jnp.sum(km * qi, axis=-1, keepdims=True)
                gate_scr[n_past:n_past + 1, col:col + 1] = jnp.sum(qi * own_mean[:, cols], axis=-1,
                                                                   keepdims=True)
    gate = gate_scr[...]
    blk = lax.broadcasted_iota(jnp.int32, gate.shape, 0)
    gate = jnp.where(blk < n_past, gate, NEG_INF)
    out_row = lax.broadcasted_iota(jnp.int32, sel_ref.shape, 0)
    sel = jnp.zeros(sel_ref.shape, jnp.int32)
    for t in range(MOBA_TOPK):
        best = jnp.max(gate, axis=0, keepdims=True)
        idx = jnp.min(jnp.where(gate == best, blk, gate.shape[0]), axis=0, keepdims=True)
        sel = jnp.where(out_row == t, idx, sel)
        gate = jnp.where(blk == idx, TAKEN, gate)
    sel_ref[...] = sel


def _select_blocks(q3, kmean, kn3, *, n_past):
    dec_b, dec_s, width = q3.shape
    cols = dec_b * N_HEADS * dec_s
    gate_rows = 8 * (-(-(n_past + 1) // 8))
    return pl.pallas_call(
        functools.partial(_select_kernel, dec_s=dec_s, n_past=n_past),
        out_shape=jax.ShapeDtypeStruct((8, cols), jnp.int32),
        scratch_shapes=[pltpu.VMEM((gate_rows, cols), F32)],
        compiler_params=pltpu.CompilerParams(vmem_limit_bytes=VMEM_LIMIT_BYTES),
        name="select_blocks",
    )(q3, kmean, kn3)


def _moba_sample_kernel(pt_ref, sel_ref, q_ref, kn_ref, vn_ref, ck_hbm, cv_hbm, o_ref, kbuf, vbuf, sem,
                        *, dec_s, n_slots):
    n_pages = 2 * n_slots
    page = kbuf.shape[2]
    n_heads = pl.num_programs(1)
    step = pl.program_id(0) * n_heads + pl.program_id(1)
    n_steps = pl.num_programs(0) * n_heads

    def page_copies(step_idx, slot):
        bb, hh = step_idx // n_heads, step_idx % n_heads
        copies = []
        for c in range(n_pages):
            blk = sel_ref[step_idx * n_slots + c // 2]
            pg = pt_ref[bb, 2 * blk + c % 2]
            src = (pg, slice(None), hh, slice(None))
            copies.append(pltpu.make_async_copy(ck_hbm.at[src], kbuf.at[slot, c], sem.at[0, slot]))
            copies.append(pltpu.make_async_copy(cv_hbm.at[src], vbuf.at[slot, c], sem.at[1, slot]))
        return copies

    @pl.when(step == 0)
    def _():
        for cp in page_copies(step, 0):
            cp.start()

    @pl.when(step + 1 < n_steps)
    def _():
        for cp in page_copies(step + 1, (step + 1) % 2):
            cp.start()

    slot = step % 2
    for cp in page_copies(step, slot):
        cp.wait()
    k_refs = [kbuf.at[slot, c] for c in range(n_pages)]
    v_refs = [vbuf.at[slot, c] for c in range(n_pages)]
    scale = HEAD_DIM ** -0.5
    rows = q_ref.shape[0]
    qb = q_ref[...].astype(BF16)
    row = lax.broadcasted_iota(jnp.int32, (rows, page), 0)
    s_list = []
    m_run = None
    for c in range(n_pages):
        s = _dot_nt(qb, k_refs[c][...].astype(BF16)) * scale
        s = jnp.where(row == c // (2 * MOBA_TOPK), s, NEG_INF)
        s_list.append(s)
        m_c = jnp.max(s, axis=-1, keepdims=True)
        m_run = m_c if m_run is None else jnp.maximum(m_run, m_c)
    s_own = _dot_nt(qb, kn_ref[...].astype(BF16)) * scale
    r_o = lax.broadcasted_iota(jnp.int32, (rows, rows), 0)
    c_o = lax.broadcasted_iota(jnp.int32, (rows, rows), 1)
    s_own = jnp.where((c_o <= r_o) & (c_o < dec_s), s_own, NEG_INF)
    m_run = jnp.maximum(m_run, jnp.max(s_own, axis=-1, keepdims=True))
    p_own = jnp.exp(s_own - m_run)
    l_run = jnp.sum(p_own, axis=-1, keepdims=True)
    acc = _dot(p_own.astype(BF16), vn_ref[...].astype(BF16))
    for c in range(n_pages):
        p = jnp.exp(s_list[c] - m_run)
        l_run = l_run + jnp.sum(p, axis=-1, keepdims=True)
        acc = acc + _dot(p.astype(BF16), v_refs[c][...].astype(BF16))
    o_ref[...] = (acc / l_run).astype(BF16)


def _moba_sample(q3, kn3, vn3, cache_k2, cache_v2, page_table, sel_flat, *, dec_s):
    dec_b, rows, width = q3.shape
    page = cache_k2.shape[1] // N_HEADS
    n_slots = dec_s * MOBA_TOPK
    new_spec = pl.BlockSpec((None, rows, HEAD_DIM), lambda b, h, pt, sel: (b, 0, h))
    hbm_spec = pl.BlockSpec(memory_space=pl.ANY)
    grid_spec = pltpu.PrefetchScalarGridSpec(
        num_scalar_prefetch=2,
        grid=(dec_b, N_HEADS),
        in_specs=[new_spec, new_spec, new_spec, hbm_spec, hbm_spec],
        out_specs=new_spec,
        scratch_shapes=[pltpu.VMEM((2, 2 * n_slots, page, HEAD_DIM), F32),
                        pltpu.VMEM((2, 2 * n_slots, page, HEAD_DIM), F32),
                        pltpu.SemaphoreType.DMA((2, 2))],
    )
    return pl.pallas_call(
        functools.partial(_moba_sample_kernel, dec_s=dec_s, n_slots=n_slots),
        grid_spec=grid_spec,
        out_shape=jax.ShapeDtypeStruct(q3.shape, BF16),
        compiler_params=_params("arbitrary", "arbitrary"),
        name="moba_sample",
    )(page_table, sel_flat, q3, kn3, vn3,
      cache_k2.reshape(-1, page, N_HEADS, HEAD_DIM), cache_v2.reshape(-1, page, N_HEADS, HEAD_DIM))


def _gated_mix_kernel(a_ref, b_ref, h_ref, wa_ref, wb_ref, wga_ref, wgb_ref, o_ref):
    h = h_ref[...]
    ga = jax.nn.sigmoid(_dot(h, wga_ref[...]))
    gb = jax.nn.sigmoid(_dot(h, wgb_ref[...]))
    a = _dot(a_ref[...], wa_ref[...])
    b = _dot(b_ref[...], wb_ref[...])
    o_ref[...] = (ga * a + gb * b).astype(BF16)


def _gated_mix(attn, gm, h, w_a, w_b, w_gate, *, tm, tn):
    m, d = h.shape
    n_col = d // tn
    return pl.pallas_call(
        _gated_mix_kernel,
        grid=(n_col, m // tm),
        in_specs=[pl.BlockSpec((tm, attn.shape[1]), lambda j, i: (i, 0)),
                  pl.BlockSpec((tm, gm.shape[1]), lambda j, i: (i, 0)),
                  pl.BlockSpec((tm, d), lambda j, i: (i, 0)),
                  pl.BlockSpec((w_a.shape[0], tn), lambda j, i: (0, j)),
                  pl.BlockSpec((w_b.shape[0], tn), lambda j, i: (0, j)),
                  pl.BlockSpec((d, tn), lambda j, i: (0, j)),
                  pl.BlockSpec((d, tn), lambda j, i: (0, j + n_col))],
        out_specs=pl.BlockSpec((tm, tn), lambda j, i: (i, j)),
        out_shape=jax.ShapeDtypeStruct((m, d), BF16),
        compiler_params=_params("parallel", "parallel"),
        name="gated_mix",
    )(attn, gm, h, w_a, w_b, w_gate, w_gate)


def _row_chunks(tm, rc):
    rc = min(tm, rc)
    return [slice(r, r + rc) for r in range(0, tm, rc)]


def _mix_out_kernel(mix_ref, x_ref, wo_ref, gpost_ref, gpre_ref, x1_ref, xn_ref):
    for rows in _row_chunks(x_ref.shape[0], NORM_ROW_CHUNK):
        mix = _dot(mix_ref[rows, :], wo_ref[...])
        x1 = x_ref[rows, :] + _rms(mix, gpost_ref[...])
        x1_ref[rows, :] = x1
        xn_ref[rows, :] = _rms(x1, gpre_ref[...]).astype(BF16)


def _mix_out(mixin, x, w_o, g_post, g_pre_ffn, *, tm):
    m, d = x.shape
    row = pl.BlockSpec((tm, d), lambda i: (i, 0))
    vec = pl.BlockSpec((1, d), lambda i: (0, 0))
    return pl.pallas_call(
        _mix_out_kernel,
        grid=(m // tm,),
        in_specs=[row, row, pl.BlockSpec((d, d), lambda i: (0, 0), pipeline_mode=pl.Buffered(1)), vec, vec],
        out_specs=[row, row],
        out_shape=[jax.ShapeDtypeStruct((m, d), F32), jax.ShapeDtypeStruct((m, d), BF16)],
        compiler_params=_params("parallel"),
        name="mix_out",
    )(mixin, x, w_o, g_post, g_pre_ffn)


def _ffn_in_kernel(x_ref, wa_ref, wg_ref, o_ref):
    x = x_ref[...]
    a = _dot(x, wa_ref[...])
    g = _dot(x, wg_ref[...])
    o_ref[...] = (jax.nn.silu(a) * g).astype(BF16)


def _ffn_in(xn, w_ffn_in, *, tm, tn):
    m, d = xn.shape
    hidden = w_ffn_in.shape[1] // 2
    n_col = hidden // tn
    return pl.pallas_call(
        _ffn_in_kernel,
        grid=(n_col, m // tm),
        in_specs=[pl.BlockSpec((tm, d), lambda j, i: (i, 0)),
                  pl.BlockSpec((d, tn), lambda j, i: (0, j)),
                  pl.BlockSpec((d, tn), lambda j, i: (0, j + n_col))],
        out_specs=pl.BlockSpec((tm, tn), lambda j, i: (i, j)),
        out_shape=jax.ShapeDtypeStruct((m, hidden), BF16),
        compiler_params=_params("parallel", "parallel"),
        name="ffn_in",
    )(xn, w_ffn_in, w_ffn_in)


def _ffn_out_ple_kernel(hm_ref, w_ref, x1_ref, gpost_ref, gple_ref, p_ref, wpg_ref, wp_ref, y_ref, acc_scr):
    k = pl.program_id(1)
    last = pl.num_programs(1) - 1

    @pl.when(k == 0)
    def _():
        acc_scr[...] = _dot(hm_ref[...], w_ref[...])

    @pl.when((k > 0) & (k < last))
    def _():
        acc_scr[...] += _dot(hm_ref[...], w_ref[...])

    @pl.when(k == last)
    def _():
        for rows in _row_chunks(x1_ref.shape[0], NORM_ROW_CHUNK):
            f = acc_scr[rows, :] + _dot(hm_ref[rows, :], w_ref[...])
            x2 = x1_ref[rows, :] + _rms(f, gpost_ref[...])
            xg = _rms(x2, gple_ref[...]).astype(BF16)
            gate = jax.nn.sigmoid(_dot(xg, wpg_ref[...]))
            y_ref[rows, :] = x2 + _dot(p_ref[rows, :].astype(BF16), wp_ref[...]) * gate


def _ffn_out_ple(hmid, w_ffn_out, x1, g_post_ffn, g_ple, p, w_ple_gate, w_ple, *, tm, tk):
    m, d = x1.shape
    hidden = hmid.shape[1]
    assert hidden // tk >= 2
    row = pl.BlockSpec((tm, d), lambda i, k: (i, 0))
    vec = pl.BlockSpec((1, d), lambda i, k: (0, 0))
    const = lambda shape: pl.BlockSpec(shape, lambda i, k: (0, 0), pipeline_mode=pl.Buffered(1))
    return pl.pallas_call(
        _ffn_out_ple_kernel,
        grid=(m // tm, hidden // tk),
        in_specs=[pl.BlockSpec((tm, tk), lambda i, k: (i, k)),
                  pl.BlockSpec((tk, d), lambda i, k: (k, 0)),
                  row, vec, vec,
                  pl.BlockSpec((tm, p.shape[1]), lambda i, k: (i, 0)),
                  const(w_ple_gate.shape), const(w_ple.shape)],
        out_specs=row,
        out_shape=jax.ShapeDtypeStruct((m, d), F32),
        scratch_shapes=[pltpu.VMEM((tm, d), F32)],
        compiler_params=_params("parallel", "arbitrary"),
        name="ffn_out_ple",
    )(hmid, w_ffn_out, x1, g_post_ffn, g_ple, p, w_ple_gate, w_ple)


def _layer_tail(x, h, attn, gm, p, w, *, tm):
    m, d = x.shape
    hidden = w["w_ffn_out"].shape[0]
    mixin = _gated_mix(attn, gm, h, w["w_a_out"], w["w_b_out"], w["w_gate"], tm=min(m, 1024), tn=512)
    x1, xn = _mix_out(mixin, x, w["w_o"], w["g_post_mix"], w["g_pre_ffn"], tm=tm)
    hmid = _ffn_in(xn, w["w_ffn_in"], tm=tm, tn=hidden // 4)
    return _ffn_out_ple(hmid, w["w_ffn_out"], x1, w["g_post_ffn"], w["g_ple"], p, w["w_ple_gate"], w["w_ple"],
                        tm=tm, tk=hidden // 4)


def _rope_tables(pos):
    freqs = jnp.power(jnp.float32(ROPE_THETA), -2.0 * jnp.arange(ROPE_HALF, dtype=F32) / ROPE_DIM)
    ang = pos.astype(F32)[:, None] * freqs[None, :]
    cos, sin = jnp.cos(ang), jnp.sin(ang)
    n = pos.shape[0]
    c = jnp.concatenate([cos, cos, jnp.ones((n, HEAD_DIM - ROPE_DIM), F32)], axis=1)
    sa = jnp.concatenate([-sin, jnp.zeros((n, HEAD_DIM - ROPE_HALF), F32)], axis=1)
    sb = jnp.concatenate([jnp.zeros((n, ROPE_HALF), F32), sin, jnp.zeros((n, HEAD_DIM - ROPE_DIM), F32)], axis=1)
    return c, sa, sb


def kernel(x_prompt, x_sample, cache_k, cache_v, page_table, p_prompt, p_sample, g_pre_mix, w_in, g_vnorm, w_spatial, b_spatial, w_a_out, w_b_out, w_gate, w_o, g_post_mix, g_pre_ffn, w_ffn_in, w_ffn_out, g_post_ffn, g_ple, w_ple_gate, w_ple):
    batch, seq, d = x_prompt.shape
    dec_b, dec_s, _ = x_sample.shape
    depth = w_in.shape[0]
    page = cache_k.shape[2]
    past_len = page_table.shape[1] * page
    n_past = past_len // MOBA_BLOCK
    assert depth == 1 and seq % MOBA_BLOCK == 0 and past_len % MOBA_BLOCK == 0 and MOBA_BLOCK == 2 * page
    assert dec_s <= GMLP_CHUNK and n_past >= MOBA_TOPK
    l = 0

    w = {
        "w_a_out": w_a_out[l].astype(BF16), "w_b_out": w_b_out[l].astype(BF16),
        "w_gate": w_gate[l].astype(BF16), "w_o": w_o[l].astype(BF16),
        "w_ffn_in": w_ffn_in[l].astype(BF16), "w_ffn_out": w_ffn_out[l].astype(BF16),
        "w_ple_gate": w_ple_gate[l].astype(BF16), "w_ple": w_ple[l].astype(BF16),
        "g_post_mix": g_post_mix[l][None], "g_pre_ffn": g_pre_ffn[l][None],
        "g_post_ffn": g_post_ffn[l][None], "g_ple": g_ple[l][None],
    }
    w_in_b = w_in[l].astype(BF16)
    g_pre = g_pre_mix[l][None]
    g_vn = g_vnorm[l][None]

    w_tril = jnp.tril(w_spatial[l])
    wsp_p = w_tril.astype(BF16)
    bsp_p = b_spatial[l].T
    eye_b = jnp.eye(dec_b, dtype=F32)
    wsp_s = jnp.einsum("ab,gts->gatbs", eye_b, w_tril[:, :dec_s, :dec_s]).reshape(
        GMLP_GROUPS, dec_b * dec_s, dec_b * dec_s).astype(BF16)
    bsp_s = jnp.tile(b_spatial[l][:, :dec_s].T, (dec_b, 1))

    mp = batch * seq
    xp = x_prompt.reshape(mp, d)
    cp, sap, sbp = _rope_tables(jnp.arange(seq, dtype=jnp.int32))
    hp, qp, kp, vp, gmp = _mixer(xp, g_pre, w_in_b, cp, sap, sbp, g_vn, wsp_p, bsp_p,
                                 tm=512, chunk=GMLP_CHUNK, with_vn=False)
    ap = _moba_prompt(qp, kp, vp, batch=batch, seq=seq)
    yp = _layer_tail(xp, hp, ap, gmp, p_prompt[l].reshape(mp, -1), w, tm=512)

    ms = dec_b * dec_s
    xs = x_sample.reshape(ms, d)
    pos_s = past_len + jnp.arange(dec_s, dtype=jnp.int32)
    cs, sas, sbs = (jnp.tile(t, (dec_b, 1)) for t in _rope_tables(pos_s))
    hs, qs, ks, vs, gms, vns = _mixer(xs, g_pre, w_in_b, cs, sas, sbs, g_vn, wsp_s, bsp_s,
                                      tm=ms, chunk=ms, with_vn=True)
    cache_k2 = cache_k.reshape(depth * cache_k.shape[1], page * N_HEADS, HEAD_DIM)
    cache_v2 = cache_v.reshape(depth * cache_v.shape[1], page * N_HEADS, HEAD_DIM)
    q3 = qs.reshape(dec_b, dec_s, ATTN_WIDTH)
    kn3 = ks.reshape(dec_b, dec_s, ATTN_WIDTH)
    vn3 = vs.reshape(dec_b, dec_s, ATTN_WIDTH)
    kmean = _past_kmean(cache_k2, page_table, n_past_blocks=n_past)
    sel = _select_blocks(q3, kmean, kn3, n_past=n_past)
    sel_flat = sel[:MOBA_TOPK].T.reshape(-1)
    pad = lambda t: jnp.pad(t, ((0, 0), (0, SAMPLE_ROWS - dec_s), (0, 0)))
    a_s = _moba_sample(pad(q3), pad(kn3), pad(vn3), cache_k2, cache_v2, page_table, sel_flat, dec_s=dec_s)
    a_s = a_s[:, :dec_s].reshape(ms, ATTN_WIDTH)
    ys = _layer_tail(xs, hs, a_s, gms, p_sample[l].reshape(ms, -1), w, tm=ms)

    return (yp.reshape(batch, seq, d), ys.reshape(dec_b, dec_s, d),
            kp.reshape(1, batch, seq, N_HEADS, HEAD_DIM), vp.reshape(1, batch, seq, N_HEADS, HEAD_DIM),
            ks.reshape(1, dec_b, dec_s, N_HEADS, HEAD_DIM), vs.reshape(1, dec_b, dec_s, N_HEADS, HEAD_DIM),
            vns.reshape(1, dec_b, dec_s, GMLP_WIDTH))
```

```python
import functools

import jax
import jax.numpy as jnp
from jax import lax
from jax.experimental import pallas as pl
from jax.experimental.pallas import tpu as pltpu

F32 = jnp.float32
BF16 = jnp.bfloat16

N_HEADS = 8
HEAD_DIM = 128
ATTN_WIDTH = N_HEADS * HEAD_DIM
MOBA_BLOCK = 256
MOBA_TOPK = 3
ROPE_THETA = 500000.0
ROPE_DIM = HEAD_DIM // 4
ROPE_HALF = ROPE_DIM // 2
GMLP_GROUPS = 8
GMLP_CHUNK = 128
GMLP_WIDTH = 1024
GMLP_GROUP_DIM = GMLP_WIDTH // GMLP_GROUPS
NORM_EPS = 1e-6
NEG_INF = -1e30
TAKEN = -3e38
SAMPLE_ROWS = 16
ROW_CHUNK = 256
NORM_ROW_CHUNK = 128

VMEM_LIMIT_BYTES = 56 * 1024 * 1024


def _params(*semantics):
    return pltpu.CompilerParams(dimension_semantics=semantics, vmem_limit_bytes=VMEM_LIMIT_BYTES)


def _rms(x, g):
    return x * lax.rsqrt(jnp.mean(x * x, axis=-1, keepdims=True) + NORM_EPS) * g


def _dot(a, b):
    return jnp.dot(a, b, preferred_element_type=F32)


def _dot_nt(a, b):
    return lax.dot_general(a, b, (((1,), (1,)), ((), ())), preferred_element_type=F32)


def _mixer_kernel(x_ref, g_ref, w_ref, cos_ref, sa_ref, sb_ref, gvn_ref, wsp_ref, bsp_ref,
                  h_out, q_out, k_out, v_out, gm_out, *rest, chunk, rc, with_vn):
    if with_vn:
        vn_out, u_scr, h_scr = rest
    else:
        u_scr, h_scr = rest
    j = pl.program_id(1)
    tm = x_ref.shape[0]
    row_chunks = [slice(r, r + rc) for r in range(0, tm, rc)]

    def rope_to(out_ref, z, rows):
        c, sa, sb = cos_ref[rows, :], sa_ref[rows, :], sb_ref[rows, :]
        for hd in range(N_HEADS):
            sl = slice(hd * HEAD_DIM, (hd + 1) * HEAD_DIM)
            zs = z[:, sl]
            out_ref[rows, sl] = (zs * c + pltpu.roll(zs, HEAD_DIM - ROPE_HALF, 1) * sa
                                 + pltpu.roll(zs, ROPE_HALF, 1) * sb)

    @pl.when(j == 0)
    def _():
        for rows in row_chunks:
            hb = _rms(x_ref[rows, :], g_ref[...]).astype(BF16)
            h_out[rows, :] = hb
            h_scr[rows, :] = hb
            rope_to(q_out, _dot(hb, w_ref[...]), rows)

    @pl.when(j == 1)
    def _():
        for rows in row_chunks:
            rope_to(k_out, _dot(h_scr[rows, :], w_ref[...]), rows)

    @pl.when(j == 2)
    def _():
        for rows in row_chunks:
            v_out[rows, :] = _dot(h_scr[rows, :], w_ref[...])

    @pl.when(j == 3)
    def _():
        for rows in row_chunks:
            u_scr[rows, :] = jax.nn.gelu(_dot(h_scr[rows, :], w_ref[...]))

    @pl.when(j == 4)
    def _():
        for rows in row_chunks:
            vg = jax.nn.gelu(_dot(h_scr[rows, :], w_ref[...]))
            xc = vg - jnp.mean(vg, axis=-1, keepdims=True)
            vn = xc * lax.rsqrt(jnp.mean(xc * xc, axis=-1, keepdims=True) + NORM_EPS) * gvn_ref[...]
            if with_vn:
                vn_out[rows, :] = vn
            vnb = vn.astype(BF16)
            for c in range(rc // chunk):
                sub = slice(c * chunk, (c + 1) * chunk)
                dst = slice(rows.start + c * chunk, rows.start + (c + 1) * chunk)
                for g in range(GMLP_GROUPS):
                    cols = slice(g * GMLP_GROUP_DIM, (g + 1) * GMLP_GROUP_DIM)
                    s = _dot(wsp_ref[g], vnb[sub, cols]) + bsp_ref[:, g:g + 1]
                    gm_out[dst, cols] = (u_scr[dst, cols] * s).astype(BF16)


def _mixer(x, g_pre, w_in, cos_t, sa_t, sb_t, g_vn, wsp, bsp, *, tm, chunk, with_vn):
    m, d = x.shape
    n_seg = w_in.shape[1] // ATTN_WIDTH
    t_blocks = cos_t.shape[0] // tm
    tab_spec = pl.BlockSpec((tm, HEAD_DIM), lambda i, j: (i % t_blocks, 0))
    row_spec = lambda w: pl.BlockSpec((tm, w), lambda i, j: (i, 0))
    n_row = m // tm

    def early_spec(w, seg):
        return pl.BlockSpec((tm, w), lambda i, j: (jnp.minimum(i + (j > seg).astype(jnp.int32), n_row - 1), 0))

    out_shape = [jax.ShapeDtypeStruct((m, d), BF16),
                 jax.ShapeDtypeStruct((m, ATTN_WIDTH), F32),
                 jax.ShapeDtypeStruct((m, ATTN_WIDTH), F32),
                 jax.ShapeDtypeStruct((m, ATTN_WIDTH), F32),
                 jax.ShapeDtypeStruct((m, GMLP_WIDTH), BF16)]
    out_specs = [early_spec(d, 0), early_spec(ATTN_WIDTH, 0), early_spec(ATTN_WIDTH, 1), early_spec(ATTN_WIDTH, 2),
                 row_spec(GMLP_WIDTH)]
    if with_vn:
        out_shape.append(jax.ShapeDtypeStruct((m, GMLP_WIDTH), F32))
        out_specs.append(row_spec(GMLP_WIDTH))
    return pl.pallas_call(
        functools.partial(_mixer_kernel, chunk=chunk, rc=max(chunk, min(tm, ROW_CHUNK)), with_vn=with_vn),
        grid=(m // tm, n_seg),
        in_specs=[row_spec(d),
                  pl.BlockSpec((1, d), lambda i, j: (0, 0)),
                  pl.BlockSpec((d, ATTN_WIDTH), lambda i, j: (0, j)),
                  tab_spec, tab_spec, tab_spec,
                  pl.BlockSpec((1, GMLP_WIDTH), lambda i, j: (0, 0)),
                  pl.BlockSpec((GMLP_GROUPS, chunk, chunk), lambda i, j: (0, 0, 0)),
                  pl.BlockSpec((chunk, GMLP_GROUPS), lambda i, j: (0, 0))],
        out_specs=out_specs,
        out_shape=out_shape,
        scratch_shapes=[pltpu.VMEM((tm, GMLP_WIDTH), F32), pltpu.VMEM((tm, d), BF16)],
        compiler_params=_params("arbitrary", "arbitrary"),
        name="mixer",
    )(x, g_pre, w_in, cos_t, sa_t, sb_t, g_vn, wsp, bsp)


def _moba_prompt_kernel(q_ref, k_ref, v_ref, o_ref, kb_scr, vt_scr, s_scr):
    seq = q_ref.shape[0]
    nblk = seq // MOBA_BLOCK
    blk = MOBA_BLOCK
    scale = HEAD_DIM ** -0.5
    kb_scr[...] = k_ref[...].astype(BF16)
    vt_scr[...] = v_ref[...].T.astype(BF16)
    qt = q_ref[...].T
    kt = k_ref[...].T

    blk_id = lax.broadcasted_iota(jnp.int32, (nblk, seq), 0)
    q_blk = lax.broadcasted_iota(jnp.int32, (nblk, seq), 1) // blk
    gate = jnp.zeros((nblk, seq), F32)
    for n in range(nblk):
        kmean_n = jnp.mean(kt[:, n * blk:(n + 1) * blk], axis=1, keepdims=True)
        g_n = jnp.sum(qt * kmean_n, axis=0, keepdims=True)
        gate = jnp.where(blk_id == n, g_n, gate)
    rank = jnp.zeros((nblk, seq), jnp.int32)
    for m in range(nblk):
        g_m = gate[m:m + 1, :]
        beats = (m < q_blk) & ((g_m > gate) | ((g_m == gate) & (m < blk_id)))
        rank = rank + beats.astype(jnp.int32)
    keep = jnp.where((blk_id < q_blk) & (rank < MOBA_TOPK), 1.0, 0.0)

    key_i = lax.broadcasted_iota(jnp.int32, (blk, blk), 0)
    qry_i = lax.broadcasted_iota(jnp.int32, (blk, blk), 1)
    causal = key_i <= qry_i

    for j in range(nblk):
        cols = slice(j * blk, (j + 1) * blk)
        qb = q_ref[cols, :].astype(BF16)
        keep_j = keep[:, cols]
        m_run = None
        for n in range(j + 1):
            st = _dot_nt(kb_scr[n * blk:(n + 1) * blk, :], qb) * scale
            if n == j:
                st = jnp.where(causal, st, NEG_INF)
            else:
                st = jnp.where(keep_j[n:n + 1, :] > 0.5, st, NEG_INF)
            s_scr[n] = st
            m_n = jnp.max(st, axis=0, keepdims=True)
            m_run = m_n if m_run is None else jnp.maximum(m_run, m_n)
        l_run = jnp.zeros((1, blk), F32)
        acc = jnp.zeros((HEAD_DIM, blk), F32)
        for n in range(j + 1):
            p = jnp.exp(s_scr[n] - m_run)
            l_run = l_run + jnp.sum(p, axis=0, keepdims=True)
            acc = acc + _dot(vt_scr[:, n * blk:(n + 1) * blk], p.astype(BF16))
        o_ref[cols, :] = (acc / l_run).T.astype(BF16)


def _moba_prompt(q, k, v, *, batch, seq):
    spec = pl.BlockSpec((seq, HEAD_DIM), lambda b, h: (b, h))
    return pl.pallas_call(
        _moba_prompt_kernel,
        grid=(batch, N_HEADS),
        in_specs=[spec, spec, spec],
        out_specs=spec,
        out_shape=jax.ShapeDtypeStruct(q.shape, BF16),
        scratch_shapes=[pltpu.VMEM((seq, HEAD_DIM), BF16), pltpu.VMEM((HEAD_DIM, seq), BF16),
                        pltpu.VMEM((seq // MOBA_BLOCK, MOBA_BLOCK, MOBA_BLOCK), F32)],
        compiler_params=_params("parallel", "parallel"),
        name="moba_prompt",
    )(q, k, v)


def _scan_key_means(page_refs, km_ref, group):
    n_blk = len(page_refs) // 2
    page_rows = page_refs[0].shape[0] // N_HEADS

    def page_sum(ref):
        return jnp.sum(ref[...].reshape(page_rows, N_HEADS, HEAD_DIM), axis=0)

    for t in range(n_blk):
        tot = (page_sum(page_refs[2 * t]) + page_sum(page_refs[2 * t + 1])) * (1.0 / MOBA_BLOCK)
        for h in range(N_HEADS):
            km_ref[h, pl.ds(group * n_blk + t, 1), :] = tot[h:h + 1, :]


def _select_kernel(q_ref, km_ref, kn_ref, sel_ref, gate_scr, *, dec_s, n_past):
    dec_b = q_ref.shape[0]
    gate_scr[...] = jnp.full(gate_scr.shape, NEG_INF, F32)
    for b in range(dec_b):
        own_mean = jnp.sum(kn_ref[b], axis=0, keepdims=True) * (1.0 / MOBA_BLOCK)
        for h in range(N_HEADS):
            cols = slice(h * HEAD_DIM, (h + 1) * HEAD_DIM)
            km = km_ref[b, h]
            for i in range(dec_s):
                col = (b * N_HEADS + h) * dec_s + i
                qi = q_ref[b, i:i + 1, cols]
                gate_scr[0:n_past, col:col + 1] = jnp.sum(km * qi, axis=-1, keepdims=True)
                gate_scr[n_past:n_past + 1, col:col + 1] = jnp.sum(qi * own_mean[:, cols], axis=-1,
                                                                   keepdims=True)
    gate = gate_scr[...]
    blk = lax.broadcasted_iota(jnp.int32, gate.shape, 0)
    gate = jnp.where(blk < n_past, gate, NEG_INF)
    out_row = lax.broadcasted_iota(jnp.int32, sel_ref.shape, 0)
    sel = jnp.zeros(sel_ref.shape, jnp.int32)
    for t in range(MOBA_TOPK):
        best = jnp.max(gate, axis=0, keepdims=True)
        idx = jnp.min(jnp.where(gate == best, blk, gate.shape[0]), axis=0, keepdims=True)
        sel = jnp.where(out_row == t, idx, sel)
        gate = jnp.where(blk == idx, TAKEN, gate)
    sel_ref[...] = sel


def _select_blocks(q3, kmean, kn3, *, n_past):
    dec_b, dec_s, width = q3.shape
    cols = dec_b * N_HEADS * dec_s
    gate_rows = 8 * (-(-(n_past + 1) // 8))
    return pl.pallas_call(
        functools.partial(_select_kernel, dec_s=dec_s, n_past=n_past),
        out_shape=jax.ShapeDtypeStruct((8, cols), jnp.int32),
        scratch_shapes=[pltpu.VMEM((gate_rows, cols), F32)],
        compiler_params=pltpu.CompilerParams(vmem_limit_bytes=VMEM_LIMIT_BYTES),
        name="select_blocks",
    )(q3, kmean, kn3)


def _moba_sample_kernel(pt_ref, sel_ref, q_ref, kn_ref, vn_ref, ck_hbm, cv_hbm, o_ref, kbuf, vbuf, sem,
                        *, dec_s, n_slots):
    n_pages = 2 * n_slots
    page = kbuf.shape[2]
    n_heads = pl.num_programs(1)
    step = pl.program_id(0) * n_heads + pl.program_id(1)
    n_steps = pl.num_programs(0) * n_heads

    def page_copies(step_idx, slot):
        bb, hh = step_idx // n_heads, step_idx % n_heads
        copies = []
        for c in range(n_pages):
            blk = sel_ref[step_idx * n_slots + c // 2]
            pg = pt_ref[bb, 2 * blk + c % 2]
            src = (pg, slice(None), hh, slice(None))
            copies.append(pltpu.make_async_copy(ck_hbm.at[src], kbuf.at[slot, c], sem.at[0, slot]))
            copies.append(pltpu.make_async_copy(cv_hbm.at[src], vbuf.at[slot, c], sem.at[1, slot]))
        return copies

    @pl.when(step == 0)
    def _():
        for cp in page_copies(step, 0):
            cp.start()

    @pl.when(step + 1 < n_steps)
    def _():
        for cp in page_copies(step + 1, (step + 1) % 2):
            cp.start()

    slot = step % 2
    for cp in page_copies(step, slot):
        cp.wait()
    k_refs = [kbuf.at[slot, c] for c in range(n_pages)]
    v_refs = [vbuf.at[slot, c] for c in range(n_pages)]
    scale = HEAD_DIM ** -0.5
    rows = q_ref.shape[0]
    qb = q_ref[...].astype(BF16)
    row = lax.broadcasted_iota(jnp.int32, (rows, page), 0)
    s_list = []
    m_run = None
    for c in range(n_pages):
        s = _dot_nt(qb, k_refs[c][...].astype(BF16)) * scale
        s = jnp.where(row == c // (2 * MOBA_TOPK), s, NEG_INF)
        s_list.append(s)
        m_c = jnp.max(s, axis=-1, keepdims=True)
        m_run = m_c if m_run is None else jnp.maximum(m_run, m_c)
    s_own = _dot_nt(qb, kn_ref[...].astype(BF16)) * scale
    r_o = lax.broadcasted_iota(jnp.int32, (rows, rows), 0)
    c_o = lax.broadcasted_iota(jnp.int32, (rows, rows), 1)
    s_own = jnp.where((c_o <= r_o) & (c_o < dec_s), s_own, NEG_INF)
    m_run = jnp.maximum(m_run, jnp.max(s_own, axis=-1, keepdims=True))
    p_own = jnp.exp(s_own - m_run)
    l_run = jnp.sum(p_own, axis=-1, keepdims=True)
    acc = _dot(p_own.astype(BF16), vn_ref[...].astype(BF16))
    for c in range(n_pages):
        p = jnp.exp(s_list[c] - m_run)
        l_run = l_run + jnp.sum(p, axis=-1, keepdims=True)
        acc = acc + _dot(p.astype(BF16), v_refs[c][...].astype(BF16))
    o_ref[...] = (acc / l_run).astype(BF16)


def _moba_sample(q3, kn3, vn3, cache_k2, cache_v2, page_table, sel_flat, *, dec_s):
    dec_b, rows, width = q3.shape
    page = cache_k2.shape[1] // N_HEADS
    n_slots = dec_s * MOBA_TOPK
    new_spec = pl.BlockSpec((None, rows, HEAD_DIM), lambda b, h, pt, sel: (b, 0, h))
    hbm_spec = pl.BlockSpec(memory_space=pl.ANY)
    grid_spec = pltpu.PrefetchScalarGridSpec(
        num_scalar_prefetch=2,
        grid=(dec_b, N_HEADS),
        in_specs=[new_spec, new_spec, new_spec, hbm_spec, hbm_spec],
        out_specs=new_spec,
        scratch_shapes=[pltpu.VMEM((2, 2 * n_slots, page, HEAD_DIM), F32),
                        pltpu.VMEM((2, 2 * n_slots, page, HEAD_DIM), F32),
                        pltpu.SemaphoreType.DMA((2, 2))],
    )
    return pl.pallas_call(
        functools.partial(_moba_sample_kernel, dec_s=dec_s, n_slots=n_slots),
        grid_spec=grid_spec,
        out_shape=jax.ShapeDtypeStruct(q3.shape, BF16),
        compiler_params=_params("arbitrary", "arbitrary"),
        name="moba_sample",
    )(page_table, sel_flat, q3, kn3, vn3,
      cache_k2.reshape(-1, page, N_HEADS, HEAD_DIM), cache_v2.reshape(-1, page, N_HEADS, HEAD_DIM))


def _gated_mix_kernel(a_ref, b_ref, h_ref, wa_ref, wb_ref, wga_ref, wgb_ref, o_ref):
    h = h_ref[...]
    ga = jax.nn.sigmoid(_dot(h, wga_ref[...]))
    gb = jax.nn.sigmoid(_dot(h, wgb_ref[...]))
    a = _dot(a_ref[...], wa_ref[...])
    b = _dot(b_ref[...], wb_ref[...])
    o_ref[...] = (ga * a + gb * b).astype(BF16)


def _gated_mix(attn, gm, h, w_a, w_b, w_gate, *, tm, tn):
    m, d = h.shape
    n_col = d // tn
    return pl.pallas_call(
        _gated_mix_kernel,
        grid=(n_col, m // tm),
        in_specs=[pl.BlockSpec((tm, attn.shape[1]), lambda j, i: (i, 0)),
                  pl.BlockSpec((tm, gm.shape[1]), lambda j, i: (i, 0)),
                  pl.BlockSpec((tm, d), lambda j, i: (i, 0)),
                  pl.BlockSpec((w_a.shape[0], tn), lambda j, i: (0, j)),
                  pl.BlockSpec((w_b.shape[0], tn), lambda j, i: (0, j)),
                  pl.BlockSpec((d, tn), lambda j, i: (0, j)),
                  pl.BlockSpec((d, tn), lambda j, i: (0, j + n_col))],
        out_specs=pl.BlockSpec((tm, tn), lambda j, i: (i, j)),
        out_shape=jax.ShapeDtypeStruct((m, d), BF16),
        compiler_params=_params("parallel", "parallel"),
        name="gated_mix",
    )(attn, gm, h, w_a, w_b, w_gate, w_gate)


def _row_chunks(tm, rc):
    rc = min(tm, rc)
    return [slice(r, r + rc) for r in range(0, tm, rc)]


def _mix_out_kernel(mix_ref, x_ref, wo_ref, gpost_ref, gpre_ref, x1_ref, xn_ref):
    for rows in _row_chunks(x_ref.shape[0], NORM_ROW_CHUNK):
        mix = _dot(mix_ref[rows, :], wo_ref[...])
        x1 = x_ref[rows, :] + _rms(mix, gpost_ref[...])
        x1_ref[rows, :] = x1
        xn_ref[rows, :] = _rms(x1, gpre_ref[...]).astype(BF16)


def _mix_out(mixin, x, w_o, g_post, g_pre_ffn, *, tm):
    m, d = x.shape
    row = pl.BlockSpec((tm, d), lambda i: (i, 0))
    vec = pl.BlockSpec((1, d), lambda i: (0, 0))
    return pl.pallas_call(
        _mix_out_kernel,
        grid=(m // tm,),
        in_specs=[row, row, pl.BlockSpec((d, d), lambda i: (0, 0), pipeline_mode=pl.Buffered(1)), vec, vec],
        out_specs=[row, row],
        out_shape=[jax.ShapeDtypeStruct((m, d), F32), jax.ShapeDtypeStruct((m, d), BF16)],
        compiler_params=_params("parallel"),
        name="mix_out",
    )(mixin, x, w_o, g_post, g_pre_ffn)


def _ffn_in_kernel(x_ref, wa_ref, wg_ref, o_ref):
    x = x_ref[...]
    a = _dot(x, wa_ref[...])
    g = _dot(x, wg_ref[...])
    o_ref[...] = (jax.nn.silu(a) * g).astype(BF16)


def _ffn_in_scan_kernel(pt_ref, x_ref, wa_ref, wg_ref, *refs, groups):
    page_refs, o_ref, km_ref = refs[:-2], refs[-2], refs[-1]
    step = pl.program_id(0) * pl.num_programs(1) + pl.program_id(1)
    _scan_key_means(page_refs, km_ref, step % groups)
    _ffn_in_kernel(x_ref, wa_ref, wg_ref, o_ref)


def _ffn_in(xn, w_ffn_in, *, tm, tn, scan=None):
    m, d = xn.shape
    hidden = w_ffn_in.shape[1] // 2
    n_col, n_row = hidden // tn, m // tm
    in_specs = [pl.BlockSpec((tm, d), lambda j, i, *_: (i, 0)),
                pl.BlockSpec((d, tn), lambda j, i, *_: (0, j)),
                pl.BlockSpec((d, tn), lambda j, i, *_: (0, j + n_col))]
    out_spec = pl.BlockSpec((tm, tn), lambda j, i, *_: (i, j))
    out_shape = jax.ShapeDtypeStruct((m, hidden), BF16)
    if scan is None:
        return pl.pallas_call(
            _ffn_in_kernel, grid=(n_col, n_row), in_specs=in_specs, out_specs=out_spec, out_shape=out_shape,
            compiler_params=_params("parallel", "parallel"), name="ffn_in",
        )(xn, w_ffn_in, w_ffn_in)

    cache_k2, page_table = scan
    dec_b, n_pages = page_table.shape
    n_steps = n_col * n_row
    pps = dec_b * n_pages // n_steps
    groups = n_pages // pps
    assert pps * n_steps == dec_b * n_pages and groups * pps == n_pages and pps % 2 == 0
    n_past = n_pages // 2

    def page_spec(c):
        def idx(j, i, pt):
            step = j * n_row + i
            return (pt[step // groups, (step % groups) * pps + c], 0, 0)
        return pl.BlockSpec((None,) + cache_k2.shape[1:], idx)

    grid_spec = pltpu.PrefetchScalarGridSpec(
        num_scalar_prefetch=1,
        grid=(n_col, n_row),
        in_specs=in_specs + [page_spec(c) for c in range(pps)],
        out_specs=[out_spec,
                   pl.BlockSpec((None, N_HEADS, n_past, HEAD_DIM),
                                lambda j, i, pt: ((j * n_row + i) // groups, 0, 0, 0))],
    )
    return pl.pallas_call(
        functools.partial(_ffn_in_scan_kernel, groups=groups),
        grid_spec=grid_spec,
        out_shape=[out_shape, jax.ShapeDtypeStruct((dec_b, N_HEADS, n_past, HEAD_DIM), F32)],
        compiler_params=_params("arbitrary", "arbitrary"),
        name="ffn_in_scan",
    )(page_table, xn, w_ffn_in, w_ffn_in, *([cache_k2] * pps))


def _ffn_out_ple_kernel(hm_ref, w_ref, x1_ref, gpost_ref, gple_ref, p_ref, wpg_ref, wp_ref, y_ref, acc_scr):
    k = pl.program_id(1)
    last = pl.num_programs(1) - 1

    @pl.when(k == 0)
    def _():
        acc_scr[...] = _dot(hm_ref[...], w_ref[...])

    @pl.when((k > 0) & (k < last))
    def _():
        acc_scr[...] += _dot(hm_ref[...], w_ref[...])

    @pl.when(k == last)
    def _():
        for rows in _row_chunks(x1_ref.shape[0], NORM_ROW_CHUNK):
            f = acc_scr[rows, :] + _dot(hm_ref[rows, :], w_ref[...])
            x2 = x1_ref[rows, :] + _rms(f, gpost_ref[...])
            xg = _rms(x2, gple_ref[...]).astype(BF16)
            gate = jax.nn.sigmoid(_dot(xg, wpg_ref[...]))
            y_ref[rows, :] = x2 + _dot(p_ref[rows, :].astype(BF16), wp_ref[...]) * gate


def _ffn_out_ple(hmid, w_ffn_out, x1, g_post_ffn, g_ple, p, w_ple_gate, w_ple, *, tm, tk):
    m, d = x1.shape
    hidden = hmid.shape[1]
    assert hidden // tk >= 2
    row = pl.BlockSpec((tm, d), lambda i, k: (i, 0))
    vec = pl.BlockSpec((1, d), lambda i, k: (0, 0))
    const = lambda shape: pl.BlockSpec(shape, lambda i, k: (0, 0), pipeline_mode=pl.Buffered(1))
    return pl.pallas_call(
        _ffn_out_ple_kernel,
        grid=(m // tm, hidden // tk),
        in_specs=[pl.BlockSpec((tm, tk), lambda i, k: (i, k)),
                  pl.BlockSpec((tk, d), lambda i, k: (k, 0)),
                  row, vec, vec,
                  pl.BlockSpec((tm, p.shape[1]), lambda i, k: (i, 0)),
                  const(w_ple_gate.shape), const(w_ple.shape)],
        out_specs=row,
        out_shape=jax.ShapeDtypeStruct((m, d), F32),
        scratch_shapes=[pltpu.VMEM((tm, d), F32)],
        compiler_params=_params("parallel", "arbitrary"),
        name="ffn_out_ple",
    )(hmid, w_ffn_out, x1, g_post_ffn, g_ple, p, w_ple_gate, w_ple)


def _layer_tail(x, h, attn, gm, p, w, *, tm, scan=None):
    m, d = x.shape
    hidden = w["w_ffn_out"].shape[0]
    mixin = _gated_mix(attn, gm, h, w["w_a_out"], w["w_b_out"], w["w_gate"], tm=min(m, 1024), tn=512)
    x1, xn = _mix_out(mixin, x, w["w_o"], w["g_post_mix"], w["g_pre_ffn"], tm=tm)
    kmean = None
    if scan is None:
        hmid = _ffn_in(xn, w["w_ffn_in"], tm=tm, tn=hidden // 4)
    else:
        hmid, kmean = _ffn_in(xn, w["w_ffn_in"], tm=tm, tn=hidden // 4, scan=scan)
    y = _ffn_out_ple(hmid, w["w_ffn_out"], x1, w["g_post_ffn"], w["g_ple"], p, w["w_ple_gate"], w["w_ple"],
                     tm=tm, tk=hidden // 4)
    return y, kmean


def _rope_tables(pos):
    freqs = jnp.power(jnp.float32(ROPE_THETA), -2.0 * jnp.arange(ROPE_HALF, dtype=F32) / ROPE_DIM)
    ang = pos.astype(F32)[:, None] * freqs[None, :]
    cos, sin = jnp.cos(ang), jnp.sin(ang)
    n = pos.shape[0]
    c = jnp.concatenate([cos, cos, jnp.ones((n, HEAD_DIM - ROPE_DIM), F32)], axis=1)
    sa = jnp.concatenate([-sin, jnp.zeros((n, HEAD_DIM - ROPE_HALF), F32)], axis=1)
    sb = jnp.concatenate([jnp.zeros((n, ROPE_HALF), F32), sin, jnp.zeros((n, HEAD_DIM - ROPE_DIM), F32)], axis=1)
    return c, sa, sb


def kernel(x_prompt, x_sample, cache_k, cache_v, page_table, p_prompt, p_sample, g_pre_mix, w_in, g_vnorm, w_spatial, b_spatial, w_a_out, w_b_out, w_gate, w_o, g_post_mix, g_pre_ffn, w_ffn_in, w_ffn_out, g_post_ffn, g_ple, w_ple_gate, w_ple):
    batch, seq, d = x_prompt.shape
    dec_b, dec_s, _ = x_sample.shape
    depth = w_in.shape[0]
    page = cache_k.shape[2]
    past_len = page_table.shape[1] * page
    n_past = past_len // MOBA_BLOCK
    assert depth == 1 and seq % MOBA_BLOCK == 0 and past_len % MOBA_BLOCK == 0 and MOBA_BLOCK == 2 * page
    assert dec_s <= GMLP_CHUNK and n_past >= MOBA_TOPK
    l = 0

    w = {
        "w_a_out": w_a_out[l].astype(BF16), "w_b_out": w_b_out[l].astype(BF16),
        "w_gate": w_gate[l].astype(BF16), "w_o": w_o[l].astype(BF16),
        "w_ffn_in": w_ffn_in[l].astype(BF16), "w_ffn_out": w_ffn_out[l].astype(BF16),
        "w_ple_gate": w_ple_gate[l].astype(BF16), "w_ple": w_ple[l].astype(BF16),
        "g_post_mix": g_post_mix[l][None], "g_pre_ffn": g_pre_ffn[l][None],
        "g_post_ffn": g_post_ffn[l][None], "g_ple": g_ple[l][None],
    }
    w_in_b = w_in[l].astype(BF16)
    g_pre = g_pre_mix[l][None]
    g_vn = g_vnorm[l][None]

    w_tril = jnp.tril(w_spatial[l])
    wsp_p = w_tril.astype(BF16)
    bsp_p = b_spatial[l].T
    eye_b = jnp.eye(dec_b, dtype=F32)
    wsp_s = jnp.einsum("ab,gts->gatbs", eye_b, w_tril[:, :dec_s, :dec_s]).reshape(
        GMLP_GROUPS, dec_b * dec_s, dec_b * dec_s).astype(BF16)
    bsp_s = jnp.tile(b_spatial[l][:, :dec_s].T, (dec_b, 1))

    mp = batch * seq
    xp = x_prompt.reshape(mp, d)
    cp, sap, sbp = _rope_tables(jnp.arange(seq, dtype=jnp.int32))
    hp, qp, kp, vp, gmp = _mixer(xp, g_pre, w_in_b, cp, sap, sbp, g_vn, wsp_p, bsp_p,
                                 tm=512, chunk=GMLP_CHUNK, with_vn=False)
    ap = _moba_prompt(qp, kp, vp, batch=batch, seq=seq)
    cache_k2 = cache_k.reshape(depth * cache_k.shape[1], page * N_HEADS, HEAD_DIM)
    cache_v2 = cache_v.reshape(depth * cache_v.shape[1], page * N_HEADS, HEAD_DIM)
    yp, kmean = _layer_tail(xp, hp, ap, gmp, p_prompt[l].reshape(mp, -1), w, tm=512,
                            scan=(cache_k2, page_table))

    ms = dec_b * dec_s
    xs = x_sample.reshape(ms, d)
    pos_s = past_len + jnp.arange(dec_s, dtype=jnp.int32)
    cs, sas, sbs = (jnp.tile(t, (dec_b, 1)) for t in _rope_tables(pos_s))
    hs, qs, ks, vs, gms, vns = _mixer(xs, g_pre, w_in_b, cs, sas, sbs, g_vn, wsp_s, bsp_s,
                                      tm=ms, chunk=ms, with_vn=True)
    q3 = qs.reshape(dec_b, dec_s, ATTN_WIDTH)
    kn3 = ks.reshape(dec_b, dec_s, ATTN_WIDTH)
    vn3 = vs.reshape(dec_b, dec_s, ATTN_WIDTH)
    sel = _select_blocks(q3, kmean, kn3, n_past=n_past)
    sel_flat = sel[:MOBA_TOPK].T.reshape(-1)
    pad = lambda t: jnp.pad(t, ((0, 0), (0, SAMPLE_ROWS - dec_s), (0, 0)))
    a_s = _moba_sample(pad(q3), pad(kn3), pad(vn3), cache_k2, cache_v2, page_table, sel_flat, dec_s=dec_s)
    a_s = a_s[:, :dec_s].reshape(ms, ATTN_WIDTH)
    ys, _ = _layer_tail(xs, hs, a_s, gms, p_sample[l].reshape(ms, -1), w, tm=ms)

    return (yp.reshape(batch, seq, d), ys.reshape(dec_b, dec_s, d),
            kp.reshape(1, batch, seq, N_HEADS, HEAD_DIM), vp.reshape(1, batch, seq, N_HEADS, HEAD_DIM),
            ks.reshape(1, dec_b, dec_s, N_HEADS, HEAD_DIM), vs.reshape(1, dec_b, dec_s, N_HEADS, HEAD_DIM),
            vns.reshape(1, dec_b, dec_s, GMLP_WIDTH))
```

```python
import functools

import jax
import jax.numpy as jnp
from jax import lax
from jax.experimental import pallas as pl
from jax.experimental.pallas import tpu as pltpu

F32 = jnp.float32
BF16 = jnp.bfloat16

N_HEADS = 8
HEAD_DIM = 128
ATTN_WIDTH = N_HEADS * HEAD_DIM
MOBA_BLOCK = 256
MOBA_TOPK = 3
ROPE_THETA = 500000.0
ROPE_DIM = HEAD_DIM // 4
ROPE_HALF = ROPE_DIM // 2
GMLP_GROUPS = 8
GMLP_CHUNK = 128
GMLP_WIDTH = 1024
GMLP_GROUP_DIM = GMLP_WIDTH // GMLP_GROUPS
NORM_EPS = 1e-6
NEG_INF = -1e30
TAKEN = -3e38
SAMPLE_ROWS = 16
ROW_CHUNK = 256
NORM_ROW_CHUNK = 128
BF16_SUBLANES = 16

VMEM_LIMIT_BYTES = 56 * 1024 * 1024


def _params(*semantics):
    return pltpu.CompilerParams(dimension_semantics=semantics, vmem_limit_bytes=VMEM_LIMIT_BYTES)


def _rms(x, g):
    return x * lax.rsqrt(jnp.mean(x * x, axis=-1, keepdims=True) + NORM_EPS) * g


def _dot(a, b):
    return jnp.dot(a, b, preferred_element_type=F32)


def _dot_nt(a, b):
    return lax.dot_general(a, b, (((1,), (1,)), ((), ())), preferred_element_type=F32)


def _mixer_kernel(x_ref, g_ref, w_ref, cos_ref, sa_ref, sb_ref, gvn_ref, wsp_ref, bsp_ref,
                  h_out, q_out, k_out, v_out, gm_out, *rest, chunk, rc, with_vn):
    if with_vn:
        vn_out, u_scr, h_scr = rest
    else:
        u_scr, h_scr = rest
    j = pl.program_id(1)
    tm = x_ref.shape[0]
    row_chunks = [slice(r, r + rc) for r in range(0, tm, rc)]

    def rope_to(out_ref, z, rows):
        c, sa, sb = cos_ref[rows, :], sa_ref[rows, :], sb_ref[rows, :]
        for hd in range(N_HEADS):
            sl = slice(hd * HEAD_DIM, (hd + 1) * HEAD_DIM)
            zs = z[:, sl]
            out_ref[rows, sl] = (zs * c + pltpu.roll(zs, HEAD_DIM - ROPE_HALF, 1) * sa
                                 + pltpu.roll(zs, ROPE_HALF, 1) * sb)

    @pl.when(j == 0)
    def _():
        for rows in row_chunks:
            hb = _rms(x_ref[rows, :], g_ref[...]).astype(BF16)
            h_out[rows, :] = hb
            h_scr[rows, :] = hb
            rope_to(q_out, _dot(hb, w_ref[...]), rows)

    @pl.when(j == 1)
    def _():
        for rows in row_chunks:
            rope_to(k_out, _dot(h_scr[rows, :], w_ref[...]), rows)

    @pl.when(j == 2)
    def _():
        for rows in row_chunks:
            v_out[rows, :] = _dot(h_scr[rows, :], w_ref[...])

    @pl.when(j == 3)
    def _():
        for rows in row_chunks:
            u_scr[rows, :] = jax.nn.gelu(_dot(h_scr[rows, :], w_ref[...]))

    @pl.when(j == 4)
    def _():
        for rows in row_chunks:
            vg = jax.nn.gelu(_dot(h_scr[rows, :], w_ref[...]))
            xc = vg - jnp.mean(vg, axis=-1, keepdims=True)
            vn = xc * lax.rsqrt(jnp.mean(xc * xc, axis=-1, keepdims=True) + NORM_EPS) * gvn_ref[...]
            if with_vn:
                vn_out[rows, :] = vn
            vnb = vn.astype(BF16)
            for c in range(rc // chunk):
                sub = slice(c * chunk, (c + 1) * chunk)
                dst = slice(rows.start + c * chunk, rows.start + (c + 1) * chunk)
                for g in range(GMLP_GROUPS):
                    cols = slice(g * GMLP_GROUP_DIM, (g + 1) * GMLP_GROUP_DIM)
                    s = _dot(wsp_ref[g], vnb[sub, cols]) + bsp_ref[:, g:g + 1]
                    gm_out[dst, cols] = (u_scr[dst, cols] * s).astype(BF16)


def _mixer(x, g_pre, w_in, cos_t, sa_t, sb_t, g_vn, wsp, bsp, *, tm, chunk, with_vn):
    m, d = x.shape
    n_seg = w_in.shape[1] // ATTN_WIDTH
    t_blocks = cos_t.shape[0] // tm
    tab_spec = pl.BlockSpec((tm, HEAD_DIM), lambda i, j: (i % t_blocks, 0))
    row_spec = lambda w: pl.BlockSpec((tm, w), lambda i, j: (i, 0))
    n_row = m // tm

    def early_spec(w, seg):
        return pl.BlockSpec((tm, w), lambda i, j: (jnp.minimum(i + (j > seg).astype(jnp.int32), n_row - 1), 0))

    out_shape = [jax.ShapeDtypeStruct((m, d), BF16),
                 jax.ShapeDtypeStruct((m, ATTN_WIDTH), F32),
                 jax.ShapeDtypeStruct((m, ATTN_WIDTH), F32),
                 jax.ShapeDtypeStruct((m, ATTN_WIDTH), F32),
                 jax.ShapeDtypeStruct((m, GMLP_WIDTH), BF16)]
    out_specs = [early_spec(d, 0), early_spec(ATTN_WIDTH, 0), early_spec(ATTN_WIDTH, 1), early_spec(ATTN_WIDTH, 2),
                 row_spec(GMLP_WIDTH)]
    if with_vn:
        out_shape.append(jax.ShapeDtypeStruct((m, GMLP_WIDTH), F32))
        out_specs.append(row_spec(GMLP_WIDTH))
    return pl.pallas_call(
        functools.partial(_mixer_kernel, chunk=chunk, rc=max(chunk, min(tm, ROW_CHUNK)), with_vn=with_vn),
        grid=(m // tm, n_seg),
        in_specs=[row_spec(d),
                  pl.BlockSpec((1, d), lambda i, j: (0, 0)),
                  pl.BlockSpec((d, ATTN_WIDTH), lambda i, j: (0, j)),
                  tab_spec, tab_spec, tab_spec,
                  pl.BlockSpec((1, GMLP_WIDTH), lambda i, j: (0, 0)),
                  pl.BlockSpec((GMLP_GROUPS, chunk, chunk), lambda i, j: (0, 0, 0)),
                  pl.BlockSpec((chunk, GMLP_GROUPS), lambda i, j: (0, 0))],
        out_specs=out_specs,
        out_shape=out_shape,
        scratch_shapes=[pltpu.VMEM((tm, GMLP_WIDTH), F32), pltpu.VMEM((tm, d), BF16)],
        compiler_params=_params("arbitrary", "arbitrary"),
        name="mixer",
    )(x, g_pre, w_in, cos_t, sa_t, sb_t, g_vn, wsp, bsp)


def _moba_prompt_kernel(q_ref, k_ref, v_ref, *refs):
    n_cast = (len(refs) - 4) // 2
    o_ref = refs[n_cast]
    kb_scr, vt_scr, s_scr = refs[-3:]
    for src_ref, dst_ref in zip(refs[:n_cast], refs[n_cast + 1:2 * n_cast + 1]):
        dst_ref[...] = src_ref[...].astype(BF16)
    seq = q_ref.shape[0]
    nblk = seq // MOBA_BLOCK
    blk = MOBA_BLOCK
    scale = HEAD_DIM ** -0.5
    kb_scr[...] = k_ref[...].astype(BF16)
    vt_scr[...] = v_ref[...].T.astype(BF16)
    qt = q_ref[...].T
    kt = k_ref[...].T

    blk_id = lax.broadcasted_iota(jnp.int32, (nblk, seq), 0)
    q_blk = lax.broadcasted_iota(jnp.int32, (nblk, seq), 1) // blk
    gate = jnp.zeros((nblk, seq), F32)
    for n in range(nblk):
        kmean_n = jnp.mean(kt[:, n * blk:(n + 1) * blk], axis=1, keepdims=True)
        g_n = jnp.sum(qt * kmean_n, axis=0, keepdims=True)
        gate = jnp.where(blk_id == n, g_n, gate)
    rank = jnp.zeros((nblk, seq), jnp.int32)
    for m in range(nblk):
        g_m = gate[m:m + 1, :]
        beats = (m < q_blk) & ((g_m > gate) | ((g_m == gate) & (m < blk_id)))
        rank = rank + beats.astype(jnp.int32)
    keep = jnp.where((blk_id < q_blk) & (rank < MOBA_TOPK), 1.0, 0.0)

    key_i = lax.broadcasted_iota(jnp.int32, (blk, blk), 0)
    qry_i = lax.broadcasted_iota(jnp.int32, (blk, blk), 1)
    causal = key_i <= qry_i

    for j in range(nblk):
        cols = slice(j * blk, (j + 1) * blk)
        qb = q_ref[cols, :].astype(BF16)
        keep_j = keep[:, cols]
        m_run = None
        for n in range(j + 1):
            st = _dot_nt(kb_scr[n * blk:(n + 1) * blk, :], qb) * scale
            if n == j:
                st = jnp.where(causal, st, NEG_INF)
            else:
                st = jnp.where(keep_j[n:n + 1, :] > 0.5, st, NEG_INF)
            s_scr[n] = st
            m_n = jnp.max(st, axis=0, keepdims=True)
            m_run = m_n if m_run is None else jnp.maximum(m_run, m_n)
        l_run = jnp.zeros((1, blk), F32)
        acc = jnp.zeros((HEAD_DIM, blk), F32)
        for n in range(j + 1):
            p = jnp.exp(s_scr[n] - m_run)
            l_run = l_run + jnp.sum(p, axis=0, keepdims=True)
            acc = acc + _dot(vt_scr[:, n * blk:(n + 1) * blk], p.astype(BF16))
        o_ref[cols, :] = (acc / l_run).T.astype(BF16)


def _moba_prompt(q, k, v, *, batch, seq, casts=()):
    n_steps = batch * N_HEADS
    spec = pl.BlockSpec((seq, HEAD_DIM), lambda b, h: (b, h))

    def slab_spec(wt):
        rows = wt.shape[0] // n_steps
        assert rows * n_steps == wt.shape[0] and rows % BF16_SUBLANES == 0
        return pl.BlockSpec((rows, wt.shape[1]), lambda b, h: (b * N_HEADS + h, 0))

    slabs = [slab_spec(wt) for wt in casts]
    out = pl.pallas_call(
        _moba_prompt_kernel,
        grid=(batch, N_HEADS),
        in_specs=[spec, spec, spec] + slabs,
        out_specs=[spec] + slabs,
        out_shape=[jax.ShapeDtypeStruct(q.shape, BF16)] + [jax.ShapeDtypeStruct(wt.shape, BF16) for wt in casts],
        scratch_shapes=[pltpu.VMEM((seq, HEAD_DIM), BF16), pltpu.VMEM((HEAD_DIM, seq), BF16),
                        pltpu.VMEM((seq // MOBA_BLOCK, MOBA_BLOCK, MOBA_BLOCK), F32)],
        compiler_params=_params("parallel", "parallel"),
        name="moba_prompt",
    )(q, k, v, *casts)
    return out[0], tuple(out[1:])


def _scan_key_means(page_refs, km_ref, group):
    n_blk = len(page_refs) // 2
    page_rows = page_refs[0].shape[0] // N_HEADS

    def page_sum(ref):
        return jnp.sum(ref[...].reshape(page_rows, N_HEADS, HEAD_DIM), axis=0)

    for t in range(n_blk):
        tot = (page_sum(page_refs[2 * t]) + page_sum(page_refs[2 * t + 1])) * (1.0 / MOBA_BLOCK)
        for h in range(N_HEADS):
            km_ref[h, pl.ds(group * n_blk + t, 1), :] = tot[h:h + 1, :]


def _select_kernel(q_ref, km_ref, kn_ref, sel_ref, gate_scr, *, dec_s, n_past):
    dec_b = q_ref.shape[0]
    gate_scr[...] = jnp.full(gate_scr.shape, NEG_INF, F32)
    for b in range(dec_b):
        own_mean = jnp.sum(kn_ref[b], axis=0, keepdims=True) * (1.0 / MOBA_BLOCK)
        for h in range(N_HEADS):
            cols = slice(h * HEAD_DIM, (h + 1) * HEAD_DIM)
            km = km_ref[b, h]
            for i in range(dec_s):
                col = (b * N_HEADS + h) * dec_s + i
                qi = q_ref[b, i:i + 1, cols]
                gate_scr[0:n_past, col:col + 1] = jnp.sum(km * qi, axis=-1, keepdims=True)
                gate_scr[n_past:n_past + 1, col:col + 1] = jnp.sum(qi * own_mean[:, cols], axis=-1,
                                                                   keepdims=True)
    gate = gate_scr[...]
    blk = lax.broadcasted_iota(jnp.int32, gate.shape, 0)
    gate = jnp.where(blk < n_past, gate, NEG_INF)
    out_row = lax.broadcasted_iota(jnp.int32, sel_ref.shape, 0)
    sel = jnp.zeros(sel_ref.shape, jnp.int32)
    for t in range(MOBA_TOPK):
        best = jnp.max(gate, axis=0, keepdims=True)
        idx = jnp.min(jnp.where(gate == best, blk, gate.shape[0]), axis=0, keepdims=True)
        sel = jnp.where(out_row == t, idx, sel)
        gate = jnp.where(blk == idx, TAKEN, gate)
    sel_ref[...] = sel


def _select_blocks(q3, kmean, kn3, *, n_past):
    dec_b, dec_s, width = q3.shape
    cols = dec_b * N_HEADS * dec_s
    gate_rows = 8 * (-(-(n_past + 1) // 8))
    return pl.pallas_call(
        functools.partial(_select_kernel, dec_s=dec_s, n_past=n_past),
        out_shape=jax.ShapeDtypeStruct((8, cols), jnp.int32),
        scratch_shapes=[pltpu.VMEM((gate_rows, cols), F32)],
        compiler_params=pltpu.CompilerParams(vmem_limit_bytes=VMEM_LIMIT_BYTES),
        name="select_blocks",
    )(q3, kmean, kn3)


def _sample_fetch(step, n_steps, pt_ref, sel_ref, ck_hbm, cv_hbm, kbuf, vbuf, sem, *, n_slots):
    n_pages = 2 * n_slots

    def page_copies(step_idx, slot):
        bb, hh = step_idx // N_HEADS, step_idx % N_HEADS
        copies = []
        for c in range(n_pages):
            blk = sel_ref[step_idx * n_slots + c // 2]
            pg = pt_ref[bb, 2 * blk + c % 2]
            src = (pg, slice(None), hh, slice(None))
            copies.append(pltpu.make_async_copy(ck_hbm.at[src], kbuf.at[slot, c], sem.at[0, slot]))
            copies.append(pltpu.make_async_copy(cv_hbm.at[src], vbuf.at[slot, c], sem.at[1, slot]))
        return copies

    @pl.when(step == 0)
    def _():
        for cp in page_copies(step, 0):
            cp.start()

    @pl.when(step + 1 < n_steps)
    def _():
        for cp in page_copies(step + 1, (step + 1) % 2):
            cp.start()

    slot = step % 2
    for cp in page_copies(step, slot):
        cp.wait()
    return slot


def _sample_attend(slot, q_ref, kn_ref, vn_ref, o_ref, kbuf, vbuf, *, dec_s, n_slots):
    n_pages = 2 * n_slots
    page = kbuf.shape[2]
    k_refs = [kbuf.at[slot, c] for c in range(n_pages)]
    v_refs = [vbuf.at[slot, c] for c in range(n_pages)]
    scale = HEAD_DIM ** -0.5
    rows = q_ref.shape[0]
    qb = q_ref[...].astype(BF16)
    row = lax.broadcasted_iota(jnp.int32, (rows, page), 0)
    s_list = []
    m_run = None
    for c in range(n_pages):
        s = _dot_nt(qb, k_refs[c][...].astype(BF16)) * scale
        s = jnp.where(row == c // (2 * MOBA_TOPK), s, NEG_INF)
        s_list.append(s)
        m_c = jnp.max(s, axis=-1, keepdims=True)
        m_run = m_c if m_run is None else jnp.maximum(m_run, m_c)
    s_own = _dot_nt(qb, kn_ref[...].astype(BF16)) * scale
    r_o = lax.broadcasted_iota(jnp.int32, (rows, rows), 0)
    c_o = lax.broadcasted_iota(jnp.int32, (rows, rows), 1)
    s_own = jnp.where((c_o <= r_o) & (c_o < dec_s), s_own, NEG_INF)
    m_run = jnp.maximum(m_run, jnp.max(s_own, axis=-1, keepdims=True))
    p_own = jnp.exp(s_own - m_run)
    l_run = jnp.sum(p_own, axis=-1, keepdims=True)
    acc = _dot(p_own.astype(BF16), vn_ref[...].astype(BF16))
    for c in range(n_pages):
        p = jnp.exp(s_list[c] - m_run)
        l_run = l_run + jnp.sum(p, axis=-1, keepdims=True)
        acc = acc + _dot(p.astype(BF16), v_refs[c][...].astype(BF16))
    o_ref[...] = (acc / l_run).astype(BF16)


def _gated_mix_kernel(a_ref, b_ref, h_ref, wa_ref, wb_ref, wga_ref, wgb_ref, o_ref):
    h = h_ref[...]
    ga = jax.nn.sigmoid(_dot(h, wga_ref[...]))
    gb = jax.nn.sigmoid(_dot(h, wgb_ref[...]))
    a = _dot(a_ref[...], wa_ref[...])
    b = _dot(b_ref[...], wb_ref[...])
    o_ref[...] = (ga * a + gb * b).astype(BF16)


def _gated_mix(attn, gm, h, w_a, w_b, w_gate, *, tm, tn):
    m, d = h.shape
    n_col = d // tn
    return pl.pallas_call(
        _gated_mix_kernel,
        grid=(n_col, m // tm),
        in_specs=[pl.BlockSpec((tm, attn.shape[1]), lambda j, i: (i, 0)),
                  pl.BlockSpec((tm, gm.shape[1]), lambda j, i: (i, 0)),
                  pl.BlockSpec((tm, d), lambda j, i: (i, 0)),
                  pl.BlockSpec((w_a.shape[0], tn), lambda j, i: (0, j)),
                  pl.BlockSpec((w_b.shape[0], tn), lambda j, i: (0, j)),
                  pl.BlockSpec((d, tn), lambda j, i: (0, j)),
                  pl.BlockSpec((d, tn), lambda j, i: (0, j + n_col))],
        out_specs=pl.BlockSpec((tm, tn), lambda j, i: (i, j)),
        out_shape=jax.ShapeDtypeStruct((m, d), BF16),
        compiler_params=_params("parallel", "parallel"),
        name="gated_mix",
    )(attn, gm, h, w_a, w_b, w_gate, w_gate)


def _row_chunks(tm, rc):
    rc = min(tm, rc)
    return [slice(r, r + rc) for r in range(0, tm, rc)]


def _mix_out_kernel(mix_ref, x_ref, wo_ref, gpost_ref, gpre_ref, x1_ref, xn_ref):
    for rows in _row_chunks(x_ref.shape[0], NORM_ROW_CHUNK):
        mix = _dot(mix_ref[rows, :], wo_ref[...])
        x1 = x_ref[rows, :] + _rms(mix, gpost_ref[...])
        x1_ref[rows, :] = x1
        xn_ref[rows, :] = _rms(x1, gpre_ref[...]).astype(BF16)


def _mix_out(mixin, x, w_o, g_post, g_pre_ffn, *, tm):
    m, d = x.shape
    row = pl.BlockSpec((tm, d), lambda i: (i, 0))
    vec = pl.BlockSpec((1, d), lambda i: (0, 0))
    return pl.pallas_call(
        _mix_out_kernel,
        grid=(m // tm,),
        in_specs=[row, row, pl.BlockSpec((d, d), lambda i: (0, 0), pipeline_mode=pl.Buffered(1)), vec, vec],
        out_specs=[row, row],
        out_shape=[jax.ShapeDtypeStruct((m, d), F32), jax.ShapeDtypeStruct((m, d), BF16)],
        compiler_params=_params("parallel"),
        name="mix_out",
    )(mixin, x, w_o, g_post, g_pre_ffn)


def _ffn_in_kernel(x_ref, wa_ref, wg_ref, o_ref):
    x = x_ref[...]
    a = _dot(x, wa_ref[...])
    g = _dot(x, wg_ref[...])
    o_ref[...] = (jax.nn.silu(a) * g).astype(BF16)


def _ffn_in_scan_kernel(pt_ref, x_ref, wa_ref, wg_ref, *refs, groups):
    page_refs, o_ref, km_ref = refs[:-2], refs[-2], refs[-1]
    step = pl.program_id(0) * pl.num_programs(1) + pl.program_id(1)
    _scan_key_means(page_refs, km_ref, step % groups)
    _ffn_in_kernel(x_ref, wa_ref, wg_ref, o_ref)


def _ffn_in(xn, w_ffn_in, *, tm, tn, scan=None):
    m, d = xn.shape
    hidden = w_ffn_in.shape[1] // 2
    n_col, n_row = hidden // tn, m // tm
    in_specs = [pl.BlockSpec((tm, d), lambda j, i, *_: (i, 0)),
                pl.BlockSpec((d, tn), lambda j, i, *_: (0, j)),
                pl.BlockSpec((d, tn), lambda j, i, *_: (0, j + n_col))]
    out_spec = pl.BlockSpec((tm, tn), lambda j, i, *_: (i, j))
    out_shape = jax.ShapeDtypeStruct((m, hidden), BF16)
    if scan is None:
        return pl.pallas_call(
            _ffn_in_kernel, grid=(n_col, n_row), in_specs=in_specs, out_specs=out_spec, out_shape=out_shape,
            compiler_params=_params("parallel", "parallel"), name="ffn_in",
        )(xn, w_ffn_in, w_ffn_in)

    cache_k2, page_table = scan
    dec_b, n_pages = page_table.shape
    n_steps = n_col * n_row
    pps = dec_b * n_pages // n_steps
    groups = n_pages // pps
    assert pps * n_steps == dec_b * n_pages and groups * pps == n_pages and pps % 2 == 0
    n_past = n_pages // 2

    def page_spec(c):
        def idx(j, i, pt):
            step = j * n_row + i
            return (pt[step // groups, (step % groups) * pps + c], 0, 0)
        return pl.BlockSpec((None,) + cache_k2.shape[1:], idx)

    grid_spec = pltpu.PrefetchScalarGridSpec(
        num_scalar_prefetch=1,
        grid=(n_col, n_row),
        in_specs=in_specs + [page_spec(c) for c in range(pps)],
        out_specs=[out_spec,
                   pl.BlockSpec((None, N_HEADS, n_past, HEAD_DIM),
                                lambda j, i, pt: ((j * n_row + i) // groups, 0, 0, 0))],
    )
    return pl.pallas_call(
        functools.partial(_ffn_in_scan_kernel, groups=groups),
        grid_spec=grid_spec,
        out_shape=[out_shape, jax.ShapeDtypeStruct((dec_b, N_HEADS, n_past, HEAD_DIM), F32)],
        compiler_params=_params("arbitrary", "arbitrary"),
        name="ffn_in_scan",
    )(page_table, xn, w_ffn_in, w_ffn_in, *([cache_k2] * pps))


def _ffn_out_ple_kernel(hm_ref, w_ref, x1_ref, gpost_ref, gple_ref, p_ref, wpg_ref, wp_ref, y_ref, acc_scr,
                        side_work=lambda: None):
    k = pl.program_id(1)
    last = pl.num_programs(1) - 1

    @pl.when(k == 0)
    def _():
        side_work()
        acc_scr[...] = _dot(hm_ref[...], w_ref[...])

    @pl.when((k > 0) & (k < last))
    def _():
        side_work()
        acc_scr[...] += _dot(hm_ref[...], w_ref[...])

    @pl.when(k == last)
    def _():
        side_work()
        for rows in _row_chunks(x1_ref.shape[0], NORM_ROW_CHUNK):
            f = acc_scr[rows, :] + _dot(hm_ref[rows, :], w_ref[...])
            x2 = x1_ref[rows, :] + _rms(f, gpost_ref[...])
            xg = _rms(x2, gple_ref[...]).astype(BF16)
            gate = jax.nn.sigmoid(_dot(xg, wpg_ref[...]))
            y_ref[rows, :] = x2 + _dot(p_ref[rows, :].astype(BF16), wp_ref[...]) * gate


def _ffn_out_ple_attend_kernel(pt_ref, sel_ref, hm_ref, w_ref, x1_ref, gpost_ref, gple_ref, p_ref, wpg_ref,
                               wp_ref, q_ref, kn_ref, vn_ref, ck_hbm, cv_hbm, y_ref, a_ref,
                               acc_scr, kbuf, vbuf, sem, *, dec_s, n_slots):
    step = pl.program_id(0) * pl.num_programs(1) + pl.program_id(1)
    n_steps = pl.num_programs(0) * pl.num_programs(1)
    slot = _sample_fetch(step, n_steps, pt_ref, sel_ref, ck_hbm, cv_hbm, kbuf, vbuf, sem, n_slots=n_slots)
    attend = functools.partial(_sample_attend, slot, q_ref, kn_ref, vn_ref, a_ref, kbuf, vbuf,
                               dec_s=dec_s, n_slots=n_slots)
    _ffn_out_ple_kernel(hm_ref, w_ref, x1_ref, gpost_ref, gple_ref, p_ref, wpg_ref, wp_ref, y_ref, acc_scr,
                        side_work=attend)


def _ffn_out_ple(hmid, w_ffn_out, x1, g_post_ffn, g_ple, p, w_ple_gate, w_ple, *, tm, tk, attend=None):
    m, d = x1.shape
    hidden = hmid.shape[1]
    n_row, n_k = m // tm, hidden // tk
    assert n_k >= 2
    row = pl.BlockSpec((tm, d), lambda i, k, *_: (i, 0))
    vec = pl.BlockSpec((1, d), lambda i, k, *_: (0, 0))
    const = lambda shape: pl.BlockSpec(shape, lambda i, k, *_: (0, 0), pipeline_mode=pl.Buffered(1))
    in_specs = [pl.BlockSpec((tm, tk), lambda i, k, *_: (i, k)),
                pl.BlockSpec((tk, d), lambda i, k, *_: (k, 0)),
                row, vec, vec,
                pl.BlockSpec((tm, p.shape[1]), lambda i, k, *_: (i, 0)),
                const(w_ple_gate.shape), const(w_ple.shape)]
    args = (hmid, w_ffn_out, x1, g_post_ffn, g_ple, p, w_ple_gate, w_ple)
    y_shape = jax.ShapeDtypeStruct((m, d), F32)
    acc = pltpu.VMEM((tm, d), F32)
    if attend is None:
        return pl.pallas_call(
            _ffn_out_ple_kernel, grid=(n_row, n_k), in_specs=in_specs, out_specs=row, out_shape=y_shape,
            scratch_shapes=[acc], compiler_params=_params("parallel", "arbitrary"), name="ffn_out_ple",
        )(*args)

    q3, kn3, vn3, cache_k4, cache_v4, page_table, sel_flat, dec_s = attend
    dec_b, rows, _ = q3.shape
    page = cache_k4.shape[1]
    n_slots = dec_s * MOBA_TOPK
    assert n_row * n_k == dec_b * N_HEADS
    pair_spec = pl.BlockSpec((None, rows, HEAD_DIM),
                             lambda i, k, *_: ((i * n_k + k) // N_HEADS, 0, (i * n_k + k) % N_HEADS))
    hbm_spec = pl.BlockSpec(memory_space=pl.ANY)
    grid_spec = pltpu.PrefetchScalarGridSpec(
        num_scalar_prefetch=2,
        grid=(n_row, n_k),
        in_specs=in_specs + [pair_spec, pair_spec, pair_spec, hbm_spec, hbm_spec],
        out_specs=[row, pair_spec],
        scratch_shapes=[acc,
                        pltpu.VMEM((2, 2 * n_slots, page, HEAD_DIM), F32),
                        pltpu.VMEM((2, 2 * n_slots, page, HEAD_DIM), F32),
                        pltpu.SemaphoreType.DMA((2, 2))],
    )
    return pl.pallas_call(
        functools.partial(_ffn_out_ple_attend_kernel, dec_s=dec_s, n_slots=n_slots),
        grid_spec=grid_spec,
        out_shape=[y_shape, jax.ShapeDtypeStruct(q3.shape, BF16)],
        compiler_params=_params("arbitrary", "arbitrary"),
        name="ffn_out_ple_attend",
    )(page_table, sel_flat, *args, q3, kn3, vn3, cache_k4, cache_v4)


def _tail_front(x, h, attn, gm, w, *, tm, scan=None):
    m, d = x.shape
    hidden = w["w_ffn_out"].shape[0]
    mixin = _gated_mix(attn, gm, h, w["w_a_out"], w["w_b_out"], w["w_gate"], tm=min(m, 1024), tn=512)
    x1, xn = _mix_out(mixin, x, w["w_o"], w["g_post_mix"], w["g_pre_ffn"], tm=tm)
    if scan is None:
        return x1, _ffn_in(xn, w["w_ffn_in"], tm=tm, tn=hidden // 4), None
    hmid, kmean = _ffn_in(xn, w["w_ffn_in"], tm=tm, tn=hidden // 4, scan=scan)
    return x1, hmid, kmean


def _tail_back(x1, hmid, p, w, *, tm, attend=None):
    return _ffn_out_ple(hmid, w["w_ffn_out"], x1, w["g_post_ffn"], w["g_ple"], p, w["w_ple_gate"], w["w_ple"],
                        tm=tm, tk=w["w_ffn_out"].shape[0] // 4, attend=attend)


def _rope_tables(pos):
    freqs = jnp.power(jnp.float32(ROPE_THETA), -2.0 * jnp.arange(ROPE_HALF, dtype=F32) / ROPE_DIM)
    ang = pos.astype(F32)[:, None] * freqs[None, :]
    cos, sin = jnp.cos(ang), jnp.sin(ang)
    n = pos.shape[0]
    c = jnp.concatenate([cos, cos, jnp.ones((n, HEAD_DIM - ROPE_DIM), F32)], axis=1)
    sa = jnp.concatenate([-sin, jnp.zeros((n, HEAD_DIM - ROPE_HALF), F32)], axis=1)
    sb = jnp.concatenate([jnp.zeros((n, ROPE_HALF), F32), sin, jnp.zeros((n, HEAD_DIM - ROPE_DIM), F32)], axis=1)
    return c, sa, sb


def kernel(x_prompt, x_sample, cache_k, cache_v, page_table, p_prompt, p_sample, g_pre_mix, w_in, g_vnorm, w_spatial, b_spatial, w_a_out, w_b_out, w_gate, w_o, g_post_mix, g_pre_ffn, w_ffn_in, w_ffn_out, g_post_ffn, g_ple, w_ple_gate, w_ple):
    batch, seq, d = x_prompt.shape
    dec_b, dec_s, _ = x_sample.shape
    depth = w_in.shape[0]
    page = cache_k.shape[2]
    past_len = page_table.shape[1] * page
    n_past = past_len // MOBA_BLOCK
    assert depth == 1 and seq % MOBA_BLOCK == 0 and past_len % MOBA_BLOCK == 0 and MOBA_BLOCK == 2 * page
    assert dec_s <= GMLP_CHUNK and n_past >= MOBA_TOPK
    l = 0

    w = {
        "w_ple": w_ple[l].astype(BF16),
        "g_post_mix": g_post_mix[l][None], "g_pre_ffn": g_pre_ffn[l][None],
        "g_post_ffn": g_post_ffn[l][None], "g_ple": g_ple[l][None],
    }
    tail_weights = {"w_a_out": w_a_out[l], "w_b_out": w_b_out[l], "w_gate": w_gate[l], "w_o": w_o[l],
                    "w_ffn_in": w_ffn_in[l], "w_ffn_out": w_ffn_out[l], "w_ple_gate": w_ple_gate[l]}
    w_in_b = w_in[l].astype(BF16)
    g_pre = g_pre_mix[l][None]
    g_vn = g_vnorm[l][None]

    w_tril = jnp.tril(w_spatial[l])
    wsp_p = w_tril.astype(BF16)
    bsp_p = b_spatial[l].T
    eye_b = jnp.eye(dec_b, dtype=F32)
    wsp_s = jnp.einsum("ab,gts->gatbs", eye_b, w_tril[:, :dec_s, :dec_s]).reshape(
        GMLP_GROUPS, dec_b * dec_s, dec_b * dec_s).astype(BF16)
    bsp_s = jnp.tile(b_spatial[l][:, :dec_s].T, (dec_b, 1))

    mp = batch * seq
    xp = x_prompt.reshape(mp, d)
    cp, sap, sbp = _rope_tables(jnp.arange(seq, dtype=jnp.int32))
    hp, qp, kp, vp, gmp = _mixer(xp, g_pre, w_in_b, cp, sap, sbp, g_vn, wsp_p, bsp_p,
                                 tm=512, chunk=GMLP_CHUNK, with_vn=False)
    ms = dec_b * dec_s
    xs = x_sample.reshape(ms, d)
    pos_s = past_len + jnp.arange(dec_s, dtype=jnp.int32)
    cs, sas, sbs = (jnp.tile(t, (dec_b, 1)) for t in _rope_tables(pos_s))
    hs, qs, ks, vs, gms, vns = _mixer(xs, g_pre, w_in_b, cs, sas, sbs, g_vn, wsp_s, bsp_s,
                                      tm=ms, chunk=ms, with_vn=True)

    ap, cast = _moba_prompt(qp, kp, vp, batch=batch, seq=seq, casts=tuple(tail_weights.values()))
    w.update(zip(tail_weights.keys(), cast))
    n_pool = depth * cache_k.shape[1]
    cache_k2 = cache_k.reshape(n_pool, page * N_HEADS, HEAD_DIM)
    x1p, hmid_p, kmean = _tail_front(xp, hp, ap, gmp, w, tm=512, scan=(cache_k2, page_table))
    q3 = qs.reshape(dec_b, dec_s, ATTN_WIDTH)
    kn3 = ks.reshape(dec_b, dec_s, ATTN_WIDTH)
    vn3 = vs.reshape(dec_b, dec_s, ATTN_WIDTH)
    sel = _select_blocks(q3, kmean, kn3, n_past=n_past)
    sel_flat = sel[:MOBA_TOPK].T.reshape(-1)
    pad = lambda t: jnp.pad(t, ((0, 0), (0, SAMPLE_ROWS - dec_s), (0, 0)))
    attend = (pad(q3), pad(kn3), pad(vn3), cache_k.reshape(n_pool, page, N_HEADS, HEAD_DIM),
              cache_v.reshape(n_pool, page, N_HEADS, HEAD_DIM), page_table, sel_flat, dec_s)
    yp, a_s = _tail_back(x1p, hmid_p, p_prompt[l].reshape(mp, -1), w, tm=512, attend=attend)

    a_s = a_s[:, :dec_s].reshape(ms, ATTN_WIDTH)
    x1s, hmid_s, _ = _tail_front(xs, hs, a_s, gms, w, tm=ms)
    ys = _tail_back(x1s, hmid_s, p_sample[l].reshape(ms, -1), w, tm=ms)

    return (yp.reshape(batch, seq, d), ys.reshape(dec_b, dec_s, d),
            kp.reshape(1, batch, seq, N_HEADS, HEAD_DIM), vp.reshape(1, batch, seq, N_HEADS, HEAD_DIM),
            ks.reshape(1, dec_b, dec_s, N_HEADS, HEAD_DIM), vs.reshape(1, dec_b, dec_s, N_HEADS, HEAD_DIM),
            vns.reshape(1, dec_b, dec_s, GMLP_WIDTH))
```

```python
import functools

import jax
import jax.numpy as jnp
from jax import lax
from jax.experimental import pallas as pl
from jax.experimental.pallas import tpu as pltpu

F32 = jnp.float32
BF16 = jnp.bfloat16

N_HEADS = 8
HEAD_DIM = 128
ATTN_WIDTH = N_HEADS * HEAD_DIM
MOBA_BLOCK = 256
MOBA_TOPK = 3
ROPE_THETA = 500000.0
ROPE_DIM = HEAD_DIM // 4
ROPE_HALF = ROPE_DIM // 2
GMLP_GROUPS = 8
GMLP_CHUNK = 128
GMLP_WIDTH = 1024
GMLP_GROUP_DIM = GMLP_WIDTH // GMLP_GROUPS
NORM_EPS = 1e-6
NEG_INF = -1e30
TAKEN = -3e38
SAMPLE_ROWS = 16
ROW_CHUNK = 256
NORM_ROW_CHUNK = 128
BF16_SUBLANES = 16
MIXER_TILES_PER_WEIGHT = 2

VMEM_LIMIT_BYTES = 56 * 1024 * 1024


def _params(*semantics):
    return pltpu.CompilerParams(dimension_semantics=semantics, vmem_limit_bytes=VMEM_LIMIT_BYTES)


def _rms(x, g):
    return x * lax.rsqrt(jnp.mean(x * x, axis=-1, keepdims=True) + NORM_EPS) * g


def _dot(a, b):
    return jnp.dot(a, b, preferred_element_type=F32)


def _dot_nt(a, b):
    return lax.dot_general(a, b, (((1,), (1,)), ((), ())), preferred_element_type=F32)


def _mixer_kernel(x_ref, g_ref, w_ref, cos_ref, sa_ref, sb_ref, gvn_ref, wsp_ref, bsp_ref,
                  h_out, q_out, k_out, v_out, gm_out, *rest, chunk, rc, with_vn):
    if with_vn:
        vn_out, u_scr, h_scr = rest
    else:
        u_scr, h_scr = rest
    j = pl.program_id(1)
    tm = x_ref.shape[0]
    base = pl.program_id(2) * tm
    row_chunks = [slice(r, r + rc) for r in range(0, tm, rc)]

    def scr(rows, size=rc):
        return pl.ds(pl.multiple_of(base + rows.start, size), size)

    def rope_to(out_ref, z, rows):
        c, sa, sb = cos_ref[rows, :], sa_ref[rows, :], sb_ref[rows, :]
        for hd in range(N_HEADS):
            sl = slice(hd * HEAD_DIM, (hd + 1) * HEAD_DIM)
            zs = z[:, sl]
            out_ref[rows, sl] = (zs * c + pltpu.roll(zs, HEAD_DIM - ROPE_HALF, 1) * sa
                                 + pltpu.roll(zs, ROPE_HALF, 1) * sb)

    @pl.when(j == 0)
    def _():
        for rows in row_chunks:
            hb = _rms(x_ref[rows, :], g_ref[...]).astype(BF16)
            h_out[rows, :] = hb
            h_scr[scr(rows), :] = hb
            rope_to(q_out, _dot(hb, w_ref[...]), rows)

    @pl.when(j == 1)
    def _():
        for rows in row_chunks:
            rope_to(k_out, _dot(h_scr[scr(rows), :], w_ref[...]), rows)

    @pl.when(j == 2)
    def _():
        for rows in row_chunks:
            v_out[rows, :] = _dot(h_scr[scr(rows), :], w_ref[...])

    @pl.when(j == 3)
    def _():
        for rows in row_chunks:
            u_scr[scr(rows), :] = jax.nn.gelu(_dot(h_scr[scr(rows), :], w_ref[...]))

    @pl.when(j == 4)
    def _():
        for rows in row_chunks:
            vg = jax.nn.gelu(_dot(h_scr[scr(rows), :], w_ref[...]))
            xc = vg - jnp.mean(vg, axis=-1, keepdims=True)
            vn = xc * lax.rsqrt(jnp.mean(xc * xc, axis=-1, keepdims=True) + NORM_EPS) * gvn_ref[...]
            if with_vn:
                vn_out[rows, :] = vn
            vnb = vn.astype(BF16)
            for c in range(rc // chunk):
                sub = slice(c * chunk, (c + 1) * chunk)
                dst = slice(rows.start + c * chunk, rows.start + (c + 1) * chunk)
                for g in range(GMLP_GROUPS):
                    cols = slice(g * GMLP_GROUP_DIM, (g + 1) * GMLP_GROUP_DIM)
                    s = _dot(wsp_ref[g], vnb[sub, cols]) + bsp_ref[:, g:g + 1]
                    gm_out[dst, cols] = (u_scr[scr(dst, chunk), cols] * s).astype(BF16)


def _mixer(x, g_pre, w_in, cos_t, sa_t, sb_t, g_vn, wsp, bsp, *, tm, chunk, with_vn):
    m, d = x.shape
    n_seg = w_in.shape[1] // ATTN_WIDTH
    t_blocks = cos_t.shape[0] // tm
    n_row = m // tm
    sub = MIXER_TILES_PER_WEIGHT if n_row % MIXER_TILES_PER_WEIGHT == 0 else 1
    n_sup = n_row // sub
    tab_spec = pl.BlockSpec((tm, HEAD_DIM), lambda s, j, t: ((s * sub + t) % t_blocks, 0))
    const = lambda shape: pl.BlockSpec(shape, lambda s, j, t: (0,) * len(shape))
    x_spec = pl.BlockSpec((tm, d), lambda s, j, t: (jnp.where(j == 0, s * sub + t, s * sub + sub - 1), 0))

    def out_spec(w, seg):
        def idx(s, j, t):
            after = jnp.minimum((s + 1) * sub, n_row - 1)
            return (jnp.where(j < seg, s * sub, jnp.where(j == seg, s * sub + t, after)), 0)
        return pl.BlockSpec((tm, w), idx)

    out_shape = [jax.ShapeDtypeStruct((m, d), BF16),
                 jax.ShapeDtypeStruct((m, ATTN_WIDTH), F32),
                 jax.ShapeDtypeStruct((m, ATTN_WIDTH), F32),
                 jax.ShapeDtypeStruct((m, ATTN_WIDTH), F32),
                 jax.ShapeDtypeStruct((m, GMLP_WIDTH), BF16)]
    out_specs = [out_spec(d, 0), out_spec(ATTN_WIDTH, 0), out_spec(ATTN_WIDTH, 1), out_spec(ATTN_WIDTH, 2),
                 out_spec(GMLP_WIDTH, n_seg - 1)]
    if with_vn:
        out_shape.append(jax.ShapeDtypeStruct((m, GMLP_WIDTH), F32))
        out_specs.append(out_spec(GMLP_WIDTH, n_seg - 1))
    return pl.pallas_call(
        functools.partial(_mixer_kernel, chunk=chunk, rc=max(chunk, min(tm, ROW_CHUNK)), with_vn=with_vn),
        grid=(n_sup, n_seg, sub),
        in_specs=[x_spec,
                  const((1, d)),
                  pl.BlockSpec((d, ATTN_WIDTH), lambda s, j, t: (0, j)),
                  tab_spec, tab_spec, tab_spec,
                  const((1, GMLP_WIDTH)),
                  const((GMLP_GROUPS, chunk, chunk)),
                  const((chunk, GMLP_GROUPS))],
        out_specs=out_specs,
        out_shape=out_shape,
        scratch_shapes=[pltpu.VMEM((sub * tm, GMLP_WIDTH), F32), pltpu.VMEM((sub * tm, d), BF16)],
        compiler_params=_params("arbitrary", "arbitrary", "arbitrary"),
        name="mixer",
    )(x, g_pre, w_in, cos_t, sa_t, sb_t, g_vn, wsp, bsp)


def _moba_prompt_kernel(q_ref, k_ref, v_ref, *refs):
    n_cast = (len(refs) - 4) // 2
    o_ref = refs[n_cast]
    kb_scr, vt_scr, s_scr = refs[-3:]
    for src_ref, dst_ref in zip(refs[:n_cast], refs[n_cast + 1:2 * n_cast + 1]):
        dst_ref[...] = src_ref[...].astype(BF16)
    seq = q_ref.shape[0]
    nblk = seq // MOBA_BLOCK
    blk = MOBA_BLOCK
    scale = HEAD_DIM ** -0.5
    kb_scr[...] = k_ref[...].astype(BF16)
    vt_scr[...] = v_ref[...].T.astype(BF16)
    qt = q_ref[...].T
    kt = k_ref[...].T

    blk_id = lax.broadcasted_iota(jnp.int32, (nblk, seq), 0)
    q_blk = lax.broadcasted_iota(jnp.int32, (nblk, seq), 1) // blk
    gate = jnp.zeros((nblk, seq), F32)
    for n in range(nblk):
        kmean_n = jnp.mean(kt[:, n * blk:(n + 1) * blk], axis=1, keepdims=True)
        g_n = jnp.sum(qt * kmean_n, axis=0, keepdims=True)
        gate = jnp.where(blk_id == n, g_n, gate)
    rank = jnp.zeros((nblk, seq), jnp.int32)
    for m in range(nblk):
        g_m = gate[m:m + 1, :]
        beats = (m < q_blk) & ((g_m > gate) | ((g_m == gate) & (m < blk_id)))
        rank = rank + beats.astype(jnp.int32)
    keep = jnp.where((blk_id < q_blk) & (rank < MOBA_TOPK), 1.0, 0.0)

    key_i = lax.broadcasted_iota(jnp.int32, (blk, blk), 0)
    qry_i = lax.broadcasted_iota(jnp.int32, (blk, blk), 1)
    causal = key_i <= qry_i

    for j in range(nblk):
        cols = slice(j * blk, (j + 1) * blk)
        qb = q_ref[cols, :].astype(BF16)
        keep_j = keep[:, cols]
        m_run = None
        for n in range(j + 1):
            st = _dot_nt(kb_scr[n * blk:(n + 1) * blk, :], qb) * scale
            if n == j:
                st = jnp.where(causal, st, NEG_INF)
            else:
                st = jnp.where(keep_j[n:n + 1, :] > 0.5, st, NEG_INF)
            s_scr[n] = st
            m_n = jnp.max(st, axis=0, keepdims=True)
            m_run = m_n if m_run is None else jnp.maximum(m_run, m_n)
        l_run = jnp.zeros((1, blk), F32)
        acc = jnp.zeros((HEAD_DIM, blk), F32)
        for n in range(j + 1):
            p = jnp.exp(s_scr[n] - m_run)
            l_run = l_run + jnp.sum(p, axis=0, keepdims=True)
            acc = acc + _dot(vt_scr[:, n * blk:(n + 1) * blk], p.astype(BF16))
        o_ref[cols, :] = (acc / l_run).T.astype(BF16)


def _moba_prompt(q, k, v, *, batch, seq, casts=()):
    n_steps = batch * N_HEADS
    spec = pl.BlockSpec((seq, HEAD_DIM), lambda b, h: (b, h))

    def slab_spec(wt):
        rows = wt.shape[0] // n_steps
        assert rows * n_steps == wt.shape[0] and rows % BF16_SUBLANES == 0
        return pl.BlockSpec((rows, wt.shape[1]), lambda b, h: (b * N_HEADS + h, 0))

    slabs = [slab_spec(wt) for wt in casts]
    out = pl.pallas_call(
        _moba_prompt_kernel,
        grid=(batch, N_HEADS),
        in_specs=[spec, spec, spec] + slabs,
        out_specs=[spec] + slabs,
        out_shape=[jax.ShapeDtypeStruct(q.shape, BF16)] + [jax.ShapeDtypeStruct(wt.shape, BF16) for wt in casts],
        scratch_shapes=[pltpu.VMEM((seq, HEAD_DIM), BF16), pltpu.VMEM((HEAD_DIM, seq), BF16),
                        pltpu.VMEM((seq // MOBA_BLOCK, MOBA_BLOCK, MOBA_BLOCK), F32)],
        compiler_params=_params("parallel", "parallel"),
        name="moba_prompt",
    )(q, k, v, *casts)
    return out[0], tuple(out[1:])


def _scan_key_means(page_refs, km_ref, group):
    n_blk = len(page_refs) // 2
    page_rows = page_refs[0].shape[0] // N_HEADS

    def page_sum(ref):
        return jnp.sum(ref[...].reshape(page_rows, N_HEADS, HEAD_DIM), axis=0)

    for t in range(n_blk):
        tot = (page_sum(page_refs[2 * t]) + page_sum(page_refs[2 * t + 1])) * (1.0 / MOBA_BLOCK)
        for h in range(N_HEADS):
            km_ref[h, pl.ds(group * n_blk + t, 1), :] = tot[h:h + 1, :]


def _select_kernel(q_ref, km_ref, kn_ref, sel_ref, gate_scr, *, dec_s, n_past):
    dec_b = q_ref.shape[0]
    gate_scr[...] = jnp.full(gate_scr.shape, NEG_INF, F32)
    for b in range(dec_b):
        own_mean = jnp.sum(kn_ref[b], axis=0, keepdims=True) * (1.0 / MOBA_BLOCK)
        for h in range(N_HEADS):
            cols = slice(h * HEAD_DIM, (h + 1) * HEAD_DIM)
            km = km_ref[b, h]
            for i in range(dec_s):
                col = (b * N_HEADS + h) * dec_s + i
                qi = q_ref[b, i:i + 1, cols]
                gate_scr[0:n_past, col:col + 1] = jnp.sum(km * qi, axis=-1, keepdims=True)
                gate_scr[n_past:n_past + 1, col:col + 1] = jnp.sum(qi * own_mean[:, cols], axis=-1,
                                                                   keepdims=True)
    gate = gate_scr[...]
    blk = lax.broadcasted_iota(jnp.int32, gate.shape, 0)
    gate = jnp.where(blk < n_past, gate, NEG_INF)
    out_row = lax.broadcasted_iota(jnp.int32, sel_ref.shape, 0)
    sel = jnp.zeros(sel_ref.shape, jnp.int32)
    for t in range(MOBA_TOPK):
        best = jnp.max(gate, axis=0, keepdims=True)
        idx = jnp.min(jnp.where(gate == best, blk, gate.shape[0]), axis=0, keepdims=True)
        sel = jnp.where(out_row == t, idx, sel)
        gate = jnp.where(blk == idx, TAKEN, gate)
    sel_ref[...] = sel


def _select_blocks(q3, kmean, kn3, *, n_past):
    dec_b, dec_s, width = q3.shape
    cols = dec_b * N_HEADS * dec_s
    gate_rows = 8 * (-(-(n_past + 1) // 8))
    return pl.pallas_call(
        functools.partial(_select_kernel, dec_s=dec_s, n_past=n_past),
        out_shape=jax.ShapeDtypeStruct((8, cols), jnp.int32),
        scratch_shapes=[pltpu.VMEM((gate_rows, cols), F32)],
        compiler_params=pltpu.CompilerParams(vmem_limit_bytes=VMEM_LIMIT_BYTES),
        name="select_blocks",
    )(q3, kmean, kn3)


def _sample_fetch(step, n_steps, pt_ref, sel_ref, ck_hbm, cv_hbm, kbuf, vbuf, sem, *, n_slots):
    n_pages = 2 * n_slots

    def page_copies(step_idx, slot):
        bb, hh = step_idx // N_HEADS, step_idx % N_HEADS
        copies = []
        for c in range(n_pages):
            blk = sel_ref[step_idx * n_slots + c // 2]
            pg = pt_ref[bb, 2 * blk + c % 2]
            src = (pg, slice(None), hh, slice(None))
            copies.append(pltpu.make_async_copy(ck_hbm.at[src], kbuf.at[slot, c], sem.at[0, slot]))
            copies.append(pltpu.make_async_copy(cv_hbm.at[src], vbuf.at[slot, c], sem.at[1, slot]))
        return copies

    @pl.when(step == 0)
    def _():
        for cp in page_copies(step, 0):
            cp.start()

    @pl.when(step + 1 < n_steps)
    def _():
        for cp in page_copies(step + 1, (step + 1) % 2):
            cp.start()

    slot = step % 2
    for cp in page_copies(step, slot):
        cp.wait()
    return slot


def _sample_attend(slot, q_ref, kn_ref, vn_ref, o_ref, kbuf, vbuf, *, dec_s, n_slots):
    n_pages = 2 * n_slots
    page = kbuf.shape[2]
    k_refs = [kbuf.at[slot, c] for c in range(n_pages)]
    v_refs = [vbuf.at[slot, c] for c in range(n_pages)]
    scale = HEAD_DIM ** -0.5
    rows = q_ref.shape[0]
    qb = q_ref[...].astype(BF16)
    row = lax.broadcasted_iota(jnp.int32, (rows, page), 0)
    s_list = []
    m_run = None
    for c in range(n_pages):
        s = _dot_nt(qb, k_refs[c][...].astype(BF16)) * scale
        s = jnp.where(row == c // (2 * MOBA_TOPK), s, NEG_INF)
        s_list.append(s)
        m_c = jnp.max(s, axis=-1, keepdims=True)
        m_run = m_c if m_run is None else jnp.maximum(m_run, m_c)
    s_own = _dot_nt(qb, kn_ref[...].astype(BF16)) * scale
    r_o = lax.broadcasted_iota(jnp.int32, (rows, rows), 0)
    c_o = lax.broadcasted_iota(jnp.int32, (rows, rows), 1)
    s_own = jnp.where((c_o <= r_o) & (c_o < dec_s), s_own, NEG_INF)
    m_run = jnp.maximum(m_run, jnp.max(s_own, axis=-1, keepdims=True))
    p_own = jnp.exp(s_own - m_run)
    l_run = jnp.sum(p_own, axis=-1, keepdims=True)
    acc = _dot(p_own.astype(BF16), vn_ref[...].astype(BF16))
    for c in range(n_pages):
        p = jnp.exp(s_list[c] - m_run)
        l_run = l_run + jnp.sum(p, axis=-1, keepdims=True)
        acc = acc + _dot(p.astype(BF16), v_refs[c][...].astype(BF16))
    o_ref[...] = (acc / l_run).astype(BF16)


def _gated_mix_kernel(a_ref, b_ref, h_ref, wa_ref, wb_ref, wga_ref, wgb_ref, o_ref):
    h = h_ref[...]
    ga = jax.nn.sigmoid(_dot(h, wga_ref[...]))
    gb = jax.nn.sigmoid(_dot(h, wgb_ref[...]))
    a = _dot(a_ref[...], wa_ref[...])
    b = _dot(b_ref[...], wb_ref[...])
    o_ref[...] = (ga * a + gb * b).astype(BF16)


def _gated_mix(attn, gm, h, w_a, w_b, w_gate, *, tm, tn):
    m, d = h.shape
    n_col = d // tn
    return pl.pallas_call(
        _gated_mix_kernel,
        grid=(n_col, m // tm),
        in_specs=[pl.BlockSpec((tm, attn.shape[1]), lambda j, i: (i, 0)),
                  pl.BlockSpec((tm, gm.shape[1]), lambda j, i: (i, 0)),
                  pl.BlockSpec((tm, d), lambda j, i: (i, 0)),
                  pl.BlockSpec((w_a.shape[0], tn), lambda j, i: (0, j)),
                  pl.BlockSpec((w_b.shape[0], tn), lambda j, i: (0, j)),
                  pl.BlockSpec((d, tn), lambda j, i: (0, j)),
                  pl.BlockSpec((d, tn), lambda j, i: (0, j + n_col))],
        out_specs=pl.BlockSpec((tm, tn), lambda j, i: (i, j)),
        out_shape=jax.ShapeDtypeStruct((m, d), BF16),
        compiler_params=_params("parallel", "parallel"),
        name="gated_mix",
    )(attn, gm, h, w_a, w_b, w_gate, w_gate)


def _row_chunks(tm, rc):
    rc = min(tm, rc)
    return [slice(r, r + rc) for r in range(0, tm, rc)]


def _mix_out_kernel(mix_ref, x_ref, wo_ref, gpost_ref, gpre_ref, x1_ref, xn_ref):
    for rows in _row_chunks(x_ref.shape[0], NORM_ROW_CHUNK):
        mix = _dot(mix_ref[rows, :], wo_ref[...])
        x1 = x_ref[rows, :] + _rms(mix, gpost_ref[...])
        x1_ref[rows, :] = x1
        xn_ref[rows, :] = _rms(x1, gpre_ref[...]).astype(BF16)


def _mix_out(mixin, x, w_o, g_post, g_pre_ffn, *, tm):
    m, d = x.shape
    row = pl.BlockSpec((tm, d), lambda i: (i, 0))
    vec = pl.BlockSpec((1, d), lambda i: (0, 0))
    return pl.pallas_call(
        _mix_out_kernel,
        grid=(m // tm,),
        in_specs=[row, row, pl.BlockSpec((d, d), lambda i: (0, 0), pipeline_mode=pl.Buffered(1)), vec, vec],
        out_specs=[row, row],
        out_shape=[jax.ShapeDtypeStruct((m, d), F32), jax.ShapeDtypeStruct((m, d), BF16)],
        compiler_params=_params("parallel"),
        name="mix_out",
    )(mixin, x, w_o, g_post, g_pre_ffn)


def _ffn_in_kernel(x_ref, wa_ref, wg_ref, o_ref):
    x = x_ref[...]
    a = _dot(x, wa_ref[...])
    g = _dot(x, wg_ref[...])
    o_ref[...] = (jax.nn.silu(a) * g).astype(BF16)


def _ffn_in_scan_kernel(pt_ref, x_ref, wa_ref, wg_ref, *refs, groups):
    page_refs, o_ref, km_ref = refs[:-2], refs[-2], refs[-1]
    step = pl.program_id(0) * pl.num_programs(1) + pl.program_id(1)
    _scan_key_means(page_refs, km_ref, step % groups)
    _ffn_in_kernel(x_ref, wa_ref, wg_ref, o_ref)


def _ffn_in(xn, w_ffn_in, *, tm, tn, scan=None):
    m, d = xn.shape
    hidden = w_ffn_in.shape[1] // 2
    n_col, n_row = hidden // tn, m // tm
    in_specs = [pl.BlockSpec((tm, d), lambda j, i, *_: (i, 0)),
                pl.BlockSpec((d, tn), lambda j, i, *_: (0, j)),
                pl.BlockSpec((d, tn), lambda j, i, *_: (0, j + n_col))]
    out_spec = pl.BlockSpec((tm, tn), lambda j, i, *_: (i, j))
    out_shape = jax.ShapeDtypeStruct((m, hidden), BF16)
    if scan is None:
        return pl.pallas_call(
            _ffn_in_kernel, grid=(n_col, n_row), in_specs=in_specs, out_specs=out_spec, out_shape=out_shape,
            compiler_params=_params("parallel", "parallel"), name="ffn_in",
        )(xn, w_ffn_in, w_ffn_in)

    cache_k2, page_table = scan
    dec_b, n_pages = page_table.shape
    n_steps = n_col * n_row
    pps = dec_b * n_pages // n_steps
    groups = n_pages // pps
    assert pps * n_steps == dec_b * n_pages and groups * pps == n_pages and pps % 2 == 0
    n_past = n_pages // 2

    def page_spec(c):
        def idx(j, i, pt):
            step = j * n_row + i
            return (pt[step // groups, (step % groups) * pps + c], 0, 0)
        return pl.BlockSpec((None,) + cache_k2.shape[1:], idx)

    grid_spec = pltpu.PrefetchScalarGridSpec(
        num_scalar_prefetch=1,
        grid=(n_col, n_row),
        in_specs=in_specs + [page_spec(c) for c in range(pps)],
        out_specs=[out_spec,
                   pl.BlockSpec((None, N_HEADS, n_past, HEAD_DIM),
                                lambda j, i, pt: ((j * n_row + i) // groups, 0, 0, 0))],
    )
    return pl.pallas_call(
        functools.partial(_ffn_in_scan_kernel, groups=groups),
        grid_spec=grid_spec,
        out_shape=[out_shape, jax.ShapeDtypeStruct((dec_b, N_HEADS, n_past, HEAD_DIM), F32)],
        compiler_params=_params("arbitrary", "arbitrary"),
        name="ffn_in_scan",
    )(page_table, xn, w_ffn_in, w_ffn_in, *([cache_k2] * pps))


def _ffn_out_ple_kernel(hm_ref, w_ref, x1_ref, gpost_ref, gple_ref, p_ref, wpg_ref, wp_ref, y_ref, acc_scr,
                        side_work=lambda: None):
    k = pl.program_id(1)
    last = pl.num_programs(1) - 1

    @pl.when(k == 0)
    def _():
        side_work()
        acc_scr[...] = _dot(hm_ref[...], w_ref[...])

    @pl.when((k > 0) & (k < last))
    def _():
        side_work()
        acc_scr[...] += _dot(hm_ref[...], w_ref[...])

    @pl.when(k == last)
    def _():
        side_work()
        for rows in _row_chunks(x1_ref.shape[0], NORM_ROW_CHUNK):
            f = acc_scr[rows, :] + _dot(hm_ref[rows, :], w_ref[...])
            x2 = x1_ref[rows, :] + _rms(f, gpost_ref[...])
            xg = _rms(x2, gple_ref[...]).astype(BF16)
            gate = jax.nn.sigmoid(_dot(xg, wpg_ref[...]))
            y_ref[rows, :] = x2 + _dot(p_ref[rows, :].astype(BF16), wp_ref[...]) * gate


def _ffn_out_ple_attend_kernel(pt_ref, sel_ref, hm_ref, w_ref, x1_ref, gpost_ref, gple_ref, p_ref, wpg_ref,
                               wp_ref, q_ref, kn_ref, vn_ref, ck_hbm, cv_hbm, y_ref, a_ref,
                               acc_scr, kbuf, vbuf, sem, *, dec_s, n_slots):
    step = pl.program_id(0) * pl.num_programs(1) + pl.program_id(1)
    n_steps = pl.num_programs(0) * pl.num_programs(1)
    slot = _sample_fetch(step, n_steps, pt_ref, sel_ref, ck_hbm, cv_hbm, kbuf, vbuf, sem, n_slots=n_slots)
    attend = functools.partial(_sample_attend, slot, q_ref, kn_ref, vn_ref, a_ref, kbuf, vbuf,
                               dec_s=dec_s, n_slots=n_slots)
    _ffn_out_ple_kernel(hm_ref, w_ref, x1_ref, gpost_ref, gple_ref, p_ref, wpg_ref, wp_ref, y_ref, acc_scr,
                        side_work=attend)


def _ffn_out_ple(hmid, w_ffn_out, x1, g_post_ffn, g_ple, p, w_ple_gate, w_ple, *, tm, tk, attend=None):
    m, d = x1.shape
    hidden = hmid.shape[1]
    n_row, n_k = m // tm, hidden // tk
    assert n_k >= 2
    row = pl.BlockSpec((tm, d), lambda i, k, *_: (i, 0))
    vec = pl.BlockSpec((1, d), lambda i, k, *_: (0, 0))
    const = lambda shape: pl.BlockSpec(shape, lambda i, k, *_: (0, 0), pipeline_mode=pl.Buffered(1))
    in_specs = [pl.BlockSpec((tm, tk), lambda i, k, *_: (i, k)),
                pl.BlockSpec((tk, d), lambda i, k, *_: (k, 0)),
                row, vec, vec,
                pl.BlockSpec((tm, p.shape[1]), lambda i, k, *_: (i, 0)),
                const(w_ple_gate.shape), const(w_ple.shape)]
    args = (hmid, w_ffn_out, x1, g_post_ffn, g_ple, p, w_ple_gate, w_ple)
    y_shape = jax.ShapeDtypeStruct((m, d), F32)
    acc = pltpu.VMEM((tm, d), F32)
    if attend is None:
        return pl.pallas_call(
            _ffn_out_ple_kernel, grid=(n_row, n_k), in_specs=in_specs, out_specs=row, out_shape=y_shape,
            scratch_shapes=[acc], compiler_params=_params("parallel", "arbitrary"), name="ffn_out_ple",
        )(*args)

    q3, kn3, vn3, cache_k4, cache_v4, page_table, sel_flat, dec_s = attend
    dec_b, rows, _ = q3.shape
    page = cache_k4.shape[1]
    n_slots = dec_s * MOBA_TOPK
    assert n_row * n_k == dec_b * N_HEADS
    pair_spec = pl.BlockSpec((None, rows, HEAD_DIM),
                             lambda i, k, *_: ((i * n_k + k) // N_HEADS, 0, (i * n_k + k) % N_HEADS))
    hbm_spec = pl.BlockSpec(memory_space=pl.ANY)
    grid_spec = pltpu.PrefetchScalarGridSpec(
        num_scalar_prefetch=2,
        grid=(n_row, n_k),
        in_specs=in_specs + [pair_spec, pair_spec, pair_spec, hbm_spec, hbm_spec],
        out_specs=[row, pair_spec],
        scratch_shapes=[acc,
                        pltpu.VMEM((2, 2 * n_slots, page, HEAD_DIM), F32),
                        pltpu.VMEM((2, 2 * n_slots, page, HEAD_DIM), F32),
                        pltpu.SemaphoreType.DMA((2, 2))],
    )
    return pl.pallas_call(
        functools.partial(_ffn_out_ple_attend_kernel, dec_s=dec_s, n_slots=n_slots),
        grid_spec=grid_spec,
        out_shape=[y_shape, jax.ShapeDtypeStruct(q3.shape, BF16)],
        compiler_params=_params("arbitrary", "arbitrary"),
        name="ffn_out_ple_attend",
    )(page_table, sel_flat, *args, q3, kn3, vn3, cache_k4, cache_v4)


def _tail_front(x, h, attn, gm, w, *, tm, scan=None):
    m, d = x.shape
    hidden = w["w_ffn_out"].shape[0]
    mixin = _gated_mix(attn, gm, h, w["w_a_out"], w["w_b_out"], w["w_gate"], tm=min(m, 1024), tn=512)
    x1, xn = _mix_out(mixin, x, w["w_o"], w["g_post_mix"], w["g_pre_ffn"], tm=tm)
    if scan is None:
        return x1, _ffn_in(xn, w["w_ffn_in"], tm=tm, tn=hidden // 4), None
    hmid, kmean = _ffn_in(xn, w["w_ffn_in"], tm=tm, tn=hidden // 4, scan=scan)
    return x1, hmid, kmean


def _tail_back(x1, hmid, p, w, *, tm, attend=None):
    return _ffn_out_ple(hmid, w["w_ffn_out"], x1, w["g_post_ffn"], w["g_ple"], p, w["w_ple_gate"], w["w_ple"],
                        tm=tm, tk=w["w_ffn_out"].shape[0] // 4, attend=attend)


def _rope_tables(pos):
    freqs = jnp.power(jnp.float32(ROPE_THETA), -2.0 * jnp.arange(ROPE_HALF, dtype=F32) / ROPE_DIM)
    ang = pos.astype(F32)[:, None] * freqs[None, :]
    cos, sin = jnp.cos(ang), jnp.sin(ang)
    n = pos.shape[0]
    c = jnp.concatenate([cos, cos, jnp.ones((n, HEAD_DIM - ROPE_DIM), F32)], axis=1)
    sa = jnp.concatenate([-sin, jnp.zeros((n, HEAD_DIM - ROPE_HALF), F32)], axis=1)
    sb = jnp.concatenate([jnp.zeros((n, ROPE_HALF), F32), sin, jnp.zeros((n, HEAD_DIM - ROPE_DIM), F32)], axis=1)
    return c, sa, sb


def kernel(x_prompt, x_sample, cache_k, cache_v, page_table, p_prompt, p_sample, g_pre_mix, w_in, g_vnorm, w_spatial, b_spatial, w_a_out, w_b_out, w_gate, w_o, g_post_mix, g_pre_ffn, w_ffn_in, w_ffn_out, g_post_ffn, g_ple, w_ple_gate, w_ple):
    batch, seq, d = x_prompt.shape
    dec_b, dec_s, _ = x_sample.shape
    depth = w_in.shape[0]
    page = cache_k.shape[2]
    past_len = page_table.shape[1] * page
    n_past = past_len // MOBA_BLOCK
    assert depth == 1 and seq % MOBA_BLOCK == 0 and past_len % MOBA_BLOCK == 0 and MOBA_BLOCK == 2 * page
    assert dec_s <= GMLP_CHUNK and n_past >= MOBA_TOPK
    l = 0

    w = {
        "w_ple": w_ple[l].astype(BF16),
        "g_post_mix": g_post_mix[l][None], "g_pre_ffn": g_pre_ffn[l][None],
        "g_post_ffn": g_post_ffn[l][None], "g_ple": g_ple[l][None],
    }
    tail_weights = {"w_a_out": w_a_out[l], "w_b_out": w_b_out[l], "w_gate": w_gate[l], "w_o": w_o[l],
                    "w_ffn_in": w_ffn_in[l], "w_ffn_out": w_ffn_out[l], "w_ple_gate": w_ple_gate[l]}
    w_in_b = w_in[l].astype(BF16)
    g_pre = g_pre_mix[l][None]
    g_vn = g_vnorm[l][None]

    w_tril = jnp.tril(w_spatial[l])
    wsp_p = w_tril.astype(BF16)
    bsp_p = b_spatial[l].T
    eye_b = jnp.eye(dec_b, dtype=F32)
    wsp_s = jnp.einsum("ab,gts->gatbs", eye_b, w_tril[:, :dec_s, :dec_s]).reshape(
        GMLP_GROUPS, dec_b * dec_s, dec_b * dec_s).astype(BF16)
    bsp_s = jnp.tile(b_spatial[l][:, :dec_s].T, (dec_b, 1))

    mp = batch * seq
    xp = x_prompt.reshape(mp, d)
    cp, sap, sbp = _rope_tables(jnp.arange(seq, dtype=jnp.int32))
    hp, qp, kp, vp, gmp = _mixer(xp, g_pre, w_in_b, cp, sap, sbp, g_vn, wsp_p, bsp_p,
                                 tm=512, chunk=GMLP_CHUNK, with_vn=False)
    ms = dec_b * dec_s
    xs = x_sample.reshape(ms, d)
    pos_s = past_len + jnp.arange(dec_s, dtype=jnp.int32)
    cs, sas, sbs = (jnp.tile(t, (dec_b, 1)) for t in _rope_tables(pos_s))
    hs, qs, ks, vs, gms, vns = _mixer(xs, g_pre, w_in_b, cs, sas, sbs, g_vn, wsp_s, bsp_s,
                                      tm=ms, chunk=ms, with_vn=True)

    ap, cast = _moba_prompt(qp, kp, vp, batch=batch, seq=seq, casts=tuple(tail_weights.values()))
    w.update(zip(tail_weights.keys(), cast))
    n_pool = depth * cache_k.shape[1]
    cache_k2 = cache_k.reshape(n_pool, page * N_HEADS, HEAD_DIM)
    x1p, hmid_p, kmean = _tail_front(xp, hp, ap, gmp, w, tm=512, scan=(cache_k2, page_table))
    q3 = qs.reshape(dec_b, dec_s, ATTN_WIDTH)
    kn3 = ks.reshape(dec_b, dec_s, ATTN_WIDTH)
    vn3 = vs.reshape(dec_b, dec_s, ATTN_WIDTH)
    sel = _select_blocks(q3, kmean, kn3, n_past=n_past)
    sel_flat = sel[:MOBA_TOPK].T.reshape(-1)
    pad = lambda t: jnp.pad(t, ((0, 0), (0, SAMPLE_ROWS - dec_s), (0, 0)))
    attend = (pad(q3), pad(kn3), pad(vn3), cache_k.reshape(n_pool, page, N_HEADS, HEAD_DIM),
              cache_v.reshape(n_pool, page, N_HEADS, HEAD_DIM), page_table, sel_flat, dec_s)
    yp, a_s = _tail_back(x1p, hmid_p, p_prompt[l].reshape(mp, -1), w, tm=512, attend=attend)

    a_s = a_s[:, :dec_s].reshape(ms, ATTN_WIDTH)
    x1s, hmid_s, _ = _tail_front(xs, hs, a_s, gms, w, tm=ms)
    ys = _tail_back(x1s, hmid_s, p_sample[l].reshape(ms, -1), w, tm=ms)

    return (yp.reshape(batch, seq, d), ys.reshape(dec_b, dec_s, d),
            kp.reshape(1, batch, seq, N_HEADS, HEAD_DIM), vp.reshape(1, batch, seq, N_HEADS, HEAD_DIM),
            ks.reshape(1, dec_b, dec_s, N_HEADS, HEAD_DIM), vs.reshape(1, dec_b, dec_s, N_HEADS, HEAD_DIM),
            vns.reshape(1, dec_b, dec_s, GMLP_WIDTH))
```

```python
import functools

import jax
import jax.numpy as jnp
from jax import lax
from jax.experimental import pallas as pl
from jax.experimental.pallas import tpu as pltpu

F32 = jnp.float32
BF16 = jnp.bfloat16

N_HEADS = 8
HEAD_DIM = 128
ATTN_WIDTH = N_HEADS * HEAD_DIM
MOBA_BLOCK = 256
MOBA_TOPK = 3
ROPE_THETA = 500000.0
ROPE_DIM = HEAD_DIM // 4
ROPE_HALF = ROPE_DIM // 2
GMLP_GROUPS = 8
GMLP_CHUNK = 128
GMLP_WIDTH = 1024
GMLP_GROUP_DIM = GMLP_WIDTH // GMLP_GROUPS
NORM_EPS = 1e-6
NEG_INF = -1e30
LOG2_E = 1.4426950408889634
TAKEN = -3e38
SAMPLE_ROWS = 16
ROW_CHUNK = 256
NORM_ROW_CHUNK = 128
BF16_SUBLANES = 16
MIXER_TILES_PER_WEIGHT = 2

VMEM_LIMIT_BYTES = 56 * 1024 * 1024


def _params(*semantics):
    return pltpu.CompilerParams(dimension_semantics=semantics, vmem_limit_bytes=VMEM_LIMIT_BYTES)


def _rms(x, g):
    return x * lax.rsqrt(jnp.mean(x * x, axis=-1, keepdims=True) + NORM_EPS) * g


def _dot(a, b):
    return jnp.dot(a, b, preferred_element_type=F32)


def _dot_nt(a, b):
    return lax.dot_general(a, b, (((1,), (1,)), ((), ())), preferred_element_type=F32)


def _mixer_kernel(x_ref, g_ref, w_ref, cos_ref, sa_ref, sb_ref, gvn_ref, wsp_ref, bsp_ref,
                  h_out, q_out, k_out, v_out, gm_out, *rest, chunk, rc, with_vn):
    if with_vn:
        vn_out, u_scr, h_scr = rest
    else:
        u_scr, h_scr = rest
    j = pl.program_id(1)
    tm = x_ref.shape[0]
    base = pl.program_id(2) * tm
    row_chunks = [slice(r, r + rc) for r in range(0, tm, rc)]

    def scr(rows, size=rc):
        return pl.ds(pl.multiple_of(base + rows.start, size), size)

    def rope_to(out_ref, z, rows):
        c, sa, sb = cos_ref[rows, :], sa_ref[rows, :], sb_ref[rows, :]
        for hd in range(N_HEADS):
            sl = slice(hd * HEAD_DIM, (hd + 1) * HEAD_DIM)
            zs = z[:, sl]
            out_ref[rows, sl] = (zs * c + pltpu.roll(zs, HEAD_DIM - ROPE_HALF, 1) * sa
                                 + pltpu.roll(zs, ROPE_HALF, 1) * sb)

    @pl.when(j == 0)
    def _():
        for rows in row_chunks:
            hb = _rms(x_ref[rows, :], g_ref[...]).astype(BF16)
            h_out[rows, :] = hb
            h_scr[scr(rows), :] = hb
            rope_to(q_out, _dot(hb, w_ref[...]), rows)

    @pl.when(j == 1)
    def _():
        for rows in row_chunks:
            rope_to(k_out, _dot(h_scr[scr(rows), :], w_ref[...]), rows)

    @pl.when(j == 2)
    def _():
        for rows in row_chunks:
            v_out[rows, :] = _dot(h_scr[scr(rows), :], w_ref[...])

    @pl.when(j == 3)
    def _():
        for rows in row_chunks:
            u_scr[scr(rows), :] = jax.nn.gelu(_dot(h_scr[scr(rows), :], w_ref[...]))

    @pl.when(j == 4)
    def _():
        for rows in row_chunks:
            vg = jax.nn.gelu(_dot(h_scr[scr(rows), :], w_ref[...]))
            xc = vg - jnp.mean(vg, axis=-1, keepdims=True)
            vn = xc * lax.rsqrt(jnp.mean(xc * xc, axis=-1, keepdims=True) + NORM_EPS) * gvn_ref[...]
            if with_vn:
                vn_out[rows, :] = vn
            vnb = vn.astype(BF16)
            for c in range(rc // chunk):
                sub = slice(c * chunk, (c + 1) * chunk)
                dst = slice(rows.start + c * chunk, rows.start + (c + 1) * chunk)
                for g in range(GMLP_GROUPS):
                    cols = slice(g * GMLP_GROUP_DIM, (g + 1) * GMLP_GROUP_DIM)
                    s = _dot(wsp_ref[g], vnb[sub, cols]) + bsp_ref[:, g:g + 1]
                    gm_out[dst, cols] = (u_scr[scr(dst, chunk), cols] * s).astype(BF16)


def _mixer(x, g_pre, w_in, cos_t, sa_t, sb_t, g_vn, wsp, bsp, *, tm, chunk, with_vn):
    m, d = x.shape
    n_seg = w_in.shape[1] // ATTN_WIDTH
    t_blocks = cos_t.shape[0] // tm
    n_row = m // tm
    sub = MIXER_TILES_PER_WEIGHT if n_row % MIXER_TILES_PER_WEIGHT == 0 else 1
    n_sup = n_row // sub
    tab_spec = pl.BlockSpec((tm, HEAD_DIM), lambda s, j, t: ((s * sub + t) % t_blocks, 0))
    const = lambda shape: pl.BlockSpec(shape, lambda s, j, t: (0,) * len(shape))
    x_spec = pl.BlockSpec((tm, d), lambda s, j, t: (jnp.where(j == 0, s * sub + t, s * sub + sub - 1), 0))

    def out_spec(w, seg):
        def idx(s, j, t):
            after = jnp.minimum((s + 1) * sub, n_row - 1)
            return (jnp.where(j < seg, s * sub, jnp.where(j == seg, s * sub + t, after)), 0)
        return pl.BlockSpec((tm, w), idx)

    out_shape = [jax.ShapeDtypeStruct((m, d), BF16),
                 jax.ShapeDtypeStruct((m, ATTN_WIDTH), F32),
                 jax.ShapeDtypeStruct((m, ATTN_WIDTH), F32),
                 jax.ShapeDtypeStruct((m, ATTN_WIDTH), F32),
                 jax.ShapeDtypeStruct((m, GMLP_WIDTH), BF16)]
    out_specs = [out_spec(d, 0), out_spec(ATTN_WIDTH, 0), out_spec(ATTN_WIDTH, 1), out_spec(ATTN_WIDTH, 2),
                 out_spec(GMLP_WIDTH, n_seg - 1)]
    if with_vn:
        out_shape.append(jax.ShapeDtypeStruct((m, GMLP_WIDTH), F32))
        out_specs.append(out_spec(GMLP_WIDTH, n_seg - 1))
    return pl.pallas_call(
        functools.partial(_mixer_kernel, chunk=chunk, rc=max(chunk, min(tm, ROW_CHUNK)), with_vn=with_vn),
        grid=(n_sup, n_seg, sub),
        in_specs=[x_spec,
                  const((1, d)),
                  pl.BlockSpec((d, ATTN_WIDTH), lambda s, j, t: (0, j)),
                  tab_spec, tab_spec, tab_spec,
                  const((1, GMLP_WIDTH)),
                  const((GMLP_GROUPS, chunk, chunk)),
                  const((chunk, GMLP_GROUPS))],
        out_specs=out_specs,
        out_shape=out_shape,
        scratch_shapes=[pltpu.VMEM((sub * tm, GMLP_WIDTH), F32), pltpu.VMEM((sub * tm, d), BF16)],
        compiler_params=_params("arbitrary", "arbitrary", "arbitrary"),
        name="mixer",
    )(x, g_pre, w_in, cos_t, sa_t, sb_t, g_vn, wsp, bsp)


def _moba_prompt_kernel(q_ref, k_ref, v_ref, *refs):
    n_cast = (len(refs) - 4) // 2
    o_ref = refs[n_cast]
    kb_scr, vt_scr, s_scr = refs[-3:]
    for src_ref, dst_ref in zip(refs[:n_cast], refs[n_cast + 1:2 * n_cast + 1]):
        dst_ref[...] = src_ref[...].astype(BF16)
    seq = q_ref.shape[0]
    nblk = seq // MOBA_BLOCK
    blk = MOBA_BLOCK
    q_scale = HEAD_DIM ** -0.5 * LOG2_E
    kb_scr[...] = k_ref[...].astype(BF16)
    vt_scr[:HEAD_DIM, :] = v_ref[...].T.astype(BF16)
    vt_scr[HEAD_DIM:, :] = jnp.ones((BF16_SUBLANES, seq), BF16)
    qt = q_ref[...].T
    kt = k_ref[...].T

    blk_id = lax.broadcasted_iota(jnp.int32, (nblk, seq), 0)
    q_blk = lax.broadcasted_iota(jnp.int32, (nblk, seq), 1) // blk
    gate = jnp.zeros((nblk, seq), F32)
    for n in range(nblk - 1):
        kmean_n = jnp.mean(kt[:, n * blk:(n + 1) * blk], axis=1, keepdims=True)
        g_past = jnp.sum(qt[:, (n + 1) * blk:] * kmean_n, axis=0, keepdims=True)
        g_n = jnp.concatenate([jnp.zeros((1, (n + 1) * blk), F32), g_past], axis=1)
        gate = jnp.where(blk_id == n, g_n, gate)
    rank = jnp.zeros((nblk, seq), jnp.int32)
    for m in range(nblk):
        g_m = gate[m:m + 1, :]
        beats = (m < q_blk) & ((g_m > gate) | ((g_m == gate) & (m < blk_id)))
        rank = rank + beats.astype(jnp.int32)
    keep = jnp.where((blk_id < q_blk) & (rank < MOBA_TOPK), 1.0, 0.0)

    key_i = lax.broadcasted_iota(jnp.int32, (blk, blk), 0)
    qry_i = lax.broadcasted_iota(jnp.int32, (blk, blk), 1)
    causal = key_i <= qry_i

    for j in range(nblk):
        cols = slice(j * blk, (j + 1) * blk)
        qb = (q_ref[cols, :] * q_scale).astype(BF16)
        keep_j = keep[:, cols]
        m_run = None
        for n in range(j + 1):
            st = _dot_nt(kb_scr[n * blk:(n + 1) * blk, :], qb)
            if n == j:
                st = jnp.where(causal, st, NEG_INF)
            else:
                st = jnp.where(keep_j[n:n + 1, :] > 0.5, st, NEG_INF)
            s_scr[n] = st
            m_n = jnp.max(st, axis=0, keepdims=True)
            m_run = m_n if m_run is None else jnp.maximum(m_run, m_n)
        acc = jnp.zeros((HEAD_DIM + BF16_SUBLANES, blk), F32)
        for n in range(j + 1):
            p = jnp.exp2(s_scr[n] - m_run)
            acc = acc + _dot(vt_scr[:, n * blk:(n + 1) * blk], p.astype(BF16))
        o_ref[cols, :] = (acc[:HEAD_DIM, :] / acc[HEAD_DIM:HEAD_DIM + 1, :]).T.astype(BF16)


def _moba_prompt(q, k, v, *, batch, seq, casts=()):
    n_steps = batch * N_HEADS
    spec = pl.BlockSpec((seq, HEAD_DIM), lambda b, h: (b, h))

    def slab_spec(wt):
        rows = wt.shape[0] // n_steps
        assert rows * n_steps == wt.shape[0] and rows % BF16_SUBLANES == 0
        return pl.BlockSpec((rows, wt.shape[1]), lambda b, h: (b * N_HEADS + h, 0))

    slabs = [slab_spec(wt) for wt in casts]
    out = pl.pallas_call(
        _moba_prompt_kernel,
        grid=(batch, N_HEADS),
        in_specs=[spec, spec, spec] + slabs,
        out_specs=[spec] + slabs,
        out_shape=[jax.ShapeDtypeStruct(q.shape, BF16)] + [jax.ShapeDtypeStruct(wt.shape, BF16) for wt in casts],
        scratch_shapes=[pltpu.VMEM((seq, HEAD_DIM), BF16), pltpu.VMEM((HEAD_DIM + BF16_SUBLANES, seq), BF16),
                        pltpu.VMEM((seq // MOBA_BLOCK, MOBA_BLOCK, MOBA_BLOCK), F32)],
        compiler_params=_params("parallel", "parallel"),
        name="moba_prompt",
    )(q, k, v, *casts)
    return out[0], tuple(out[1:])


def _scan_key_means(page_refs, km_ref, group):
    n_blk = len(page_refs) // 2
    page_rows = page_refs[0].shape[0] // N_HEADS

    def page_sum(ref):
        return jnp.sum(ref[...].reshape(page_rows, N_HEADS, HEAD_DIM), axis=0)

    for t in range(n_blk):
        tot = (page_sum(page_refs[2 * t]) + page_sum(page_refs[2 * t + 1])) * (1.0 / MOBA_BLOCK)
        for h in range(N_HEADS):
            km_ref[h, pl.ds(group * n_blk + t, 1), :] = tot[h:h + 1, :]


def _select_kernel(q_ref, km_ref, kn_ref, sel_ref, gate_scr, *, dec_s, n_past):
    dec_b = q_ref.shape[0]
    gate_scr[...] = jnp.full(gate_scr.shape, NEG_INF, F32)
    for b in range(dec_b):
        own_mean = jnp.sum(kn_ref[b], axis=0, keepdims=True) * (1.0 / MOBA_BLOCK)
        for h in range(N_HEADS):
            cols = slice(h * HEAD_DIM, (h + 1) * HEAD_DIM)
            km = km_ref[b, h]
            for i in range(dec_s):
                col = (b * N_HEADS + h) * dec_s + i
                qi = q_ref[b, i:i + 1, cols]
                gate_scr[0:n_past, col:col + 1] = jnp.sum(km * qi, axis=-1, keepdims=True)
                gate_scr[n_past:n_past + 1, col:col + 1] = jnp.sum(qi * own_mean[:, cols], axis=-1,
                                                                   keepdims=True)
    gate = gate_scr[...]
    blk = lax.broadcasted_iota(jnp.int32, gate.shape, 0)
    gate = jnp.where(blk < n_past, gate, NEG_INF)
    out_row = lax.broadcasted_iota(jnp.int32, sel_ref.shape, 0)
    sel = jnp.zeros(sel_ref.shape, jnp.int32)
    for t in range(MOBA_TOPK):
        best = jnp.max(gate, axis=0, keepdims=True)
        idx = jnp.min(jnp.where(gate == best, blk, gate.shape[0]), axis=0, keepdims=True)
        sel = jnp.where(out_row == t, idx, sel)
        gate = jnp.where(blk == idx, TAKEN, gate)
    sel_ref[...] = sel


def _select_blocks(q3, kmean, kn3, *, n_past):
    dec_b, dec_s, width = q3.shape
    cols = dec_b * N_HEADS * dec_s
    gate_rows = 8 * (-(-(n_past + 1) // 8))
    return pl.pallas_call(
        functools.partial(_select_kernel, dec_s=dec_s, n_past=n_past),
        out_shape=jax.ShapeDtypeStruct((8, cols), jnp.int32),
        scratch_shapes=[pltpu.VMEM((gate_rows, cols), F32)],
        compiler_params=pltpu.CompilerParams(vmem_limit_bytes=VMEM_LIMIT_BYTES),
        name="select_blocks",
    )(q3, kmean, kn3)


def _sample_fetch(step, n_steps, pt_ref, sel_ref, ck_hbm, cv_hbm, kbuf, vbuf, sem, *, n_slots):
    n_pages = 2 * n_slots

    def page_copies(step_idx, slot):
        bb, hh = step_idx // N_HEADS, step_idx % N_HEADS
        copies = []
        for c in range(n_pages):
            blk = sel_ref[step_idx * n_slots + c // 2]
            pg = pt_ref[bb, 2 * blk + c % 2]
            src = (pg, slice(None), hh, slice(None))
            copies.append(pltpu.make_async_copy(ck_hbm.at[src], kbuf.at[slot, c], sem.at[0, slot]))
            copies.append(pltpu.make_async_copy(cv_hbm.at[src], vbuf.at[slot, c], sem.at[1, slot]))
        return copies

    @pl.when(step == 0)
    def _():
        for cp in page_copies(step, 0):
            cp.start()

    @pl.when(step + 1 < n_steps)
    def _():
        for cp in page_copies(step + 1, (step + 1) % 2):
            cp.start()

    slot = step % 2
    for cp in page_copies(step, slot):
        cp.wait()
    return slot


def _sample_attend(slot, q_ref, kn_ref, vn_ref, o_ref, kbuf, vbuf, *, dec_s, n_slots):
    n_pages = 2 * n_slots
    page = kbuf.shape[2]
    k_refs = [kbuf.at[slot, c] for c in range(n_pages)]
    v_refs = [vbuf.at[slot, c] for c in range(n_pages)]
    scale = HEAD_DIM ** -0.5
    rows = q_ref.shape[0]
    qb = q_ref[...].astype(BF16)
    row = lax.broadcasted_iota(jnp.int32, (rows, page), 0)
    s_list = []
    m_run = None
    for c in range(n_pages):
        s = _dot_nt(qb, k_refs[c][...].astype(BF16)) * scale
        s = jnp.where(row == c // (2 * MOBA_TOPK), s, NEG_INF)
        s_list.append(s)
        m_c = jnp.max(s, axis=-1, keepdims=True)
        m_run = m_c if m_run is None else jnp.maximum(m_run, m_c)
    s_own = _dot_nt(qb, kn_ref[...].astype(BF16)) * scale
    r_o = lax.broadcasted_iota(jnp.int32, (rows, rows), 0)
    c_o = lax.broadcasted_iota(jnp.int32, (rows, rows), 1)
    s_own = jnp.where((c_o <= r_o) & (c_o < dec_s), s_own, NEG_INF)
    m_run = jnp.maximum(m_run, jnp.max(s_own, axis=-1, keepdims=True))
    p_own = jnp.exp(s_own - m_run)
    l_run = jnp.sum(p_own, axis=-1, keepdims=True)
    acc = _dot(p_own.astype(BF16), vn_ref[...].astype(BF16))
    for c in range(n_pages):
        p = jnp.exp(s_list[c] - m_run)
        l_run = l_run + jnp.sum(p, axis=-1, keepdims=True)
        acc = acc + _dot(p.astype(BF16), v_refs[c][...].astype(BF16))
    o_ref[...] = (acc / l_run).astype(BF16)


def _gated_mix_kernel(a_ref, b_ref, h_ref, wa_ref, wb_ref, wga_ref, wgb_ref, o_ref):
    h = h_ref[...]
    ga = jax.nn.sigmoid(_dot(h, wga_ref[...]))
    gb = jax.nn.sigmoid(_dot(h, wgb_ref[...]))
    a = _dot(a_ref[...], wa_ref[...])
    b = _dot(b_ref[...], wb_ref[...])
    o_ref[...] = (ga * a + gb * b).astype(BF16)


def _gated_mix(attn, gm, h, w_a, w_b, w_gate, *, tm, tn):
    m, d = h.shape
    n_col = d // tn
    return pl.pallas_call(
        _gated_mix_kernel,
        grid=(n_col, m // tm),
        in_specs=[pl.BlockSpec((tm, attn.shape[1]), lambda j, i: (i, 0)),
                  pl.BlockSpec((tm, gm.shape[1]), lambda j, i: (i, 0)),
                  pl.BlockSpec((tm, d), lambda j, i: (i, 0)),
                  pl.BlockSpec((w_a.shape[0], tn), lambda j, i: (0, j)),
                  pl.BlockSpec((w_b.shape[0], tn), lambda j, i: (0, j)),
                  pl.BlockSpec((d, tn), lambda j, i: (0, j)),
                  pl.BlockSpec((d, tn), lambda j, i: (0, j + n_col))],
        out_specs=pl.BlockSpec((tm, tn), lambda j, i: (i, j)),
        out_shape=jax.ShapeDtypeStruct((m, d), BF16),
        compiler_params=_params("parallel", "parallel"),
        name="gated_mix",
    )(attn, gm, h, w_a, w_b, w_gate, w_gate)


def _row_chunks(tm, rc):
    rc = min(tm, rc)
    return [slice(r, r + rc) for r in range(0, tm, rc)]


def _mix_out_kernel(mix_ref, x_ref, wo_ref, gpost_ref, gpre_ref, x1_ref, xn_ref):
    for rows in _row_chunks(x_ref.shape[0], NORM_ROW_CHUNK):
        mix = _dot(mix_ref[rows, :], wo_ref[...])
        x1 = x_ref[rows, :] + _rms(mix, gpost_ref[...])
        x1_ref[rows, :] = x1
        xn_ref[rows, :] = _rms(x1, gpre_ref[...]).astype(BF16)


def _mix_out(mixin, x, w_o, g_post, g_pre_ffn, *, tm):
    m, d = x.shape
    row = pl.BlockSpec((tm, d), lambda i: (i, 0))
    vec = pl.BlockSpec((1, d), lambda i: (0, 0))
    return pl.pallas_call(
        _mix_out_kernel,
        grid=(m // tm,),
        in_specs=[row, row, pl.BlockSpec((d, d), lambda i: (0, 0), pipeline_mode=pl.Buffered(1)), vec, vec],
        out_specs=[row, row],
        out_shape=[jax.ShapeDtypeStruct((m, d), F32), jax.ShapeDtypeStruct((m, d), BF16)],
        compiler_params=_params("parallel"),
        name="mix_out",
    )(mixin, x, w_o, g_post, g_pre_ffn)


def _ffn_in_kernel(x_ref, wa_ref, wg_ref, o_ref):
    x = x_ref[...]
    a = _dot(x, wa_ref[...])
    g = _dot(x, wg_ref[...])
    o_ref[...] = (jax.nn.silu(a) * g).astype(BF16)


def _ffn_in_scan_kernel(pt_ref, x_ref, wa_ref, wg_ref, *refs, groups):
    page_refs, o_ref, km_ref = refs[:-2], refs[-2], refs[-1]
    step = pl.program_id(0) * pl.num_programs(1) + pl.program_id(1)
    _scan_key_means(page_refs, km_ref, step % groups)
    _ffn_in_kernel(x_ref, wa_ref, wg_ref, o_ref)


def _ffn_in(xn, w_ffn_in, *, tm, tn, scan=None):
    m, d = xn.shape
    hidden = w_ffn_in.shape[1] // 2
    n_col, n_row = hidden // tn, m // tm
    in_specs = [pl.BlockSpec((tm, d), lambda j, i, *_: (i, 0)),
                pl.BlockSpec((d, tn), lambda j, i, *_: (0, j)),
                pl.BlockSpec((d, tn), lambda j, i, *_: (0, j + n_col))]
    out_spec = pl.BlockSpec((tm, tn), lambda j, i, *_: (i, j))
    out_shape = jax.ShapeDtypeStruct((m, hidden), BF16)
    if scan is None:
        return pl.pallas_call(
            _ffn_in_kernel, grid=(n_col, n_row), in_specs=in_specs, out_specs=out_spec, out_shape=out_shape,
            compiler_params=_params("parallel", "parallel"), name="ffn_in",
        )(xn, w_ffn_in, w_ffn_in)

    cache_k2, page_table = scan
    dec_b, n_pages = page_table.shape
    n_steps = n_col * n_row
    pps = dec_b * n_pages // n_steps
    groups = n_pages // pps
    assert pps * n_steps == dec_b * n_pages and groups * pps == n_pages and pps % 2 == 0
    n_past = n_pages // 2

    def page_spec(c):
        def idx(j, i, pt):
            step = j * n_row + i
            return (pt[step // groups, (step % groups) * pps + c], 0, 0)
        return pl.BlockSpec((None,) + cache_k2.shape[1:], idx)

    grid_spec = pltpu.PrefetchScalarGridSpec(
        num_scalar_prefetch=1,
        grid=(n_col, n_row),
        in_specs=in_specs + [page_spec(c) for c in range(pps)],
        out_specs=[out_spec,
                   pl.BlockSpec((None, N_HEADS, n_past, HEAD_DIM),
                                lambda j, i, pt: ((j * n_row + i) // groups, 0, 0, 0))],
    )
    return pl.pallas_call(
        functools.partial(_ffn_in_scan_kernel, groups=groups),
        grid_spec=grid_spec,
        out_shape=[out_shape, jax.ShapeDtypeStruct((dec_b, N_HEADS, n_past, HEAD_DIM), F32)],
        compiler_params=_params("arbitrary", "arbitrary"),
        name="ffn_in_scan",
    )(page_table, xn, w_ffn_in, w_ffn_in, *([cache_k2] * pps))


def _ffn_out_ple_kernel(hm_ref, w_ref, x1_ref, gpost_ref, gple_ref, p_ref, wpg_ref, wp_ref, y_ref, acc_scr,
                        side_work=lambda: None):
    k = pl.program_id(1)
    last = pl.num_programs(1) - 1

    @pl.when(k == 0)
    def _():
        side_work()
        acc_scr[...] = _dot(hm_ref[...], w_ref[...])

    @pl.when((k > 0) & (k < last))
    def _():
        side_work()
        acc_scr[...] += _dot(hm_ref[...], w_ref[...])

    @pl.when(k == last)
    def _():
        side_work()
        for rows in _row_chunks(x1_ref.shape[0], NORM_ROW_CHUNK):
            f = acc_scr[rows, :] + _dot(hm_ref[rows, :], w_ref[...])
            x2 = x1_ref[rows, :] + _rms(f, gpost_ref[...])
            xg = _rms(x2, gple_ref[...]).astype(BF16)
            gate = jax.nn.sigmoid(_dot(xg, wpg_ref[...]))
            y_ref[rows, :] = x2 + _dot(p_ref[rows, :].astype(BF16), wp_ref[...]) * gate


def _ffn_out_ple_attend_kernel(pt_ref, sel_ref, hm_ref, w_ref, x1_ref, gpost_ref, gple_ref, p_ref, wpg_ref,
                               wp_ref, q_ref, kn_ref, vn_ref, ck_hbm, cv_hbm, y_ref, a_ref,
                               acc_scr, kbuf, vbuf, sem, *, dec_s, n_slots):
    step = pl.program_id(0) * pl.num_programs(1) + pl.program_id(1)
    n_steps = pl.num_programs(0) * pl.num_programs(1)
    slot = _sample_fetch(step, n_steps, pt_ref, sel_ref, ck_hbm, cv_hbm, kbuf, vbuf, sem, n_slots=n_slots)
    attend = functools.partial(_sample_attend, slot, q_ref, kn_ref, vn_ref, a_ref, kbuf, vbuf,
                               dec_s=dec_s, n_slots=n_slots)
    _ffn_out_ple_kernel(hm_ref, w_ref, x1_ref, gpost_ref, gple_ref, p_ref, wpg_ref, wp_ref, y_ref, acc_scr,
                        side_work=attend)


def _ffn_out_ple(hmid, w_ffn_out, x1, g_post_ffn, g_ple, p, w_ple_gate, w_ple, *, tm, tk, attend=None):
    m, d = x1.shape
    hidden = hmid.shape[1]
    n_row, n_k = m // tm, hidden // tk
    assert n_k >= 2
    row = pl.BlockSpec((tm, d), lambda i, k, *_: (i, 0))
    vec = pl.BlockSpec((1, d), lambda i, k, *_: (0, 0))
    const = lambda shape: pl.BlockSpec(shape, lambda i, k, *_: (0, 0), pipeline_mode=pl.Buffered(1))
    in_specs = [pl.BlockSpec((tm, tk), lambda i, k, *_: (i, k)),
                pl.BlockSpec((tk, d), lambda i, k, *_: (k, 0)),
                row, vec, vec,
                pl.BlockSpec((tm, p.shape[1]), lambda i, k, *_: (i, 0)),
                const(w_ple_gate.shape), const(w_ple.shape)]
    args = (hmid, w_ffn_out, x1, g_post_ffn, g_ple, p, w_ple_gate, w_ple)
    y_shape = jax.ShapeDtypeStruct((m, d), F32)
    acc = pltpu.VMEM((tm, d), F32)
    if attend is None:
        return pl.pallas_call(
            _ffn_out_ple_kernel, grid=(n_row, n_k), in_specs=in_specs, out_specs=row, out_shape=y_shape,
            scratch_shapes=[acc], compiler_params=_params("parallel", "arbitrary"), name="ffn_out_ple",
        )(*args)

    q3, kn3, vn3, cache_k4, cache_v4, page_table, sel_flat, dec_s = attend
    dec_b, rows, _ = q3.shape
    page = cache_k4.shape[1]
    n_slots = dec_s * MOBA_TOPK
    assert n_row * n_k == dec_b * N_HEADS
    pair_spec = pl.BlockSpec((None, rows, HEAD_DIM),
                             lambda i, k, *_: ((i * n_k + k) // N_HEADS, 0, (i * n_k + k) % N_HEADS))
    hbm_spec = pl.BlockSpec(memory_space=pl.ANY)
    grid_spec = pltpu.PrefetchScalarGridSpec(
        num_scalar_prefetch=2,
        grid=(n_row, n_k),
        in_specs=in_specs + [pair_spec, pair_spec, pair_spec, hbm_spec, hbm_spec],
        out_specs=[row, pair_spec],
        scratch_shapes=[acc,
                        pltpu.VMEM((2, 2 * n_slots, page, HEAD_DIM), F32),
                        pltpu.VMEM((2, 2 * n_slots, page, HEAD_DIM), F32),
                        pltpu.SemaphoreType.DMA((2, 2))],
    )
    return pl.pallas_call(
        functools.partial(_ffn_out_ple_attend_kernel, dec_s=dec_s, n_slots=n_slots),
        grid_spec=grid_spec,
        out_shape=[y_shape, jax.ShapeDtypeStruct(q3.shape, BF16)],
        compiler_params=_params("arbitrary", "arbitrary"),
        name="ffn_out_ple_attend",
    )(page_table, sel_flat, *args, q3, kn3, vn3, cache_k4, cache_v4)


def _tail_front(x, h, attn, gm, w, *, tm, scan=None):
    m, d = x.shape
    hidden = w["w_ffn_out"].shape[0]
    mixin = _gated_mix(attn, gm, h, w["w_a_out"], w["w_b_out"], w["w_gate"], tm=min(m, 1024), tn=512)
    x1, xn = _mix_out(mixin, x, w["w_o"], w["g_post_mix"], w["g_pre_ffn"], tm=tm)
    if scan is None:
        return x1, _ffn_in(xn, w["w_ffn_in"], tm=tm, tn=hidden // 4), None
    hmid, kmean = _ffn_in(xn, w["w_ffn_in"], tm=tm, tn=hidden // 4, scan=scan)
    return x1, hmid, kmean


def _tail_back(x1, hmid, p, w, *, tm, attend=None):
    return _ffn_out_ple(hmid, w["w_ffn_out"], x1, w["g_post_ffn"], w["g_ple"], p, w["w_ple_gate"], w["w_ple"],
                        tm=tm, tk=w["w_ffn_out"].shape[0] // 4, attend=attend)


def _rope_tables(pos):
    freqs = jnp.power(jnp.float32(ROPE_THETA), -2.0 * jnp.arange(ROPE_HALF, dtype=F32) / ROPE_DIM)
    ang = pos.astype(F32)[:, None] * freqs[None, :]
    cos, sin = jnp.cos(ang), jnp.sin(ang)
    n = pos.shape[0]
    c = jnp.concatenate([cos, cos, jnp.ones((n, HEAD_DIM - ROPE_DIM), F32)], axis=1)
    sa = jnp.concatenate([-sin, jnp.zeros((n, HEAD_DIM - ROPE_HALF), F32)], axis=1)
    sb = jnp.concatenate([jnp.zeros((n, ROPE_HALF), F32), sin, jnp.zeros((n, HEAD_DIM - ROPE_DIM), F32)], axis=1)
    return c, sa, sb


def kernel(x_prompt, x_sample, cache_k, cache_v, page_table, p_prompt, p_sample, g_pre_mix, w_in, g_vnorm, w_spatial, b_spatial, w_a_out, w_b_out, w_gate, w_o, g_post_mix, g_pre_ffn, w_ffn_in, w_ffn_out, g_post_ffn, g_ple, w_ple_gate, w_ple):
    batch, seq, d = x_prompt.shape
    dec_b, dec_s, _ = x_sample.shape
    depth = w_in.shape[0]
    page = cache_k.shape[2]
    past_len = page_table.shape[1] * page
    n_past = past_len // MOBA_BLOCK
    assert depth == 1 and seq % MOBA_BLOCK == 0 and past_len % MOBA_BLOCK == 0 and MOBA_BLOCK == 2 * page
    assert dec_s <= GMLP_CHUNK and n_past >= MOBA_TOPK
    l = 0

    w = {
        "w_ple": w_ple[l].astype(BF16),
        "g_post_mix": g_post_mix[l][None], "g_pre_ffn": g_pre_ffn[l][None],
        "g_post_ffn": g_post_ffn[l][None], "g_ple": g_ple[l][None],
    }
    tail_weights = {"w_a_out": w_a_out[l], "w_b_out": w_b_out[l], "w_gate": w_gate[l], "w_o": w_o[l],
                    "w_ffn_in": w_ffn_in[l], "w_ffn_out": w_ffn_out[l], "w_ple_gate": w_ple_gate[l]}
    w_in_b = w_in[l].astype(BF16)
    g_pre = g_pre_mix[l][None]
    g_vn = g_vnorm[l][None]

    w_tril = jnp.tril(w_spatial[l])
    wsp_p = w_tril.astype(BF16)
    bsp_p = b_spatial[l].T
    eye_b = jnp.eye(dec_b, dtype=F32)
    wsp_s = jnp.einsum("ab,gts->gatbs", eye_b, w_tril[:, :dec_s, :dec_s]).reshape(
        GMLP_GROUPS, dec_b * dec_s, dec_b * dec_s).astype(BF16)
    bsp_s = jnp.tile(b_spatial[l][:, :dec_s].T, (dec_b, 1))

    mp = batch * seq
    xp = x_prompt.reshape(mp, d)
    cp, sap, sbp = _rope_tables(jnp.arange(seq, dtype=jnp.int32))
    hp, qp, kp, vp, gmp = _mixer(xp, g_pre, w_in_b, cp, sap, sbp, g_vn, wsp_p, bsp_p,
                                 tm=512, chunk=GMLP_CHUNK, with_vn=False)
    ms = dec_b * dec_s
    xs = x_sample.reshape(ms, d)
    pos_s = past_len + jnp.arange(dec_s, dtype=jnp.int32)
    cs, sas, sbs = (jnp.tile(t, (dec_b, 1)) for t in _rope_tables(pos_s))
    hs, qs, ks, vs, gms, vns = _mixer(xs, g_pre, w_in_b, cs, sas, sbs, g_vn, wsp_s, bsp_s,
                                      tm=ms, chunk=ms, with_vn=True)

    ap, cast = _moba_prompt(qp, kp, vp, batch=batch, seq=seq, casts=tuple(tail_weights.values()))
    w.update(zip(tail_weights.keys(), cast))
    n_pool = depth * cache_k.shape[1]
    cache_k2 = cache_k.reshape(n_pool, page * N_HEADS, HEAD_DIM)
    x1p, hmid_p, kmean = _tail_front(xp, hp, ap, gmp, w, tm=512, scan=(cache_k2, page_table))
    q3 = qs.reshape(dec_b, dec_s, ATTN_WIDTH)
    kn3 = ks.reshape(dec_b, dec_s, ATTN_WIDTH)
    vn3 = vs.reshape(dec_b, dec_s, ATTN_WIDTH)
    sel = _select_blocks(q3, kmean, kn3, n_past=n_past)
    sel_flat = sel[:MOBA_TOPK].T.reshape(-1)
    pad = lambda t: jnp.pad(t, ((0, 0), (0, SAMPLE_ROWS - dec_s), (0, 0)))
    attend = (pad(q3), pad(kn3), pad(vn3), cache_k.reshape(n_pool, page, N_HEADS, HEAD_DIM),
              cache_v.reshape(n_pool, page, N_HEADS, HEAD_DIM), page_table, sel_flat, dec_s)
    yp, a_s = _tail_back(x1p, hmid_p, p_prompt[l].reshape(mp, -1), w, tm=512, attend=attend)

    a_s = a_s[:, :dec_s].reshape(ms, ATTN_WIDTH)
    x1s, hmid_s, _ = _tail_front(xs, hs, a_s, gms, w, tm=ms)
    ys = _tail_back(x1s, hmid_s, p_sample[l].reshape(ms, -1), w, tm=ms)

    return (yp.reshape(batch, seq, d), ys.reshape(dec_b, dec_s, d),
            kp.reshape(1, batch, seq, N_HEADS, HEAD_DIM), vp.reshape(1, batch, seq, N_HEADS, HEAD_DIM),
            ks.reshape(1, dec_b, dec_s, N_HEADS, HEAD_DIM), vs.reshape(1, dec_b, dec_s, N_HEADS, HEAD_DIM),
            vns.reshape(1, dec_b, dec_s, GMLP_WIDTH))
```

```python
import functools

import jax
import jax.numpy as jnp
from jax import lax
from jax.experimental import pallas as pl
from jax.experimental.pallas import tpu as pltpu

F32 = jnp.float32
BF16 = jnp.bfloat16

N_HEADS = 8
HEAD_DIM = 128
ATTN_WIDTH = N_HEADS * HEAD_DIM
MOBA_BLOCK = 256
MOBA_TOPK = 3
ROPE_THETA = 500000.0
ROPE_DIM = HEAD_DIM // 4
ROPE_HALF = ROPE_DIM // 2
GMLP_GROUPS = 8
GMLP_CHUNK = 128
GMLP_WIDTH = 1024
GMLP_GROUP_DIM = GMLP_WIDTH // GMLP_GROUPS
NORM_EPS = 1e-6
NEG_INF = -1e30
LOG2_E = 1.4426950408889634
TAKEN = -3e38
SAMPLE_ROWS = 16
ROW_CHUNK = 256
NORM_ROW_CHUNK = 256
BF16_SUBLANES = 16
MIXER_TILES_PER_WEIGHT = 2

VMEM_LIMIT_BYTES = 56 * 1024 * 1024


def _params(*semantics):
    return pltpu.CompilerParams(dimension_semantics=semantics, vmem_limit_bytes=VMEM_LIMIT_BYTES)


def _rms(x, g):
    return x * lax.rsqrt(jnp.mean(x * x, axis=-1, keepdims=True) + NORM_EPS) * g


def _dot(a, b):
    return jnp.dot(a, b, preferred_element_type=F32)


def _dot_nt(a, b):
    return lax.dot_general(a, b, (((1,), (1,)), ((), ())), preferred_element_type=F32)


def _mixer_kernel(x_ref, g_ref, w_ref, cos_ref, sa_ref, sb_ref, gvn_ref, wsp_ref, bsp_ref,
                  h_out, q_out, k_out, v_out, gm_out, *rest, chunk, rc, with_vn):
    if with_vn:
        vn_out, u_scr, h_scr = rest
    else:
        u_scr, h_scr = rest
    j = pl.program_id(1)
    tm = x_ref.shape[0]
    base = pl.program_id(2) * tm
    row_chunks = [slice(r, r + rc) for r in range(0, tm, rc)]

    def scr(rows, size=rc):
        return pl.ds(pl.multiple_of(base + rows.start, size), size)

    def rope_to(out_ref, z, rows):
        c, sa, sb = cos_ref[rows, :], sa_ref[rows, :], sb_ref[rows, :]
        for hd in range(N_HEADS):
            sl = slice(hd * HEAD_DIM, (hd + 1) * HEAD_DIM)
            zs = z[:, sl]
            out_ref[rows, sl] = (zs * c + pltpu.roll(zs, HEAD_DIM - ROPE_HALF, 1) * sa
                                 + pltpu.roll(zs, ROPE_HALF, 1) * sb)

    @pl.when(j == 0)
    def _():
        for rows in row_chunks:
            hb = _rms(x_ref[rows, :], g_ref[...]).astype(BF16)
            h_out[rows, :] = hb
            h_scr[scr(rows), :] = hb
            rope_to(q_out, _dot(hb, w_ref[...]), rows)

    @pl.when(j == 1)
    def _():
        for rows in row_chunks:
            rope_to(k_out, _dot(h_scr[scr(rows), :], w_ref[...]), rows)

    @pl.when(j == 2)
    def _():
        for rows in row_chunks:
            v_out[rows, :] = _dot(h_scr[scr(rows), :], w_ref[...])

    @pl.when(j == 3)
    def _():
        for rows in row_chunks:
            u_scr[scr(rows), :] = jax.nn.gelu(_dot(h_scr[scr(rows), :], w_ref[...]))

    @pl.when(j == 4)
    def _():
        for rows in row_chunks:
            vg = jax.nn.gelu(_dot(h_scr[scr(rows), :], w_ref[...]))
            xc = vg - jnp.mean(vg, axis=-1, keepdims=True)
            vn = xc * lax.rsqrt(jnp.mean(xc * xc, axis=-1, keepdims=True) + NORM_EPS) * gvn_ref[...]
            if with_vn:
                vn_out[rows, :] = vn
            vnb = vn.astype(BF16)
            for c in range(rc // chunk):
                sub = slice(c * chunk, (c + 1) * chunk)
                dst = slice(rows.start + c * chunk, rows.start + (c + 1) * chunk)
                for g in range(GMLP_GROUPS):
                    cols = slice(g * GMLP_GROUP_DIM, (g + 1) * GMLP_GROUP_DIM)
                    s = _dot(wsp_ref[g], vnb[sub, cols]) + bsp_ref[:, g:g + 1]
                    gm_out[dst, cols] = (u_scr[scr(dst, chunk), cols] * s).astype(BF16)


def _mixer(x, g_pre, w_in, cos_t, sa_t, sb_t, g_vn, wsp, bsp, *, tm, chunk, with_vn):
    m, d = x.shape
    n_seg = w_in.shape[1] // ATTN_WIDTH
    t_blocks = cos_t.shape[0] // tm
    n_row = m // tm
    sub = MIXER_TILES_PER_WEIGHT if n_row % MIXER_TILES_PER_WEIGHT == 0 else 1
    n_sup = n_row // sub
    tab_spec = pl.BlockSpec((tm, HEAD_DIM), lambda s, j, t: ((s * sub + t) % t_blocks, 0))
    const = lambda shape: pl.BlockSpec(shape, lambda s, j, t: (0,) * len(shape))
    x_spec = pl.BlockSpec((tm, d), lambda s, j, t: (jnp.where(j == 0, s * sub + t, s * sub + sub - 1), 0))

    def out_spec(w, seg):
        def idx(s, j, t):
            after = jnp.minimum((s + 1) * sub, n_row - 1)
            return (jnp.where(j < seg, s * sub, jnp.where(j == seg, s * sub + t, after)), 0)
        return pl.BlockSpec((tm, w), idx)

    out_shape = [jax.ShapeDtypeStruct((m, d), BF16),
                 jax.ShapeDtypeStruct((m, ATTN_WIDTH), F32),
                 jax.ShapeDtypeStruct((m, ATTN_WIDTH), F32),
                 jax.ShapeDtypeStruct((m, ATTN_WIDTH), F32),
                 jax.ShapeDtypeStruct((m, GMLP_WIDTH), BF16)]
    out_specs = [out_spec(d, 0), out_spec(ATTN_WIDTH, 0), out_spec(ATTN_WIDTH, 1), out_spec(ATTN_WIDTH, 2),
                 out_spec(GMLP_WIDTH, n_seg - 1)]
    if with_vn:
        out_shape.append(jax.ShapeDtypeStruct((m, GMLP_WIDTH), F32))
        out_specs.append(out_spec(GMLP_WIDTH, n_seg - 1))
    return pl.pallas_call(
        functools.partial(_mixer_kernel, chunk=chunk, rc=max(chunk, min(tm, ROW_CHUNK)), with_vn=with_vn),
        grid=(n_sup, n_seg, sub),
        in_specs=[x_spec,
                  const((1, d)),
                  pl.BlockSpec((d, ATTN_WIDTH), lambda s, j, t: (0, j)),
                  tab_spec, tab_spec, tab_spec,
                  const((1, GMLP_WIDTH)),
                  const((GMLP_GROUPS, chunk, chunk)),
                  const((chunk, GMLP_GROUPS))],
        out_specs=out_specs,
        out_shape=out_shape,
        scratch_shapes=[pltpu.VMEM((sub * tm, GMLP_WIDTH), F32), pltpu.VMEM((sub * tm, d), BF16)],
        compiler_params=_params("arbitrary", "arbitrary", "arbitrary"),
        name="mixer",
    )(x, g_pre, w_in, cos_t, sa_t, sb_t, g_vn, wsp, bsp)


def _moba_prompt_kernel(q_ref, k_ref, v_ref, *refs):
    n_cast = (len(refs) - 4) // 2
    o_ref = refs[n_cast]
    kb_scr, vt_scr, s_scr = refs[-3:]
    for src_ref, dst_ref in zip(refs[:n_cast], refs[n_cast + 1:2 * n_cast + 1]):
        dst_ref[...] = src_ref[...].astype(BF16)
    seq = q_ref.shape[0]
    nblk = seq // MOBA_BLOCK
    blk = MOBA_BLOCK
    q_scale = HEAD_DIM ** -0.5 * LOG2_E
    kb_scr[...] = k_ref[...].astype(BF16)
    vt_scr[:HEAD_DIM, :] = v_ref[...].T.astype(BF16)
    vt_scr[HEAD_DIM:, :] = jnp.ones((BF16_SUBLANES, seq), BF16)
    qt = q_ref[...].T
    kt = k_ref[...].T

    blk_id = lax.broadcasted_iota(jnp.int32, (nblk, seq), 0)
    q_blk = lax.broadcasted_iota(jnp.int32, (nblk, seq), 1) // blk
    gate = jnp.zeros((nblk, seq), F32)
    for n in range(nblk - 1):
        kmean_n = jnp.mean(kt[:, n * blk:(n + 1) * blk], axis=1, keepdims=True)
        g_past = jnp.sum(qt[:, (n + 1) * blk:] * kmean_n, axis=0, keepdims=True)
        g_n = jnp.concatenate([jnp.zeros((1, (n + 1) * blk), F32), g_past], axis=1)
        gate = jnp.where(blk_id == n, g_n, gate)
    rank = jnp.zeros((nblk, seq), jnp.int32)
    for m in range(nblk):
        g_m = gate[m:m + 1, :]
        beats = (m < q_blk) & ((g_m > gate) | ((g_m == gate) & (m < blk_id)))
        rank = rank + beats.astype(jnp.int32)
    keep = jnp.where((blk_id < q_blk) & (rank < MOBA_TOPK), 1.0, 0.0)

    key_i = lax.broadcasted_iota(jnp.int32, (blk, blk), 0)
    qry_i = lax.broadcasted_iota(jnp.int32, (blk, blk), 1)
    causal = key_i <= qry_i

    for j in range(nblk):
        cols = slice(j * blk, (j + 1) * blk)
        qb = (q_ref[cols, :] * q_scale).astype(BF16)
        keep_j = keep[:, cols]
        m_run = None
        for n in range(j + 1):
            st = _dot_nt(kb_scr[n * blk:(n + 1) * blk, :], qb)
            if n == j:
                st = jnp.where(causal, st, NEG_INF)
            else:
                st = jnp.where(keep_j[n:n + 1, :] > 0.5, st, NEG_INF)
            s_scr[n] = st
            m_n = jnp.max(st, axis=0, keepdims=True)
            m_run = m_n if m_run is None else jnp.maximum(m_run, m_n)
        acc = jnp.zeros((HEAD_DIM + BF16_SUBLANES, blk), F32)
        for n in range(j + 1):
            p = jnp.exp2(s_scr[n] - m_run)
            acc = acc + _dot(vt_scr[:, n * blk:(n + 1) * blk], p.astype(BF16))
        o_ref[cols, :] = (acc[:HEAD_DIM, :] / acc[HEAD_DIM:HEAD_DIM + 1, :]).T.astype(BF16)


def _moba_prompt(q, k, v, *, batch, seq, casts=()):
    n_steps = batch * N_HEADS
    spec = pl.BlockSpec((seq, HEAD_DIM), lambda b, h: (b, h))

    def slab_spec(wt):
        rows = wt.shape[0] // n_steps
        assert rows * n_steps == wt.shape[0] and rows % BF16_SUBLANES == 0
        return pl.BlockSpec((rows, wt.shape[1]), lambda b, h: (b * N_HEADS + h, 0))

    slabs = [slab_spec(wt) for wt in casts]
    out = pl.pallas_call(
        _moba_prompt_kernel,
        grid=(batch, N_HEADS),
        in_specs=[spec, spec, spec] + slabs,
        out_specs=[spec] + slabs,
        out_shape=[jax.ShapeDtypeStruct(q.shape, BF16)] + [jax.ShapeDtypeStruct(wt.shape, BF16) for wt in casts],
        scratch_shapes=[pltpu.VMEM((seq, HEAD_DIM), BF16), pltpu.VMEM((HEAD_DIM + BF16_SUBLANES, seq), BF16),
                        pltpu.VMEM((seq // MOBA_BLOCK, MOBA_BLOCK, MOBA_BLOCK), F32)],
        compiler_params=_params("parallel", "parallel"),
        name="moba_prompt",
    )(q, k, v, *casts)
    return out[0], tuple(out[1:])


def _scan_key_means(page_refs, km_ref, group):
    n_blk = len(page_refs) // 2
    page_rows = page_refs[0].shape[0] // N_HEADS

    def page_sum(ref):
        return jnp.sum(ref[...].reshape(page_rows, N_HEADS, HEAD_DIM), axis=0)

    for t in range(n_blk):
        tot = (page_sum(page_refs[2 * t]) + page_sum(page_refs[2 * t + 1])) * (1.0 / MOBA_BLOCK)
        for h in range(N_HEADS):
            km_ref[h, pl.ds(group * n_blk + t, 1), :] = tot[h:h + 1, :]


def _select_kernel(q_ref, km_ref, kn_ref, sel_ref, gate_scr, *, dec_s, n_past):
    dec_b = km_ref.shape[0]
    gate_scr[...] = jnp.full(gate_scr.shape, NEG_INF, F32)
    for b in range(dec_b):
        own_mean = jnp.sum(kn_ref[b * dec_s:(b + 1) * dec_s, :], axis=0, keepdims=True) * (1.0 / MOBA_BLOCK)
        for h in range(N_HEADS):
            cols = slice(h * HEAD_DIM, (h + 1) * HEAD_DIM)
            km = km_ref[b, h]
            for i in range(dec_s):
                col = (b * N_HEADS + h) * dec_s + i
                qi = q_ref[b * dec_s + i:b * dec_s + i + 1, cols]
                gate_scr[0:n_past, col:col + 1] = jnp.sum(km * qi, axis=-1, keepdims=True)
                gate_scr[n_past:n_past + 1, col:col + 1] = jnp.sum(qi * own_mean[:, cols], axis=-1,
                                                                   keepdims=True)
    gate = gate_scr[...]
    blk = lax.broadcasted_iota(jnp.int32, gate.shape, 0)
    gate = jnp.where(blk < n_past, gate, NEG_INF)
    out_row = lax.broadcasted_iota(jnp.int32, sel_ref.shape, 0)
    sel = jnp.zeros(sel_ref.shape, jnp.int32)
    for t in range(MOBA_TOPK):
        best = jnp.max(gate, axis=0, keepdims=True)
        idx = jnp.min(jnp.where(gate == best, blk, gate.shape[0]), axis=0, keepdims=True)
        sel = jnp.where(out_row == t, idx, sel)
        gate = jnp.where(blk == idx, TAKEN, gate)
    sel_ref[...] = sel


def _select_blocks(q2, kmean, kn2, *, dec_s, n_past):
    cols = q2.shape[0] * N_HEADS
    gate_rows = 8 * (-(-(n_past + 1) // 8))
    return pl.pallas_call(
        functools.partial(_select_kernel, dec_s=dec_s, n_past=n_past),
        out_shape=jax.ShapeDtypeStruct((8, cols), jnp.int32),
        scratch_shapes=[pltpu.VMEM((gate_rows, cols), F32)],
        compiler_params=pltpu.CompilerParams(vmem_limit_bytes=VMEM_LIMIT_BYTES),
        name="select_blocks",
    )(q2, kmean, kn2)


def _sample_fetch(step, n_steps, pt_ref, sel_ref, ck_hbm, cv_hbm, kbuf, vbuf, sem, *, dec_s, n_slots):
    n_pages = 2 * n_slots

    def page_copies(step_idx, slot):
        bb, hh = step_idx // N_HEADS, step_idx % N_HEADS
        copies = []
        for c in range(n_pages):
            query, pick = (c // 2) // MOBA_TOPK, (c // 2) % MOBA_TOPK
            blk = sel_ref[pick, step_idx * dec_s + query]
            pg = pt_ref[bb, 2 * blk + c % 2]
            src = (pg, slice(None), hh, slice(None))
            copies.append(pltpu.make_async_copy(ck_hbm.at[src], kbuf.at[slot, c], sem.at[0, slot]))
            copies.append(pltpu.make_async_copy(cv_hbm.at[src], vbuf.at[slot, c], sem.at[1, slot]))
        return copies

    @pl.when(step == 0)
    def _():
        for cp in page_copies(step, 0):
            cp.start()

    @pl.when(step + 1 < n_steps)
    def _():
        for cp in page_copies(step + 1, (step + 1) % 2):
            cp.start()

    slot = step % 2
    for cp in page_copies(step, slot):
        cp.wait()
    return slot


def _sample_attend(slot, q_ref, kn_ref, vn_ref, o_ref, kbuf, vbuf, *, dec_s, n_slots):
    n_pages = 2 * n_slots
    page = kbuf.shape[2]
    k_refs = [kbuf.at[slot, c] for c in range(n_pages)]
    v_refs = [vbuf.at[slot, c] for c in range(n_pages)]
    scale = HEAD_DIM ** -0.5
    rows = q_ref.shape[0]
    qb = q_ref[...].astype(BF16)
    row = lax.broadcasted_iota(jnp.int32, (rows, page), 0)
    s_list = []
    m_run = None
    for c in range(n_pages):
        s = _dot_nt(qb, k_refs[c][...].astype(BF16)) * scale
        s = jnp.where(row == c // (2 * MOBA_TOPK), s, NEG_INF)
        s_list.append(s)
        m_c = jnp.max(s, axis=-1, keepdims=True)
        m_run = m_c if m_run is None else jnp.maximum(m_run, m_c)
    s_own = _dot_nt(qb, kn_ref[...].astype(BF16)) * scale
    r_o = lax.broadcasted_iota(jnp.int32, (rows, rows), 0)
    c_o = lax.broadcasted_iota(jnp.int32, (rows, rows), 1)
    s_own = jnp.where((c_o <= r_o) & (c_o < dec_s), s_own, NEG_INF)
    m_run = jnp.maximum(m_run, jnp.max(s_own, axis=-1, keepdims=True))
    p_own = jnp.exp(s_own - m_run)
    l_run = jnp.sum(p_own, axis=-1, keepdims=True)
    acc = _dot(p_own.astype(BF16), vn_ref[...].astype(BF16))
    for c in range(n_pages):
        p = jnp.exp(s_list[c] - m_run)
        l_run = l_run + jnp.sum(p, axis=-1, keepdims=True)
        acc = acc + _dot(p.astype(BF16), v_refs[c][...].astype(BF16))
    o_ref[...] = (acc / l_run).astype(BF16)


def _gated_mix_kernel(a_ref, b_ref, h_ref, wa_ref, wb_ref, wga_ref, wgb_ref, o_ref):
    h = h_ref[...]
    ga = jax.nn.sigmoid(_dot(h, wga_ref[...]))
    gb = jax.nn.sigmoid(_dot(h, wgb_ref[...]))
    a = _dot(a_ref[...], wa_ref[...])
    b = _dot(b_ref[...], wb_ref[...])
    o_ref[...] = (ga * a + gb * b).astype(BF16)


def _gated_mix(attn, gm, h, w_a, w_b, w_gate, *, tm, tn):
    m, d = h.shape
    n_col = d // tn
    return pl.pallas_call(
        _gated_mix_kernel,
        grid=(n_col, m // tm),
        in_specs=[pl.BlockSpec((tm, attn.shape[1]), lambda j, i: (i, 0)),
                  pl.BlockSpec((tm, gm.shape[1]), lambda j, i: (i, 0)),
                  pl.BlockSpec((tm, d), lambda j, i: (i, 0)),
                  pl.BlockSpec((w_a.shape[0], tn), lambda j, i: (0, j)),
                  pl.BlockSpec((w_b.shape[0], tn), lambda j, i: (0, j)),
                  pl.BlockSpec((d, tn), lambda j, i: (0, j)),
                  pl.BlockSpec((d, tn), lambda j, i: (0, j + n_col))],
        out_specs=pl.BlockSpec((tm, tn), lambda j, i: (i, j)),
        out_shape=jax.ShapeDtypeStruct((m, d), BF16),
        compiler_params=_params("parallel", "parallel"),
        name="gated_mix",
    )(attn, gm, h, w_a, w_b, w_gate, w_gate)


def _row_chunks(tm, rc):
    rc = min(tm, rc)
    return [slice(r, r + rc) for r in range(0, tm, rc)]


def _mix_out_kernel(mix_ref, x_ref, wo_ref, gpost_ref, gpre_ref, x1_ref, xn_ref):
    for rows in _row_chunks(x_ref.shape[0], NORM_ROW_CHUNK):
        mix = _dot(mix_ref[rows, :], wo_ref[...])
        x1 = x_ref[rows, :] + _rms(mix, gpost_ref[...])
        x1_ref[rows, :] = x1
        xn_ref[rows, :] = _rms(x1, gpre_ref[...]).astype(BF16)


def _mix_out(mixin, x, w_o, g_post, g_pre_ffn, *, tm):
    m, d = x.shape
    row = pl.BlockSpec((tm, d), lambda i: (i, 0))
    vec = pl.BlockSpec((1, d), lambda i: (0, 0))
    return pl.pallas_call(
        _mix_out_kernel,
        grid=(m // tm,),
        in_specs=[row, row, pl.BlockSpec((d, d), lambda i: (0, 0), pipeline_mode=pl.Buffered(1)), vec, vec],
        out_specs=[row, row],
        out_shape=[jax.ShapeDtypeStruct((m, d), F32), jax.ShapeDtypeStruct((m, d), BF16)],
        compiler_params=_params("parallel"),
        name="mix_out",
    )(mixin, x, w_o, g_post, g_pre_ffn)


def _ffn_in_kernel(x_ref, wa_ref, wg_ref, o_ref):
    x = x_ref[...]
    a = _dot(x, wa_ref[...])
    g = _dot(x, wg_ref[...])
    o_ref[...] = (jax.nn.silu(a) * g).astype(BF16)


def _ffn_in_scan_kernel(pt_ref, x_ref, wa_ref, wg_ref, *refs, groups):
    page_refs, o_ref, km_ref = refs[:-2], refs[-2], refs[-1]
    step = pl.program_id(0) * pl.num_programs(1) + pl.program_id(1)
    _scan_key_means(page_refs, km_ref, step % groups)
    _ffn_in_kernel(x_ref, wa_ref, wg_ref, o_ref)


def _ffn_in(xn, w_ffn_in, *, tm, tn, scan=None):
    m, d = xn.shape
    hidden = w_ffn_in.shape[1] // 2
    n_col, n_row = hidden // tn, m // tm
    in_specs = [pl.BlockSpec((tm, d), lambda j, i, *_: (i, 0)),
                pl.BlockSpec((d, tn), lambda j, i, *_: (0, j)),
                pl.BlockSpec((d, tn), lambda j, i, *_: (0, j + n_col))]
    out_spec = pl.BlockSpec((tm, tn), lambda j, i, *_: (i, j))
    out_shape = jax.ShapeDtypeStruct((m, hidden), BF16)
    if scan is None:
        return pl.pallas_call(
            _ffn_in_kernel, grid=(n_col, n_row), in_specs=in_specs, out_specs=out_spec, out_shape=out_shape,
            compiler_params=_params("parallel", "parallel"), name="ffn_in",
        )(xn, w_ffn_in, w_ffn_in)

    cache_k2, page_table = scan
    dec_b, n_pages = page_table.shape
    n_steps = n_col * n_row
    pps = dec_b * n_pages // n_steps
    groups = n_pages // pps
    assert pps * n_steps == dec_b * n_pages and groups * pps == n_pages and pps % 2 == 0
    n_past = n_pages // 2

    def page_spec(c):
        def idx(j, i, pt):
            step = j * n_row + i
            return (pt[step // groups, (step % groups) * pps + c], 0, 0)
        return pl.BlockSpec((None,) + cache_k2.shape[1:], idx)

    grid_spec = pltpu.PrefetchScalarGridSpec(
        num_scalar_prefetch=1,
        grid=(n_col, n_row),
        in_specs=in_specs + [page_spec(c) for c in range(pps)],
        out_specs=[out_spec,
                   pl.BlockSpec((None, N_HEADS, n_past, HEAD_DIM),
                                lambda j, i, pt: ((j * n_row + i) // groups, 0, 0, 0))],
    )
    return pl.pallas_call(
        functools.partial(_ffn_in_scan_kernel, groups=groups),
        grid_spec=grid_spec,
        out_shape=[out_shape, jax.ShapeDtypeStruct((dec_b, N_HEADS, n_past, HEAD_DIM), F32)],
        compiler_params=_params("arbitrary", "arbitrary"),
        name="ffn_in_scan",
    )(page_table, xn, w_ffn_in, w_ffn_in, *([cache_k2] * pps))


def _ffn_out_ple_kernel(hm_ref, w_ref, x1_ref, gpost_ref, gple_ref, p_ref, wpg_ref, wp_ref, y_ref, acc_scr,
                        side_work=lambda: None):
    k = pl.program_id(1)
    last = pl.num_programs(1) - 1

    @pl.when(k == 0)
    def _():
        side_work()
        acc_scr[...] = _dot(hm_ref[...], w_ref[...])

    @pl.when((k > 0) & (k < last))
    def _():
        side_work()
        acc_scr[...] += _dot(hm_ref[...], w_ref[...])

    @pl.when(k == last)
    def _():
        side_work()
        for rows in _row_chunks(x1_ref.shape[0], NORM_ROW_CHUNK):
            f = acc_scr[rows, :] + _dot(hm_ref[rows, :], w_ref[...])
            x2 = x1_ref[rows, :] + _rms(f, gpost_ref[...])
            xg = _rms(x2, gple_ref[...]).astype(BF16)
            gate = jax.nn.sigmoid(_dot(xg, wpg_ref[...]))
            y_ref[rows, :] = x2 + _dot(p_ref[rows, :].astype(BF16), wp_ref[...]) * gate


def _ffn_out_ple_attend_kernel(pt_ref, sel_ref, hm_ref, w_ref, x1_ref, gpost_ref, gple_ref, p_ref, wpg_ref,
                               wp_ref, q_ref, kn_ref, vn_ref, ck_hbm, cv_hbm, y_ref, a_ref,
                               acc_scr, kbuf, vbuf, sem, *, dec_s, n_slots):
    step = pl.program_id(0) * pl.num_programs(1) + pl.program_id(1)
    n_steps = pl.num_programs(0) * pl.num_programs(1)
    slot = _sample_fetch(step, n_steps, pt_ref, sel_ref, ck_hbm, cv_hbm, kbuf, vbuf, sem,
                         dec_s=dec_s, n_slots=n_slots)
    attend = functools.partial(_sample_attend, slot, q_ref, kn_ref, vn_ref, a_ref, kbuf, vbuf,
                               dec_s=dec_s, n_slots=n_slots)
    _ffn_out_ple_kernel(hm_ref, w_ref, x1_ref, gpost_ref, gple_ref, p_ref, wpg_ref, wp_ref, y_ref, acc_scr,
                        side_work=attend)


def _ffn_out_ple(hmid, w_ffn_out, x1, g_post_ffn, g_ple, p, w_ple_gate, w_ple, *, tm, tk, attend=None):
    m, d = x1.shape
    hidden = hmid.shape[1]
    n_row, n_k = m // tm, hidden // tk
    assert n_k >= 2
    row = pl.BlockSpec((tm, d), lambda i, k, *_: (i, 0))
    vec = pl.BlockSpec((1, d), lambda i, k, *_: (0, 0))
    const = lambda shape: pl.BlockSpec(shape, lambda i, k, *_: (0, 0), pipeline_mode=pl.Buffered(1))
    in_specs = [pl.BlockSpec((tm, tk), lambda i, k, *_: (i, k)),
                pl.BlockSpec((tk, d), lambda i, k, *_: (k, 0)),
                row, vec, vec,
                pl.BlockSpec((tm, p.shape[1]), lambda i, k, *_: (i, 0)),
                const(w_ple_gate.shape), const(w_ple.shape)]
    args = (hmid, w_ffn_out, x1, g_post_ffn, g_ple, p, w_ple_gate, w_ple)
    y_shape = jax.ShapeDtypeStruct((m, d), F32)
    acc = pltpu.VMEM((tm, d), F32)
    if attend is None:
        return pl.pallas_call(
            _ffn_out_ple_kernel, grid=(n_row, n_k), in_specs=in_specs, out_specs=row, out_shape=y_shape,
            scratch_shapes=[acc], compiler_params=_params("parallel", "arbitrary"), name="ffn_out_ple",
        )(*args)

    qkv, cache_k4, cache_v4, page_table, sel, dec_s = attend
    _, dec_b, rows, width = qkv.shape
    page = cache_k4.shape[1]
    n_slots = dec_s * MOBA_TOPK
    assert n_row * n_k == dec_b * N_HEADS
    pair_idx = lambda i, k: ((i * n_k + k) // N_HEADS, 0, (i * n_k + k) % N_HEADS)
    pair_spec = pl.BlockSpec((None, rows, HEAD_DIM), lambda i, k, *_: pair_idx(i, k))
    part_spec = lambda part: pl.BlockSpec((None, None, rows, HEAD_DIM), lambda i, k, *_: (part,) + pair_idx(i, k))
    hbm_spec = pl.BlockSpec(memory_space=pl.ANY)
    grid_spec = pltpu.PrefetchScalarGridSpec(
        num_scalar_prefetch=2,
        grid=(n_row, n_k),
        in_specs=in_specs + [part_spec(0), part_spec(1), part_spec(2), hbm_spec, hbm_spec],
        out_specs=[row, pair_spec],
        scratch_shapes=[acc,
                        pltpu.VMEM((2, 2 * n_slots, page, HEAD_DIM), F32),
                        pltpu.VMEM((2, 2 * n_slots, page, HEAD_DIM), F32),
                        pltpu.SemaphoreType.DMA((2, 2))],
    )
    return pl.pallas_call(
        functools.partial(_ffn_out_ple_attend_kernel, dec_s=dec_s, n_slots=n_slots),
        grid_spec=grid_spec,
        out_shape=[y_shape, jax.ShapeDtypeStruct((dec_b, rows, width), BF16)],
        compiler_params=_params("arbitrary", "arbitrary"),
        name="ffn_out_ple_attend",
    )(page_table, sel, *args, qkv, qkv, qkv, cache_k4, cache_v4)


def _tail_front(x, h, attn, gm, w, *, tm, scan=None):
    m, d = x.shape
    hidden = w["w_ffn_out"].shape[0]
    mixin = _gated_mix(attn, gm, h, w["w_a_out"], w["w_b_out"], w["w_gate"], tm=min(m, 1024), tn=512)
    x1, xn = _mix_out(mixin, x, w["w_o"], w["g_post_mix"], w["g_pre_ffn"], tm=tm)
    if scan is None:
        return x1, _ffn_in(xn, w["w_ffn_in"], tm=tm, tn=hidden // 4), None
    hmid, kmean = _ffn_in(xn, w["w_ffn_in"], tm=tm, tn=hidden // 4, scan=scan)
    return x1, hmid, kmean


def _tail_back(x1, hmid, p, w, *, tm, attend=None):
    return _ffn_out_ple(hmid, w["w_ffn_out"], x1, w["g_post_ffn"], w["g_ple"], p, w["w_ple_gate"], w["w_ple"],
                        tm=tm, tk=w["w_ffn_out"].shape[0] // 4, attend=attend)


def _rope_tables(pos):
    lane = jnp.arange(HEAD_DIM, dtype=jnp.int32)
    freqs = jnp.power(jnp.float32(ROPE_THETA), -2.0 * (lane % ROPE_HALF).astype(F32) / ROPE_DIM)
    ang = pos.astype(F32)[:, None] * freqs[None, :]
    cos, sin = jnp.cos(ang), jnp.sin(ang)
    c = jnp.where(lane < ROPE_DIM, cos, 1.0)
    sa = jnp.where(lane < ROPE_HALF, -sin, 0.0)
    sb = jnp.where((lane >= ROPE_HALF) & (lane < ROPE_DIM), sin, 0.0)
    return c, sa, sb


def kernel(x_prompt, x_sample, cache_k, cache_v, page_table, p_prompt, p_sample, g_pre_mix, w_in, g_vnorm, w_spatial, b_spatial, w_a_out, w_b_out, w_gate, w_o, g_post_mix, g_pre_ffn, w_ffn_in, w_ffn_out, g_post_ffn, g_ple, w_ple_gate, w_ple):
    batch, seq, d = x_prompt.shape
    dec_b, dec_s, _ = x_sample.shape
    depth = w_in.shape[0]
    page = cache_k.shape[2]
    past_len = page_table.shape[1] * page
    n_past = past_len // MOBA_BLOCK
    assert depth == 1 and seq % MOBA_BLOCK == 0 and past_len % MOBA_BLOCK == 0 and MOBA_BLOCK == 2 * page
    assert dec_s <= GMLP_CHUNK and n_past >= MOBA_TOPK
    l = 0

    w = {
        "w_ple": w_ple[l].astype(BF16),
        "g_post_mix": g_post_mix[l][None], "g_pre_ffn": g_pre_ffn[l][None],
        "g_post_ffn": g_post_ffn[l][None], "g_ple": g_ple[l][None],
    }
    tail_weights = {"w_a_out": w_a_out[l], "w_b_out": w_b_out[l], "w_gate": w_gate[l], "w_o": w_o[l],
                    "w_ffn_in": w_ffn_in[l], "w_ffn_out": w_ffn_out[l], "w_ple_gate": w_ple_gate[l]}
    w_in_b = w_in[l].astype(BF16)
    g_pre = g_pre_mix[l][None]
    g_vn = g_vnorm[l][None]

    w_tril = jnp.tril(w_spatial[l])
    wsp_p = w_tril.astype(BF16)
    bsp_p = b_spatial[l].T
    eye_b = jnp.eye(dec_b, dtype=F32)
    wsp_s = jnp.einsum("ab,gts->gatbs", eye_b, w_tril[:, :dec_s, :dec_s]).reshape(
        GMLP_GROUPS, dec_b * dec_s, dec_b * dec_s).astype(BF16)
    bsp_s = jnp.tile(b_spatial[l][:, :dec_s].T, (dec_b, 1))

    mp = batch * seq
    xp = x_prompt.reshape(mp, d)
    cp, sap, sbp = _rope_tables(jnp.arange(seq, dtype=jnp.int32))
    hp, qp, kp, vp, gmp = _mixer(xp, g_pre, w_in_b, cp, sap, sbp, g_vn, wsp_p, bsp_p,
                                 tm=512, chunk=GMLP_CHUNK, with_vn=False)
    ms = dec_b * dec_s
    xs = x_sample.reshape(ms, d)
    pos_s = past_len + jnp.arange(dec_s, dtype=jnp.int32)
    cs, sas, sbs = _rope_tables(jnp.tile(pos_s, dec_b))
    hs, qs, ks, vs, gms, vns = _mixer(xs, g_pre, w_in_b, cs, sas, sbs, g_vn, wsp_s, bsp_s,
                                      tm=ms, chunk=ms, with_vn=True)

    ap, cast = _moba_prompt(qp, kp, vp, batch=batch, seq=seq, casts=tuple(tail_weights.values()))
    w.update(zip(tail_weights.keys(), cast))
    n_pool = depth * cache_k.shape[1]
    cache_k2 = cache_k.reshape(n_pool, page * N_HEADS, HEAD_DIM)
    x1p, hmid_p, kmean = _tail_front(xp, hp, ap, gmp, w, tm=512, scan=(cache_k2, page_table))
    sel = _select_blocks(qs, kmean, ks, dec_s=dec_s, n_past=n_past)
    qkv = jnp.pad(jnp.stack([qs, ks, vs]).reshape(3, dec_b, dec_s, ATTN_WIDTH),
                  ((0, 0), (0, 0), (0, SAMPLE_ROWS - dec_s), (0, 0)))
    attend = (qkv, cache_k.reshape(n_pool, page, N_HEADS, HEAD_DIM),
              cache_v.reshape(n_pool, page, N_HEADS, HEAD_DIM), page_table, sel, dec_s)
    yp, a_s = _tail_back(x1p, hmid_p, p_prompt[l].reshape(mp, -1), w, tm=512, attend=attend)

    a_s = a_s[:, :dec_s].reshape(ms, ATTN_WIDTH)
    x1s, hmid_s, _ = _tail_front(xs, hs, a_s, gms, w, tm=ms)
    ys = _tail_back(x1s, hmid_s, p_sample[l].reshape(ms, -1), w, tm=ms)

    return (yp.reshape(batch, seq, d), ys.reshape(dec_b, dec_s, d),
            kp.reshape(1, batch, seq, N_HEADS, HEAD_DIM), vp.reshape(1, batch, seq, N_HEADS, HEAD_DIM),
            ks.reshape(1, dec_b, dec_s, N_HEADS, HEAD_DIM), vs.reshape(1, dec_b, dec_s, N_HEADS, HEAD_DIM),
            vns.reshape(1, dec_b, dec_s, GMLP_WIDTH))
```

```python
import functools

import jax
import jax.numpy as jnp
from jax import lax
from jax.experimental import pallas as pl
from jax.experimental.pallas import tpu as pltpu

F32 = jnp.float32
BF16 = jnp.bfloat16

N_HEADS = 8
HEAD_DIM = 128
ATTN_WIDTH = N_HEADS * HEAD_DIM
MOBA_BLOCK = 256
MOBA_TOPK = 3
ROPE_THETA = 500000.0
ROPE_DIM = HEAD_DIM // 4
ROPE_HALF = ROPE_DIM // 2
GMLP_GROUPS = 8
GMLP_CHUNK = 128
GMLP_WIDTH = 1024
GMLP_GROUP_DIM = GMLP_WIDTH // GMLP_GROUPS
NORM_EPS = 1e-6
NEG_INF = -1e30
LOG2_E = 1.4426950408889634
TAKEN = -3e38
SAMPLE_ROWS = 16
MIXER_ROWS = 256
NORM_ROW_CHUNK = 256
BF16_SUBLANES = 16

VMEM_LIMIT_BYTES = 56 * 1024 * 1024


def _params(*semantics):
    return pltpu.CompilerParams(dimension_semantics=semantics, vmem_limit_bytes=VMEM_LIMIT_BYTES)


def _rms(x, g):
    return x * lax.rsqrt(jnp.mean(x * x, axis=-1, keepdims=True) + NORM_EPS) * g


def _dot(a, b):
    return jnp.dot(a, b, preferred_element_type=F32)


def _dot_nt(a, b):
    return lax.dot_general(a, b, (((1,), (1,)), ((), ())), preferred_element_type=F32)


def _mixer_kernel(x_ref, g_ref, w_ref, cos_ref, sa_ref, sb_ref, gvn_ref, wsp_ref, bsp_ref,
                  h_out, q_out, k_out, v_out, gm_out, *rest, chunk, with_vn):
    tm = x_ref.shape[0]
    seg = ATTN_WIDTH
    hb = _rms(x_ref[...], g_ref[...]).astype(BF16)
    h_out[...] = hb

    def proj(s):
        return _dot(hb, w_ref[:, s * seg:(s + 1) * seg])

    def rope_to(out_ref, z):
        c, sa, sb = cos_ref[...], sa_ref[...], sb_ref[...]
        for hd in range(N_HEADS):
            sl = slice(hd * HEAD_DIM, (hd + 1) * HEAD_DIM)
            zs = z[:, sl]
            out_ref[:, sl] = (zs * c + pltpu.roll(zs, HEAD_DIM - ROPE_HALF, 1) * sa
                              + pltpu.roll(zs, ROPE_HALF, 1) * sb)

    rope_to(q_out, proj(0))
    rope_to(k_out, proj(1))
    v_out[...] = proj(2)
    u = jax.nn.gelu(proj(3))
    vg = jax.nn.gelu(proj(4))
    xc = vg - jnp.mean(vg, axis=-1, keepdims=True)
    vn = xc * lax.rsqrt(jnp.mean(xc * xc, axis=-1, keepdims=True) + NORM_EPS) * gvn_ref[...]
    if with_vn:
        rest[0][...] = vn
    vnb = vn.astype(BF16)
    for c in range(tm // chunk):
        rows = slice(c * chunk, (c + 1) * chunk)
        for g in range(GMLP_GROUPS):
            cols = slice(g * GMLP_GROUP_DIM, (g + 1) * GMLP_GROUP_DIM)
            s = _dot(wsp_ref[g], vnb[rows, cols]) + bsp_ref[:, g:g + 1]
            gm_out[rows, cols] = (u[rows, cols] * s).astype(BF16)


def _mixer(x, g_pre, w_in, cos_t, sa_t, sb_t, g_vn, wsp, bsp, *, tm, chunk, with_vn):
    m, d = x.shape
    assert w_in.shape[1] == 3 * ATTN_WIDTH + 2 * GMLP_WIDTH and ATTN_WIDTH == GMLP_WIDTH
    t_blocks = cos_t.shape[0] // tm
    tab_spec = pl.BlockSpec((tm, HEAD_DIM), lambda i: (i % t_blocks, 0))
    row_spec = lambda w: pl.BlockSpec((tm, w), lambda i: (i, 0))
    const = lambda shape: pl.BlockSpec(shape, lambda i: (0,) * len(shape))
    out_shape = [jax.ShapeDtypeStruct((m, d), BF16),
                 jax.ShapeDtypeStruct((m, ATTN_WIDTH), F32),
                 jax.ShapeDtypeStruct((m, ATTN_WIDTH), F32),
                 jax.ShapeDtypeStruct((m, ATTN_WIDTH), F32),
                 jax.ShapeDtypeStruct((m, GMLP_WIDTH), BF16)]
    out_specs = [row_spec(d), row_spec(ATTN_WIDTH), row_spec(ATTN_WIDTH), row_spec(ATTN_WIDTH),
                 row_spec(GMLP_WIDTH)]
    if with_vn:
        out_shape.append(jax.ShapeDtypeStruct((m, GMLP_WIDTH), F32))
        out_specs.append(row_spec(GMLP_WIDTH))
    return pl.pallas_call(
        functools.partial(_mixer_kernel, chunk=chunk, with_vn=with_vn),
        grid=(m // tm,),
        in_specs=[row_spec(d),
                  const((1, d)),
                  pl.BlockSpec(w_in.shape, lambda i: (0, 0), pipeline_mode=pl.Buffered(1)),
                  tab_spec, tab_spec, tab_spec,
                  const((1, GMLP_WIDTH)),
                  const((GMLP_GROUPS, chunk, chunk)),
                  const((chunk, GMLP_GROUPS))],
        out_specs=out_specs,
        out_shape=out_shape,
        compiler_params=_params("parallel"),
        name="mixer",
    )(x, g_pre, w_in, cos_t, sa_t, sb_t, g_vn, wsp, bsp)


def _moba_prompt_kernel(q_ref, k_ref, v_ref, *refs):
    n_cast = (len(refs) - 4) // 2
    o_ref = refs[n_cast]
    kb_scr, vt_scr, s_scr = refs[-3:]
    for src_ref, dst_ref in zip(refs[:n_cast], refs[n_cast + 1:2 * n_cast + 1]):
        dst_ref[...] = src_ref[...].astype(BF16)
    seq = q_ref.shape[0]
    nblk = seq // MOBA_BLOCK
    blk = MOBA_BLOCK
    q_scale = HEAD_DIM ** -0.5 * LOG2_E
    kb_scr[...] = k_ref[...].astype(BF16)
    vt_scr[:HEAD_DIM, :] = v_ref[...].T.astype(BF16)
    vt_scr[HEAD_DIM:, :] = jnp.ones((BF16_SUBLANES, seq), BF16)
    qt = q_ref[...].T
    kt = k_ref[...].T

    blk_id = lax.broadcasted_iota(jnp.int32, (nblk, seq), 0)
    q_blk = lax.broadcasted_iota(jnp.int32, (nblk, seq), 1) // blk
    gate = jnp.zeros((nblk, seq), F32)
    for n in range(nblk - 1):
        kmean_n = jnp.mean(kt[:, n * blk:(n + 1) * blk], axis=1, keepdims=True)
        g_past = jnp.sum(qt[:, (n + 1) * blk:] * kmean_n, axis=0, keepdims=True)
        g_n = jnp.concatenate([jnp.zeros((1, (n + 1) * blk), F32), g_past], axis=1)
        gate = jnp.where(blk_id == n, g_n, gate)
    rank = jnp.zeros((nblk, seq), jnp.int32)
    for m in range(nblk):
        g_m = gate[m:m + 1, :]
        beats = (m < q_blk) & ((g_m > gate) | ((g_m == gate) & (m < blk_id)))
        rank = rank + beats.astype(jnp.int32)
    keep = jnp.where((blk_id < q_blk) & (rank < MOBA_TOPK), 1.0, 0.0)

    key_i = lax.broadcasted_iota(jnp.int32, (blk, blk), 0)
    qry_i = lax.broadcasted_iota(jnp.int32, (blk, blk), 1)
    causal = key_i <= qry_i

    for j in range(nblk):
        cols = slice(j * blk, (j + 1) * blk)
        qb = (q_ref[cols, :] * q_scale).astype(BF16)
        keep_j = keep[:, cols]
        m_run = None
        for n in range(j + 1):
            st = _dot_nt(kb_scr[n * blk:(n + 1) * blk, :], qb)
            if n == j:
                st = jnp.where(causal, st, NEG_INF)
            else:
                st = jnp.where(keep_j[n:n + 1, :] > 0.5, st, NEG_INF)
            s_scr[n] = st
            m_n = jnp.max(st, axis=0, keepdims=True)
            m_run = m_n if m_run is None else jnp.maximum(m_run, m_n)
        acc = jnp.zeros((HEAD_DIM + BF16_SUBLANES, blk), F32)
        for n in range(j + 1):
            p = jnp.exp2(s_scr[n] - m_run)
            acc = acc + _dot(vt_scr[:, n * blk:(n + 1) * blk], p.astype(BF16))
        o_ref[cols, :] = (acc[:HEAD_DIM, :] / acc[HEAD_DIM:HEAD_DIM + 1, :]).T.astype(BF16)


def _moba_prompt(q, k, v, *, batch, seq, casts=()):
    n_steps = batch * N_HEADS
    spec = pl.BlockSpec((seq, HEAD_DIM), lambda b, h: (b, h))

    def slab_spec(wt):
        rows = wt.shape[0] // n_steps
        assert rows * n_steps == wt.shape[0] and rows % BF16_SUBLANES == 0
        return pl.BlockSpec((rows, wt.shape[1]), lambda b, h: (b * N_HEADS + h, 0))

    slabs = [slab_spec(wt) for wt in casts]
    out = pl.pallas_call(
        _moba_prompt_kernel,
        grid=(batch, N_HEADS),
        in_specs=[spec, spec, spec] + slabs,
        out_specs=[spec] + slabs,
        out_shape=[jax.ShapeDtypeStruct(q.shape, BF16)] + [jax.ShapeDtypeStruct(wt.shape, BF16) for wt in casts],
        scratch_shapes=[pltpu.VMEM((seq, HEAD_DIM), BF16), pltpu.VMEM((HEAD_DIM + BF16_SUBLANES, seq), BF16),
                        pltpu.VMEM((seq // MOBA_BLOCK, MOBA_BLOCK, MOBA_BLOCK), F32)],
        compiler_params=_params("parallel", "parallel"),
        name="moba_prompt",
    )(q, k, v, *casts)
    return out[0], tuple(out[1:])


def _scan_key_means(page_refs, km_ref, group):
    n_blk = len(page_refs) // 2
    page_rows = page_refs[0].shape[0] // N_HEADS

    def page_sum(ref):
        return jnp.sum(ref[...].reshape(page_rows, N_HEADS, HEAD_DIM), axis=0)

    for t in range(n_blk):
        tot = (page_sum(page_refs[2 * t]) + page_sum(page_refs[2 * t + 1])) * (1.0 / MOBA_BLOCK)
        for h in range(N_HEADS):
            km_ref[h, pl.ds(group * n_blk + t, 1), :] = tot[h:h + 1, :]


def _select_kernel(q_ref, km_ref, kn_ref, sel_ref, gate_scr, *, dec_s, n_past):
    dec_b = km_ref.shape[0]
    gate_scr[...] = jnp.full(gate_scr.shape, NEG_INF, F32)
    for b in range(dec_b):
        own_mean = jnp.sum(kn_ref[b * dec_s:(b + 1) * dec_s, :], axis=0, keepdims=True) * (1.0 / MOBA_BLOCK)
        for h in range(N_HEADS):
            cols = slice(h * HEAD_DIM, (h + 1) * HEAD_DIM)
            km = km_ref[b, h]
            for i in range(dec_s):
                col = (b * N_HEADS + h) * dec_s + i
                qi = q_ref[b * dec_s + i:b * dec_s + i + 1, cols]
                gate_scr[0:n_past, col:col + 1] = jnp.sum(km * qi, axis=-1, keepdims=True)
                gate_scr[n_past:n_past + 1, col:col + 1] = jnp.sum(qi * own_mean[:, cols], axis=-1,
                                                                   keepdims=True)
    gate = gate_scr[...]
    blk = lax.broadcasted_iota(jnp.int32, gate.shape, 0)
    gate = jnp.where(blk < n_past, gate, NEG_INF)
    out_row = lax.broadcasted_iota(jnp.int32, sel_ref.shape, 0)
    sel = jnp.zeros(sel_ref.shape, jnp.int32)
    for t in range(MOBA_TOPK):
        best = jnp.max(gate, axis=0, keepdims=True)
        idx = jnp.min(jnp.where(gate == best, blk, gate.shape[0]), axis=0, keepdims=True)
        sel = jnp.where(out_row == t, idx, sel)
        gate = jnp.where(blk == idx, TAKEN, gate)
    sel_ref[...] = sel


def _select_blocks(q2, kmean, kn2, *, dec_s, n_past):
    cols = q2.shape[0] * N_HEADS
    gate_rows = 8 * (-(-(n_past + 1) // 8))
    return pl.pallas_call(
        functools.partial(_select_kernel, dec_s=dec_s, n_past=n_past),
        out_shape=jax.ShapeDtypeStruct((8, cols), jnp.int32),
        scratch_shapes=[pltpu.VMEM((gate_rows, cols), F32)],
        compiler_params=pltpu.CompilerParams(vmem_limit_bytes=VMEM_LIMIT_BYTES),
        name="select_blocks",
    )(q2, kmean, kn2)


def _sample_fetch(step, n_steps, pt_ref, sel_ref, ck_hbm, cv_hbm, kbuf, vbuf, sem, *, dec_s, n_slots):
    n_pages = 2 * n_slots

    def page_copies(step_idx, slot):
        bb, hh = step_idx // N_HEADS, step_idx % N_HEADS
        copies = []
        for c in range(n_pages):
            query, pick = (c // 2) // MOBA_TOPK, (c // 2) % MOBA_TOPK
            blk = sel_ref[pick, step_idx * dec_s + query]
            pg = pt_ref[bb, 2 * blk + c % 2]
            src = (pg, slice(None), hh, slice(None))
            copies.append(pltpu.make_async_copy(ck_hbm.at[src], kbuf.at[slot, c], sem.at[0, slot]))
            copies.append(pltpu.make_async_copy(cv_hbm.at[src], vbuf.at[slot, c], sem.at[1, slot]))
        return copies

    @pl.when(step == 0)
    def _():
        for cp in page_copies(step, 0):
            cp.start()

    @pl.when(step + 1 < n_steps)
    def _():
        for cp in page_copies(step + 1, (step + 1) % 2):
            cp.start()

    slot = step % 2
    for cp in page_copies(step, slot):
        cp.wait()
    return slot


def _sample_attend(slot, q_ref, kn_ref, vn_ref, o_ref, kbuf, vbuf, *, dec_s, n_slots):
    n_pages = 2 * n_slots
    page = kbuf.shape[2]
    k_refs = [kbuf.at[slot, c] for c in range(n_pages)]
    v_refs = [vbuf.at[slot, c] for c in range(n_pages)]
    scale = HEAD_DIM ** -0.5
    rows = q_ref.shape[0]
    qb = q_ref[...].astype(BF16)
    row = lax.broadcasted_iota(jnp.int32, (rows, page), 0)
    s_list = []
    m_run = None
    for c in range(n_pages):
        s = _dot_nt(qb, k_refs[c][...].astype(BF16)) * scale
        s = jnp.where(row == c // (2 * MOBA_TOPK), s, NEG_INF)
        s_list.append(s)
        m_c = jnp.max(s, axis=-1, keepdims=True)
        m_run = m_c if m_run is None else jnp.maximum(m_run, m_c)
    s_own = _dot_nt(qb, kn_ref[...].astype(BF16)) * scale
    r_o = lax.broadcasted_iota(jnp.int32, (rows, rows), 0)
    c_o = lax.broadcasted_iota(jnp.int32, (rows, rows), 1)
    s_own = jnp.where((c_o <= r_o) & (c_o < dec_s), s_own, NEG_INF)
    m_run = jnp.maximum(m_run, jnp.max(s_own, axis=-1, keepdims=True))
    p_own = jnp.exp(s_own - m_run)
    l_run = jnp.sum(p_own, axis=-1, keepdims=True)
    acc = _dot(p_own.astype(BF16), vn_ref[...].astype(BF16))
    for c in range(n_pages):
        p = jnp.exp(s_list[c] - m_run)
        l_run = l_run + jnp.sum(p, axis=-1, keepdims=True)
        acc = acc + _dot(p.astype(BF16), v_refs[c][...].astype(BF16))
    o_ref[...] = (acc / l_run).astype(BF16)


def _gated_mix_kernel(a_ref, b_ref, h_ref, wa_ref, wb_ref, wga_ref, wgb_ref, o_ref):
    h = h_ref[...]
    ga = jax.nn.sigmoid(_dot(h, wga_ref[...]))
    gb = jax.nn.sigmoid(_dot(h, wgb_ref[...]))
    a = _dot(a_ref[...], wa_ref[...])
    b = _dot(b_ref[...], wb_ref[...])
    o_ref[...] = (ga * a + gb * b).astype(BF16)


def _gated_mix(attn, gm, h, w_a, w_b, w_gate, *, tm, tn):
    m, d = h.shape
    n_col = d // tn
    return pl.pallas_call(
        _gated_mix_kernel,
        grid=(n_col, m // tm),
        in_specs=[pl.BlockSpec((tm, attn.shape[1]), lambda j, i: (i, 0)),
                  pl.BlockSpec((tm, gm.shape[1]), lambda j, i: (i, 0)),
                  pl.BlockSpec((tm, d), lambda j, i: (i, 0)),
                  pl.BlockSpec((w_a.shape[0], tn), lambda j, i: (0, j)),
                  pl.BlockSpec((w_b.shape[0], tn), lambda j, i: (0, j)),
                  pl.BlockSpec((d, tn), lambda j, i: (0, j)),
                  pl.BlockSpec((d, tn), lambda j, i: (0, j + n_col))],
        out_specs=pl.BlockSpec((tm, tn), lambda j, i: (i, j)),
        out_shape=jax.ShapeDtypeStruct((m, d), BF16),
        compiler_params=_params("parallel", "parallel"),
        name="gated_mix",
    )(attn, gm, h, w_a, w_b, w_gate, w_gate)


def _row_chunks(tm, rc):
    rc = min(tm, rc)
    return [slice(r, r + rc) for r in range(0, tm, rc)]


def _mix_out_kernel(mix_ref, x_ref, wo_ref, gpost_ref, gpre_ref, x1_ref, xn_ref):
    for rows in _row_chunks(x_ref.shape[0], NORM_ROW_CHUNK):
        mix = _dot(mix_ref[rows, :], wo_ref[...])
        x1 = x_ref[rows, :] + _rms(mix, gpost_ref[...])
        x1_ref[rows, :] = x1
        xn_ref[rows, :] = _rms(x1, gpre_ref[...]).astype(BF16)


def _mix_out(mixin, x, w_o, g_post, g_pre_ffn, *, tm):
    m, d = x.shape
    row = pl.BlockSpec((tm, d), lambda i: (i, 0))
    vec = pl.BlockSpec((1, d), lambda i: (0, 0))
    return pl.pallas_call(
        _mix_out_kernel,
        grid=(m // tm,),
        in_specs=[row, row, pl.BlockSpec((d, d), lambda i: (0, 0), pipeline_mode=pl.Buffered(1)), vec, vec],
        out_specs=[row, row],
        out_shape=[jax.ShapeDtypeStruct((m, d), F32), jax.ShapeDtypeStruct((m, d), BF16)],
        compiler_params=_params("parallel"),
        name="mix_out",
    )(mixin, x, w_o, g_post, g_pre_ffn)


def _ffn_in_kernel(x_ref, wa_ref, wg_ref, o_ref):
    x = x_ref[...]
    a = _dot(x, wa_ref[...])
    g = _dot(x, wg_ref[...])
    o_ref[...] = (jax.nn.silu(a) * g).astype(BF16)


def _ffn_in_scan_kernel(pt_ref, x_ref, wa_ref, wg_ref, *refs, groups):
    page_refs, o_ref, km_ref = refs[:-2], refs[-2], refs[-1]
    step = pl.program_id(0) * pl.num_programs(1) + pl.program_id(1)
    _scan_key_means(page_refs, km_ref, step % groups)
    _ffn_in_kernel(x_ref, wa_ref, wg_ref, o_ref)


def _ffn_in(xn, w_ffn_in, *, tm, tn, scan=None):
    m, d = xn.shape
    hidden = w_ffn_in.shape[1] // 2
    n_col, n_row = hidden // tn, m // tm
    in_specs = [pl.BlockSpec((tm, d), lambda j, i, *_: (i, 0)),
                pl.BlockSpec((d, tn), lambda j, i, *_: (0, j)),
                pl.BlockSpec((d, tn), lambda j, i, *_: (0, j + n_col))]
    out_spec = pl.BlockSpec((tm, tn), lambda j, i, *_: (i, j))
    out_shape = jax.ShapeDtypeStruct((m, hidden), BF16)
    if scan is None:
        return pl.pallas_call(
            _ffn_in_kernel, grid=(n_col, n_row), in_specs=in_specs, out_specs=out_spec, out_shape=out_shape,
            compiler_params=_params("parallel", "parallel"), name="ffn_in",
        )(xn, w_ffn_in, w_ffn_in)

    cache_k2, page_table = scan
    dec_b, n_pages = page_table.shape
    n_steps = n_col * n_row
    pps = dec_b * n_pages // n_steps
    groups = n_pages // pps
    assert pps * n_steps == dec_b * n_pages and groups * pps == n_pages and pps % 2 == 0
    n_past = n_pages // 2

    def page_spec(c):
        def idx(j, i, pt):
            step = j * n_row + i
            return (pt[step // groups, (step % groups) * pps + c], 0, 0)
        return pl.BlockSpec((None,) + cache_k2.shape[1:], idx)

    grid_spec = pltpu.PrefetchScalarGridSpec(
        num_scalar_prefetch=1,
        grid=(n_col, n_row),
        in_specs=in_specs + [page_spec(c) for c in range(pps)],
        out_specs=[out_spec,
                   pl.BlockSpec((None, N_HEADS, n_past, HEAD_DIM),
                                lambda j, i, pt: ((j * n_row + i) // groups, 0, 0, 0))],
    )
    return pl.pallas_call(
        functools.partial(_ffn_in_scan_kernel, groups=groups),
        grid_spec=grid_spec,
        out_shape=[out_shape, jax.ShapeDtypeStruct((dec_b, N_HEADS, n_past, HEAD_DIM), F32)],
        compiler_params=_params("arbitrary", "arbitrary"),
        name="ffn_in_scan",
    )(page_table, xn, w_ffn_in, w_ffn_in, *([cache_k2] * pps))


def _ffn_out_ple_kernel(hm_ref, w_ref, x1_ref, gpost_ref, gple_ref, p_ref, wpg_ref, wp_ref, y_ref, acc_scr,
                        side_work=lambda: None):
    k = pl.program_id(1)
    last = pl.num_programs(1) - 1

    @pl.when(k == 0)
    def _():
        side_work()
        acc_scr[...] = _dot(hm_ref[...], w_ref[...])

    @pl.when((k > 0) & (k < last))
    def _():
        side_work()
        acc_scr[...] += _dot(hm_ref[...], w_ref[...])

    @pl.when(k == last)
    def _():
        side_work()
        for rows in _row_chunks(x1_ref.shape[0], NORM_ROW_CHUNK):
            f = acc_scr[rows, :] + _dot(hm_ref[rows, :], w_ref[...])
            x2 = x1_ref[rows, :] + _rms(f, gpost_ref[...])
            xg = _rms(x2, gple_ref[...]).astype(BF16)
            gate = jax.nn.sigmoid(_dot(xg, wpg_ref[...]))
            y_ref[rows, :] = x2 + _dot(p_ref[rows, :].astype(BF16), wp_ref[...]) * gate


def _ffn_out_ple_attend_kernel(pt_ref, sel_ref, hm_ref, w_ref, x1_ref, gpost_ref, gple_ref, p_ref, wpg_ref,
                               wp_ref, q_ref, kn_ref, vn_ref, ck_hbm, cv_hbm, y_ref, a_ref,
                               acc_scr, kbuf, vbuf, sem, *, dec_s, n_slots):
    step = pl.program_id(0) * pl.num_programs(1) + pl.program_id(1)
    n_steps = pl.num_programs(0) * pl.num_programs(1)
    slot = _sample_fetch(step, n_steps, pt_ref, sel_ref, ck_hbm, cv_hbm, kbuf, vbuf, sem,
                         dec_s=dec_s, n_slots=n_slots)
    attend = functools.partial(_sample_attend, slot, q_ref, kn_ref, vn_ref, a_ref, kbuf, vbuf,
                               dec_s=dec_s, n_slots=n_slots)
    _ffn_out_ple_kernel(hm_ref, w_ref, x1_ref, gpost_ref, gple_ref, p_ref, wpg_ref, wp_ref, y_ref, acc_scr,
                        side_work=attend)


def _ffn_out_ple(hmid, w_ffn_out, x1, g_post_ffn, g_ple, p, w_ple_gate, w_ple, *, tm, tk, attend=None):
    m, d = x1.shape
    hidden = hmid.shape[1]
    n_row, n_k = m // tm, hidden // tk
    assert n_k >= 2
    row = pl.BlockSpec((tm, d), lambda i, k, *_: (i, 0))
    vec = pl.BlockSpec((1, d), lambda i, k, *_: (0, 0))
    const = lambda shape: pl.BlockSpec(shape, lambda i, k, *_: (0, 0), pipeline_mode=pl.Buffered(1))
    in_specs = [pl.BlockSpec((tm, tk), lambda i, k, *_: (i, k)),
                pl.BlockSpec((tk, d), lambda i, k, *_: (k, 0)),
                row, vec, vec,
                pl.BlockSpec((tm, p.shape[1]), lambda i, k, *_: (i, 0)),
                const(w_ple_gate.shape), const(w_ple.shape)]
    args = (hmid, w_ffn_out, x1, g_post_ffn, g_ple, p, w_ple_gate, w_ple)
    y_shape = jax.ShapeDtypeStruct((m, d), F32)
    acc = pltpu.VMEM((tm, d), F32)
    if attend is None:
        return pl.pallas_call(
            _ffn_out_ple_kernel, grid=(n_row, n_k), in_specs=in_specs, out_specs=row, out_shape=y_shape,
            scratch_shapes=[acc], compiler_params=_params("parallel", "arbitrary"), name="ffn_out_ple",
        )(*args)

    qkv, cache_k4, cache_v4, page_table, sel, dec_s = attend
    _, dec_b, rows, width = qkv.shape
    page = cache_k4.shape[1]
    n_slots = dec_s * MOBA_TOPK
    assert n_row * n_k == dec_b * N_HEADS
    pair_idx = lambda i, k: ((i * n_k + k) // N_HEADS, 0, (i * n_k + k) % N_HEADS)
    pair_spec = pl.BlockSpec((None, rows, HEAD_DIM), lambda i, k, *_: pair_idx(i, k))
    part_spec = lambda part: pl.BlockSpec((None, None, rows, HEAD_DIM), lambda i, k, *_: (part,) + pair_idx(i, k))
    hbm_spec = pl.BlockSpec(memory_space=pl.ANY)
    grid_spec = pltpu.PrefetchScalarGridSpec(
        num_scalar_prefetch=2,
        grid=(n_row, n_k),
        in_specs=in_specs + [part_spec(0), part_spec(1), part_spec(2), hbm_spec, hbm_spec],
        out_specs=[row, pair_spec],
        scratch_shapes=[acc,
                        pltpu.VMEM((2, 2 * n_slots, page, HEAD_DIM), F32),
                        pltpu.VMEM((2, 2 * n_slots, page, HEAD_DIM), F32),
                        pltpu.SemaphoreType.DMA((2, 2))],
    )
    return pl.pallas_call(
        functools.partial(_ffn_out_ple_attend_kernel, dec_s=dec_s, n_slots=n_slots),
        grid_spec=grid_spec,
        out_shape=[y_shape, jax.ShapeDtypeStruct((dec_b, rows, width), BF16)],
        compiler_params=_params("arbitrary", "arbitrary"),
        name="ffn_out_ple_attend",
    )(page_table, sel, *args, qkv, qkv, qkv, cache_k4, cache_v4)


def _tail_front(x, h, attn, gm, w, *, tm, scan=None):
    m, d = x.shape
    hidden = w["w_ffn_out"].shape[0]
    mixin = _gated_mix(attn, gm, h, w["w_a_out"], w["w_b_out"], w["w_gate"], tm=min(m, 1024), tn=512)
    x1, xn = _mix_out(mixin, x, w["w_o"], w["g_post_mix"], w["g_pre_ffn"], tm=tm)
    if scan is None:
        return x1, _ffn_in(xn, w["w_ffn_in"], tm=tm, tn=hidden // 4), None
    hmid, kmean = _ffn_in(xn, w["w_ffn_in"], tm=tm, tn=hidden // 4, scan=scan)
    return x1, hmid, kmean


def _tail_back(x1, hmid, p, w, *, tm, attend=None):
    return _ffn_out_ple(hmid, w["w_ffn_out"], x1, w["g_post_ffn"], w["g_ple"], p, w["w_ple_gate"], w["w_ple"],
                        tm=tm, tk=w["w_ffn_out"].shape[0] // 4, attend=attend)


def _rope_tables(pos):
    lane = jnp.arange(HEAD_DIM, dtype=jnp.int32)
    freqs = jnp.power(jnp.float32(ROPE_THETA), -2.0 * (lane % ROPE_HALF).astype(F32) / ROPE_DIM)
    ang = pos.astype(F32)[:, None] * freqs[None, :]
    cos, sin = jnp.cos(ang), jnp.sin(ang)
    c = jnp.where(lane < ROPE_DIM, cos, 1.0)
    sa = jnp.where(lane < ROPE_HALF, -sin, 0.0)
    sb = jnp.where((lane >= ROPE_HALF) & (lane < ROPE_DIM), sin, 0.0)
    return c, sa, sb


def kernel(x_prompt, x_sample, cache_k, cache_v, page_table, p_prompt, p_sample, g_pre_mix, w_in, g_vnorm, w_spatial, b_spatial, w_a_out, w_b_out, w_gate, w_o, g_post_mix, g_pre_ffn, w_ffn_in, w_ffn_out, g_post_ffn, g_ple, w_ple_gate, w_ple):
    batch, seq, d = x_prompt.shape
    dec_b, dec_s, _ = x_sample.shape
    depth = w_in.shape[0]
    page = cache_k.shape[2]
    past_len = page_table.shape[1] * page
    n_past = past_len // MOBA_BLOCK
    assert depth == 1 and seq % MOBA_BLOCK == 0 and past_len % MOBA_BLOCK == 0 and MOBA_BLOCK == 2 * page
    assert dec_s <= GMLP_CHUNK and n_past >= MOBA_TOPK
    l = 0

    w = {
        "w_ple": w_ple[l].astype(BF16),
        "g_post_mix": g_post_mix[l][None], "g_pre_ffn": g_pre_ffn[l][None],
        "g_post_ffn": g_post_ffn[l][None], "g_ple": g_ple[l][None],
    }
    tail_weights = {"w_a_out": w_a_out[l], "w_b_out": w_b_out[l], "w_gate": w_gate[l], "w_o": w_o[l],
                    "w_ffn_in": w_ffn_in[l], "w_ffn_out": w_ffn_out[l], "w_ple_gate": w_ple_gate[l]}
    w_in_b = w_in[l].astype(BF16)
    g_pre = g_pre_mix[l][None]
    g_vn = g_vnorm[l][None]

    w_tril = jnp.tril(w_spatial[l])
    wsp_p = w_tril.astype(BF16)
    bsp_p = b_spatial[l].T
    eye_b = jnp.eye(dec_b, dtype=F32)
    wsp_s = jnp.einsum("ab,gts->gatbs", eye_b, w_tril[:, :dec_s, :dec_s]).reshape(
        GMLP_GROUPS, dec_b * dec_s, dec_b * dec_s).astype(BF16)
    bsp_s = jnp.tile(b_spatial[l][:, :dec_s].T, (dec_b, 1))

    mp = batch * seq
    xp = x_prompt.reshape(mp, d)
    cp, sap, sbp = _rope_tables(jnp.arange(seq, dtype=jnp.int32))
    hp, qp, kp, vp, gmp = _mixer(xp, g_pre, w_in_b, cp, sap, sbp, g_vn, wsp_p, bsp_p,
                                 tm=MIXER_ROWS, chunk=GMLP_CHUNK, with_vn=False)
    ms = dec_b * dec_s
    xs = x_sample.reshape(ms, d)
    pos_s = past_len + jnp.arange(dec_s, dtype=jnp.int32)
    cs, sas, sbs = _rope_tables(jnp.tile(pos_s, dec_b))
    hs, qs, ks, vs, gms, vns = _mixer(xs, g_pre, w_in_b, cs, sas, sbs, g_vn, wsp_s, bsp_s,
                                      tm=ms, chunk=ms, with_vn=True)

    ap, cast = _moba_prompt(qp, kp, vp, batch=batch, seq=seq, casts=tuple(tail_weights.values()))
    w.update(zip(tail_weights.keys(), cast))
    n_pool = depth * cache_k.shape[1]
    cache_k2 = cache_k.reshape(n_pool, page * N_HEADS, HEAD_DIM)
    x1p, hmid_p, kmean = _tail_front(xp, hp, ap, gmp, w, tm=512, scan=(cache_k2, page_table))
    sel = _select_blocks(qs, kmean, ks, dec_s=dec_s, n_past=n_past)
    qkv = jnp.pad(jnp.stack([qs, ks, vs]).reshape(3, dec_b, dec_s, ATTN_WIDTH),
                  ((0, 0), (0, 0), (0, SAMPLE_ROWS - dec_s), (0, 0)))
    attend = (qkv, cache_k.reshape(n_pool, page, N_HEADS, HEAD_DIM),
              cache_v.reshape(n_pool, page, N_HEADS, HEAD_DIM), page_table, sel, dec_s)
    yp, a_s = _tail_back(x1p, hmid_p, p_prompt[l].reshape(mp, -1), w, tm=512, attend=attend)

    a_s = a_s[:, :dec_s].reshape(ms, ATTN_WIDTH)
    x1s, hmid_s, _ = _tail_front(xs, hs, a_s, gms, w, tm=ms)
    ys = _tail_back(x1s, hmid_s, p_sample[l].reshape(ms, -1), w, tm=ms)

    return (yp.reshape(batch, seq, d), ys.reshape(dec_b, dec_s, d),
            kp.reshape(1, batch, seq, N_HEADS, HEAD_DIM), vp.reshape(1, batch, seq, N_HEADS, HEAD_DIM),
            ks.reshape(1, dec_b, dec_s, N_HEADS, HEAD_DIM), vs.reshape(1, dec_b, dec_s, N_HEADS, HEAD_DIM),
            vns.reshape(1, dec_b, dec_s, GMLP_WIDTH))
```

```python
import functools

import jax
import jax.numpy as jnp
from jax import lax
from jax.experimental import pallas as pl
from jax.experimental.pallas import tpu as pltpu

F32 = jnp.float32
BF16 = jnp.bfloat16

N_HEADS = 8
HEAD_DIM = 128
ATTN_WIDTH = N_HEADS * HEAD_DIM
MOBA_BLOCK = 256
MOBA_TOPK = 3
ROPE_THETA = 500000.0
ROPE_DIM = HEAD_DIM // 4
ROPE_HALF = ROPE_DIM // 2
GMLP_GROUPS = 8
GMLP_CHUNK = 128
GMLP_WIDTH = 1024
GMLP_GROUP_DIM = GMLP_WIDTH // GMLP_GROUPS
NORM_EPS = 1e-6
NEG_INF = -1e30
LOG2_E = 1.4426950408889634
TAKEN = -3e38
MIXER_ROWS = 256
NORM_ROW_CHUNK = 256
BF16_SUBLANES = 16

VMEM_LIMIT_BYTES = 56 * 1024 * 1024


def _params(*semantics):
    return pltpu.CompilerParams(dimension_semantics=semantics, vmem_limit_bytes=VMEM_LIMIT_BYTES)


def _rms(x, g):
    return x * lax.rsqrt(jnp.mean(x * x, axis=-1, keepdims=True) + NORM_EPS) * g


def _dot(a, b):
    return jnp.dot(a, b, preferred_element_type=F32)


def _dot_nt(a, b):
    return lax.dot_general(a, b, (((1,), (1,)), ((), ())), preferred_element_type=F32)


def _mixer_kernel(x_ref, g_ref, w_ref, cos_ref, sa_ref, sb_ref, gvn_ref, wsp_ref, bsp_ref,
                  h_out, q_out, k_out, v_out, gm_out, *rest, chunk, with_vn):
    tm = x_ref.shape[0]
    seg = ATTN_WIDTH
    hb = _rms(x_ref[...], g_ref[...]).astype(BF16)
    h_out[...] = hb

    def proj(s):
        return _dot(hb, w_ref[:, s * seg:(s + 1) * seg])

    def rope_to(out_ref, z):
        c, sa, sb = cos_ref[...], sa_ref[...], sb_ref[...]
        for hd in range(N_HEADS):
            sl = slice(hd * HEAD_DIM, (hd + 1) * HEAD_DIM)
            zs = z[:, sl]
            out_ref[:, sl] = (zs * c + pltpu.roll(zs, HEAD_DIM - ROPE_HALF, 1) * sa
                              + pltpu.roll(zs, ROPE_HALF, 1) * sb)

    rope_to(q_out, proj(0))
    rope_to(k_out, proj(1))
    v_out[...] = proj(2)
    u = jax.nn.gelu(proj(3))
    vg = jax.nn.gelu(proj(4))
    xc = vg - jnp.mean(vg, axis=-1, keepdims=True)
    vn = xc * lax.rsqrt(jnp.mean(xc * xc, axis=-1, keepdims=True) + NORM_EPS) * gvn_ref[...]
    if with_vn:
        rest[0][...] = vn
    vnb = vn.astype(BF16)
    for c in range(tm // chunk):
        rows = slice(c * chunk, (c + 1) * chunk)
        for g in range(GMLP_GROUPS):
            cols = slice(g * GMLP_GROUP_DIM, (g + 1) * GMLP_GROUP_DIM)
            s = _dot(wsp_ref[g], vnb[rows, cols]) + bsp_ref[:, g:g + 1]
            gm_out[rows, cols] = (u[rows, cols] * s).astype(BF16)


def _mixer(x, g_pre, w_in, cos_t, sa_t, sb_t, g_vn, wsp, bsp, *, tm, chunk, with_vn):
    m, d = x.shape
    assert w_in.shape[1] == 3 * ATTN_WIDTH + 2 * GMLP_WIDTH and ATTN_WIDTH == GMLP_WIDTH
    t_blocks = cos_t.shape[0] // tm
    tab_spec = pl.BlockSpec((tm, HEAD_DIM), lambda i: (i % t_blocks, 0))
    row_spec = lambda w: pl.BlockSpec((tm, w), lambda i: (i, 0))
    const = lambda shape: pl.BlockSpec(shape, lambda i: (0,) * len(shape))
    out_shape = [jax.ShapeDtypeStruct((m, d), BF16),
                 jax.ShapeDtypeStruct((m, ATTN_WIDTH), F32),
                 jax.ShapeDtypeStruct((m, ATTN_WIDTH), F32),
                 jax.ShapeDtypeStruct((m, ATTN_WIDTH), F32),
                 jax.ShapeDtypeStruct((m, GMLP_WIDTH), BF16)]
    out_specs = [row_spec(d), row_spec(ATTN_WIDTH), row_spec(ATTN_WIDTH), row_spec(ATTN_WIDTH),
                 row_spec(GMLP_WIDTH)]
    if with_vn:
        out_shape.append(jax.ShapeDtypeStruct((m, GMLP_WIDTH), F32))
        out_specs.append(row_spec(GMLP_WIDTH))
    return pl.pallas_call(
        functools.partial(_mixer_kernel, chunk=chunk, with_vn=with_vn),
        grid=(m // tm,),
        in_specs=[row_spec(d),
                  const((1, d)),
                  pl.BlockSpec(w_in.shape, lambda i: (0, 0), pipeline_mode=pl.Buffered(1)),
                  tab_spec, tab_spec, tab_spec,
                  const((1, GMLP_WIDTH)),
                  const((GMLP_GROUPS, chunk, chunk)),
                  const((chunk, GMLP_GROUPS))],
        out_specs=out_specs,
        out_shape=out_shape,
        compiler_params=_params("parallel"),
        name="mixer",
    )(x, g_pre, w_in, cos_t, sa_t, sb_t, g_vn, wsp, bsp)


def _moba_prompt_kernel(q_ref, k_ref, v_ref, *refs):
    n_cast = (len(refs) - 4) // 2
    o_ref = refs[n_cast]
    kb_scr, vt_scr, s_scr = refs[-3:]
    for src_ref, dst_ref in zip(refs[:n_cast], refs[n_cast + 1:2 * n_cast + 1]):
        dst_ref[...] = src_ref[...].astype(BF16)
    seq = q_ref.shape[0]
    nblk = seq // MOBA_BLOCK
    blk = MOBA_BLOCK
    q_scale = HEAD_DIM ** -0.5 * LOG2_E
    kb_scr[...] = k_ref[...].astype(BF16)
    vt_scr[:HEAD_DIM, :] = v_ref[...].T.astype(BF16)
    vt_scr[HEAD_DIM:, :] = jnp.ones((BF16_SUBLANES, seq), BF16)
    qt = q_ref[...].T
    kt = k_ref[...].T

    blk_id = lax.broadcasted_iota(jnp.int32, (nblk, seq), 0)
    q_blk = lax.broadcasted_iota(jnp.int32, (nblk, seq), 1) // blk
    gate = jnp.zeros((nblk, seq), F32)
    for n in range(nblk - 1):
        kmean_n = jnp.mean(kt[:, n * blk:(n + 1) * blk], axis=1, keepdims=True)
        g_past = jnp.sum(qt[:, (n + 1) * blk:] * kmean_n, axis=0, keepdims=True)
        g_n = jnp.concatenate([jnp.zeros((1, (n + 1) * blk), F32), g_past], axis=1)
        gate = jnp.where(blk_id == n, g_n, gate)
    rank = jnp.zeros((nblk, seq), jnp.int32)
    for m in range(nblk):
        g_m = gate[m:m + 1, :]
        beats = (m < q_blk) & ((g_m > gate) | ((g_m == gate) & (m < blk_id)))
        rank = rank + beats.astype(jnp.int32)
    keep = jnp.where((blk_id < q_blk) & (rank < MOBA_TOPK), 1.0, 0.0)

    key_i = lax.broadcasted_iota(jnp.int32, (blk, blk), 0)
    qry_i = lax.broadcasted_iota(jnp.int32, (blk, blk), 1)
    causal = key_i <= qry_i

    for j in range(nblk):
        cols = slice(j * blk, (j + 1) * blk)
        qb = (q_ref[cols, :] * q_scale).astype(BF16)
        keep_j = keep[:, cols]
        m_run = None
        for n in range(j + 1):
            st = _dot_nt(kb_scr[n * blk:(n + 1) * blk, :], qb)
            if n == j:
                st = jnp.where(causal, st, NEG_INF)
            else:
                st = jnp.where(keep_j[n:n + 1, :] > 0.5, st, NEG_INF)
            s_scr[n] = st
            m_n = jnp.max(st, axis=0, keepdims=True)
            m_run = m_n if m_run is None else jnp.maximum(m_run, m_n)
        acc = jnp.zeros((HEAD_DIM + BF16_SUBLANES, blk), F32)
        for n in range(j + 1):
            p = jnp.exp2(s_scr[n] - m_run)
            acc = acc + _dot(vt_scr[:, n * blk:(n + 1) * blk], p.astype(BF16))
        o_ref[cols, :] = (acc[:HEAD_DIM, :] / acc[HEAD_DIM:HEAD_DIM + 1, :]).T.astype(BF16)


def _moba_prompt(q, k, v, *, batch, seq, casts=()):
    n_steps = batch * N_HEADS
    spec = pl.BlockSpec((seq, HEAD_DIM), lambda b, h: (b, h))

    def slab_spec(wt):
        rows = wt.shape[0] // n_steps
        assert rows * n_steps == wt.shape[0] and rows % BF16_SUBLANES == 0
        return pl.BlockSpec((rows, wt.shape[1]), lambda b, h: (b * N_HEADS + h, 0))

    slabs = [slab_spec(wt) for wt in casts]
    out = pl.pallas_call(
        _moba_prompt_kernel,
        grid=(batch, N_HEADS),
        in_specs=[spec, spec, spec] + slabs,
        out_specs=[spec] + slabs,
        out_shape=[jax.ShapeDtypeStruct(q.shape, BF16)] + [jax.ShapeDtypeStruct(wt.shape, BF16) for wt in casts],
        scratch_shapes=[pltpu.VMEM((seq, HEAD_DIM), BF16), pltpu.VMEM((HEAD_DIM + BF16_SUBLANES, seq), BF16),
                        pltpu.VMEM((seq // MOBA_BLOCK, MOBA_BLOCK, MOBA_BLOCK), F32)],
        compiler_params=_params("parallel", "parallel"),
        name="moba_prompt",
    )(q, k, v, *casts)
    return out[0], tuple(out[1:])


def _scan_key_means(page_refs, km_ref, group):
    n_blk = len(page_refs) // 2
    page_rows = page_refs[0].shape[0] // N_HEADS

    def page_sum(ref):
        return jnp.sum(ref[...].reshape(page_rows, N_HEADS, HEAD_DIM), axis=0)

    for t in range(n_blk):
        tot = (page_sum(page_refs[2 * t]) + page_sum(page_refs[2 * t + 1])) * (1.0 / MOBA_BLOCK)
        for h in range(N_HEADS):
            km_ref[h, pl.ds(group * n_blk + t, 1), :] = tot[h:h + 1, :]


def _select_kernel(q_ref, km_ref, kn_ref, sel_ref, gate_scr, *, dec_s, n_past):
    dec_b = km_ref.shape[0]
    gate_scr[...] = jnp.full(gate_scr.shape, NEG_INF, F32)
    for b in range(dec_b):
        own_mean = jnp.sum(kn_ref[b * dec_s:(b + 1) * dec_s, :], axis=0, keepdims=True) * (1.0 / MOBA_BLOCK)
        for h in range(N_HEADS):
            cols = slice(h * HEAD_DIM, (h + 1) * HEAD_DIM)
            km = km_ref[b, h]
            for i in range(dec_s):
                col = (b * N_HEADS + h) * dec_s + i
                qi = q_ref[b * dec_s + i:b * dec_s + i + 1, cols]
                gate_scr[0:n_past, col:col + 1] = jnp.sum(km * qi, axis=-1, keepdims=True)
                gate_scr[n_past:n_past + 1, col:col + 1] = jnp.sum(qi * own_mean[:, cols], axis=-1,
                                                                   keepdims=True)
    gate = gate_scr[...]
    blk = lax.broadcasted_iota(jnp.int32, gate.shape, 0)
    gate = jnp.where(blk < n_past, gate, NEG_INF)
    out_row = lax.broadcasted_iota(jnp.int32, sel_ref.shape, 0)
    sel = jnp.zeros(sel_ref.shape, jnp.int32)
    for t in range(MOBA_TOPK):
        best = jnp.max(gate, axis=0, keepdims=True)
        idx = jnp.min(jnp.where(gate == best, blk, gate.shape[0]), axis=0, keepdims=True)
        sel = jnp.where(out_row == t, idx, sel)
        gate = jnp.where(blk == idx, TAKEN, gate)
    sel_ref[...] = sel


def _select_blocks(q2, kmean, kn2, *, dec_s, n_past):
    cols = q2.shape[0] * N_HEADS
    gate_rows = 8 * (-(-(n_past + 1) // 8))
    return pl.pallas_call(
        functools.partial(_select_kernel, dec_s=dec_s, n_past=n_past),
        out_shape=jax.ShapeDtypeStruct((8, cols), jnp.int32),
        scratch_shapes=[pltpu.VMEM((gate_rows, cols), F32)],
        compiler_params=pltpu.CompilerParams(vmem_limit_bytes=VMEM_LIMIT_BYTES),
        name="select_blocks",
    )(q2, kmean, kn2)


def _sample_fetch(step, n_steps, pt_ref, sel_ref, ck_hbm, cv_hbm, kbuf, vbuf, sem, *, dec_s, n_slots,
                  maybe_first, maybe_last):
    n_pages = 2 * n_slots

    def page_copies(step_idx, slot):
        bb, hh = step_idx // N_HEADS, step_idx % N_HEADS
        copies = []
        for c in range(n_pages):
            query, pick = (c // 2) // MOBA_TOPK, (c // 2) % MOBA_TOPK
            blk = sel_ref[pick, step_idx * dec_s + query]
            pg = pt_ref[bb, 2 * blk + c % 2]
            src = (pg, slice(None), hh, slice(None))
            copies.append(pltpu.make_async_copy(ck_hbm.at[src], kbuf.at[slot, c], sem.at[0, slot]))
            copies.append(pltpu.make_async_copy(cv_hbm.at[src], vbuf.at[slot, c], sem.at[1, slot]))
        return copies

    def start_next():
        for cp in page_copies(step + 1, (step + 1) % 2):
            cp.start()

    if maybe_first:
        @pl.when(step == 0)
        def _():
            for cp in page_copies(step, 0):
                cp.start()

    if maybe_last:
        pl.when(step + 1 < n_steps)(start_next)
    else:
        start_next()

    slot = step % 2
    for cp in page_copies(step, slot):
        cp.wait()
    return slot


def _sample_attend(slot, q_ref, kn_ref, vn_ref, o_ref, kbuf, vbuf, *, dec_s, n_slots):
    pages_per_query = 2 * MOBA_TOPK
    assert n_slots * 2 == dec_s * pages_per_query
    scale = HEAD_DIM ** -0.5
    kn, vn = kn_ref[...], vn_ref[...]
    new_row = lax.broadcasted_iota(jnp.int32, (dec_s, 1), 0)
    for i in range(dec_s):
        qi = q_ref[i:i + 1, :]
        pages = [i * pages_per_query + c for c in range(pages_per_query)]
        s_pages = [jnp.sum(kbuf[slot, c] * qi, axis=-1, keepdims=True) * scale for c in pages]
        s_own = jnp.where(new_row <= i, jnp.sum(kn * qi, axis=-1, keepdims=True) * scale, NEG_INF)
        m = jnp.max(s_own, axis=0, keepdims=True)
        for s in s_pages:
            m = jnp.maximum(m, jnp.max(s, axis=0, keepdims=True))
        p_own = jnp.exp(s_own - m)
        l = jnp.sum(p_own, axis=0, keepdims=True)
        acc = jnp.sum(p_own * vn, axis=0, keepdims=True)
        for c, s in zip(pages, s_pages):
            p = jnp.exp(s - m)
            l = l + jnp.sum(p, axis=0, keepdims=True)
            acc = acc + jnp.sum(p * vbuf[slot, c], axis=0, keepdims=True)
        o_ref[i:i + 1, :] = acc / l


def _gated_mix_kernel(a_ref, b_ref, h_ref, wa_ref, wb_ref, wga_ref, wgb_ref, o_ref):
    h = h_ref[...]
    ga = jax.nn.sigmoid(_dot(h, wga_ref[...]))
    gb = jax.nn.sigmoid(_dot(h, wgb_ref[...]))
    a = _dot(a_ref[...], wa_ref[...])
    b = _dot(b_ref[...], wb_ref[...])
    o_ref[...] = (ga * a + gb * b).astype(BF16)


def _gated_mix(attn, gm, h, w_a, w_b, w_gate, *, tm, tn):
    m, d = h.shape
    n_col = d // tn
    return pl.pallas_call(
        _gated_mix_kernel,
        grid=(n_col, m // tm),
        in_specs=[pl.BlockSpec((tm, attn.shape[1]), lambda j, i: (i, 0)),
                  pl.BlockSpec((tm, gm.shape[1]), lambda j, i: (i, 0)),
                  pl.BlockSpec((tm, d), lambda j, i: (i, 0)),
                  pl.BlockSpec((w_a.shape[0], tn), lambda j, i: (0, j)),
                  pl.BlockSpec((w_b.shape[0], tn), lambda j, i: (0, j)),
                  pl.BlockSpec((d, tn), lambda j, i: (0, j)),
                  pl.BlockSpec((d, tn), lambda j, i: (0, j + n_col))],
        out_specs=pl.BlockSpec((tm, tn), lambda j, i: (i, j)),
        out_shape=jax.ShapeDtypeStruct((m, d), BF16),
        compiler_params=_params("parallel", "parallel"),
        name="gated_mix",
    )(attn, gm, h, w_a, w_b, w_gate, w_gate)


def _row_chunks(tm, rc):
    rc = min(tm, rc)
    return [slice(r, r + rc) for r in range(0, tm, rc)]


def _mix_out_kernel(mix_ref, x_ref, wo_ref, gpost_ref, gpre_ref, x1_ref, xn_ref):
    for rows in _row_chunks(x_ref.shape[0], NORM_ROW_CHUNK):
        mix = _dot(mix_ref[rows, :], wo_ref[...])
        x1 = x_ref[rows, :] + _rms(mix, gpost_ref[...])
        x1_ref[rows, :] = x1
        xn_ref[rows, :] = _rms(x1, gpre_ref[...]).astype(BF16)


def _mix_out(mixin, x, w_o, g_post, g_pre_ffn, *, tm):
    m, d = x.shape
    row = pl.BlockSpec((tm, d), lambda i: (i, 0))
    vec = pl.BlockSpec((1, d), lambda i: (0, 0))
    return pl.pallas_call(
        _mix_out_kernel,
        grid=(m // tm,),
        in_specs=[row, row, pl.BlockSpec((d, d), lambda i: (0, 0), pipeline_mode=pl.Buffered(1)), vec, vec],
        out_specs=[row, row],
        out_shape=[jax.ShapeDtypeStruct((m, d), F32), jax.ShapeDtypeStruct((m, d), BF16)],
        compiler_params=_params("parallel"),
        name="mix_out",
    )(mixin, x, w_o, g_post, g_pre_ffn)


def _ffn_in_kernel(x_ref, wa_ref, wg_ref, o_ref):
    x = x_ref[...]
    a = _dot(x, wa_ref[...])
    g = _dot(x, wg_ref[...])
    o_ref[...] = (jax.nn.silu(a) * g).astype(BF16)


def _ffn_in_scan_kernel(pt_ref, x_ref, wa_ref, wg_ref, *refs, groups):
    page_refs, o_ref, km_ref = refs[:-2], refs[-2], refs[-1]
    step = pl.program_id(0) * pl.num_programs(1) + pl.program_id(1)
    _scan_key_means(page_refs, km_ref, step % groups)
    _ffn_in_kernel(x_ref, wa_ref, wg_ref, o_ref)


def _ffn_in(xn, w_ffn_in, *, tm, tn, scan=None):
    m, d = xn.shape
    hidden = w_ffn_in.shape[1] // 2
    n_col, n_row = hidden // tn, m // tm
    in_specs = [pl.BlockSpec((tm, d), lambda j, i, *_: (i, 0)),
                pl.BlockSpec((d, tn), lambda j, i, *_: (0, j)),
                pl.BlockSpec((d, tn), lambda j, i, *_: (0, j + n_col))]
    out_spec = pl.BlockSpec((tm, tn), lambda j, i, *_: (i, j))
    out_shape = jax.ShapeDtypeStruct((m, hidden), BF16)
    if scan is None:
        return pl.pallas_call(
            _ffn_in_kernel, grid=(n_col, n_row), in_specs=in_specs, out_specs=out_spec, out_shape=out_shape,
            compiler_params=_params("parallel", "parallel"), name="ffn_in",
        )(xn, w_ffn_in, w_ffn_in)

    cache_k2, page_table = scan
    dec_b, n_pages = page_table.shape
    n_steps = n_col * n_row
    pps = dec_b * n_pages // n_steps
    groups = n_pages // pps
    assert pps * n_steps == dec_b * n_pages and groups * pps == n_pages and pps % 2 == 0
    n_past = n_pages // 2

    def page_spec(c):
        def idx(j, i, pt):
            step = j * n_row + i
            return (pt[step // groups, (step % groups) * pps + c], 0, 0)
        return pl.BlockSpec((None,) + cache_k2.shape[1:], idx)

    grid_spec = pltpu.PrefetchScalarGridSpec(
        num_scalar_prefetch=1,
        grid=(n_col, n_row),
        in_specs=in_specs + [page_spec(c) for c in range(pps)],
        out_specs=[out_spec,
                   pl.BlockSpec((None, N_HEADS, n_past, HEAD_DIM),
                                lambda j, i, pt: ((j * n_row + i) // groups, 0, 0, 0))],
    )
    return pl.pallas_call(
        functools.partial(_ffn_in_scan_kernel, groups=groups),
        grid_spec=grid_spec,
        out_shape=[out_shape, jax.ShapeDtypeStruct((dec_b, N_HEADS, n_past, HEAD_DIM), F32)],
        compiler_params=_params("arbitrary", "arbitrary"),
        name="ffn_in_scan",
    )(page_table, xn, w_ffn_in, w_ffn_in, *([cache_k2] * pps))


def _ffn_out_ple_kernel(hm_ref, w_ref, x1_ref, gpost_ref, gple_ref, p_ref, wpg_ref, wp_ref, y_ref, acc_scr,
                        side_work=lambda maybe_first, maybe_last: None):
    k = pl.program_id(1)
    last = pl.num_programs(1) - 1

    @pl.when(k == 0)
    def _():
        side_work(maybe_first=True, maybe_last=False)
        acc_scr[...] = _dot(hm_ref[...], w_ref[...])

    @pl.when((k > 0) & (k < last))
    def _():
        side_work(maybe_first=False, maybe_last=False)
        acc_scr[...] += _dot(hm_ref[...], w_ref[...])

    @pl.when(k == last)
    def _():
        side_work(maybe_first=False, maybe_last=True)
        for rows in _row_chunks(x1_ref.shape[0], NORM_ROW_CHUNK):
            f = acc_scr[rows, :] + _dot(hm_ref[rows, :], w_ref[...])
            x2 = x1_ref[rows, :] + _rms(f, gpost_ref[...])
            xg = _rms(x2, gple_ref[...]).astype(BF16)
            gate = jax.nn.sigmoid(_dot(xg, wpg_ref[...]))
            y_ref[rows, :] = x2 + _dot(p_ref[rows, :].astype(BF16), wp_ref[...]) * gate


def _ffn_out_ple_attend_kernel(pt_ref, sel_ref, hm_ref, w_ref, x1_ref, gpost_ref, gple_ref, p_ref, wpg_ref,
                               wp_ref, q_ref, kn_ref, vn_ref, ck_hbm, cv_hbm, y_ref, a_ref,
                               acc_scr, kbuf, vbuf, sem, *, dec_s, n_slots):
    step = pl.program_id(0) * pl.num_programs(1) + pl.program_id(1)
    n_steps = pl.num_programs(0) * pl.num_programs(1)

    def attend(maybe_first, maybe_last):
        slot = _sample_fetch(step, n_steps, pt_ref, sel_ref, ck_hbm, cv_hbm, kbuf, vbuf, sem, dec_s=dec_s,
                             n_slots=n_slots, maybe_first=maybe_first, maybe_last=maybe_last)
        _sample_attend(slot, q_ref, kn_ref, vn_ref, a_ref, kbuf, vbuf, dec_s=dec_s, n_slots=n_slots)

    _ffn_out_ple_kernel(hm_ref, w_ref, x1_ref, gpost_ref, gple_ref, p_ref, wpg_ref, wp_ref, y_ref, acc_scr,
                        side_work=attend)


def _ffn_out_ple(hmid, w_ffn_out, x1, g_post_ffn, g_ple, p, w_ple_gate, w_ple, *, tm, tk, attend=None):
    m, d = x1.shape
    hidden = hmid.shape[1]
    n_row, n_k = m // tm, hidden // tk
    assert n_k >= 2
    row = pl.BlockSpec((tm, d), lambda i, k, *_: (i, 0))
    vec = pl.BlockSpec((1, d), lambda i, k, *_: (0, 0))
    const = lambda shape: pl.BlockSpec(shape, lambda i, k, *_: (0, 0), pipeline_mode=pl.Buffered(1))
    in_specs = [pl.BlockSpec((tm, tk), lambda i, k, *_: (i, k)),
                pl.BlockSpec((tk, d), lambda i, k, *_: (k, 0)),
                row, vec, vec,
                pl.BlockSpec((tm, p.shape[1]), lambda i, k, *_: (i, 0)),
                const(w_ple_gate.shape), const(w_ple.shape)]
    args = (hmid, w_ffn_out, x1, g_post_ffn, g_ple, p, w_ple_gate, w_ple)
    y_shape = jax.ShapeDtypeStruct((m, d), F32)
    acc = pltpu.VMEM((tm, d), F32)
    if attend is None:
        return pl.pallas_call(
            _ffn_out_ple_kernel, grid=(n_row, n_k), in_specs=in_specs, out_specs=row, out_shape=y_shape,
            scratch_shapes=[acc], compiler_params=_params("parallel", "arbitrary"), name="ffn_out_ple",
        )(*args)

    qkv, cache_k4, cache_v4, page_table, sel, dec_s = attend
    _, dec_b, rows, width = qkv.shape
    page = cache_k4.shape[1]
    n_slots = dec_s * MOBA_TOPK
    assert n_row * n_k == dec_b * N_HEADS
    pair_idx = lambda i, k: ((i * n_k + k) // N_HEADS, 0, (i * n_k + k) % N_HEADS)
    pair_spec = pl.BlockSpec((None, rows, HEAD_DIM), lambda i, k, *_: pair_idx(i, k))
    part_spec = lambda part: pl.BlockSpec((None, None, rows, HEAD_DIM), lambda i, k, *_: (part,) + pair_idx(i, k))
    hbm_spec = pl.BlockSpec(memory_space=pl.ANY)
    grid_spec = pltpu.PrefetchScalarGridSpec(
        num_scalar_prefetch=2,
        grid=(n_row, n_k),
        in_specs=in_specs + [part_spec(0), part_spec(1), part_spec(2), hbm_spec, hbm_spec],
        out_specs=[row, pair_spec],
        scratch_shapes=[acc,
                        pltpu.VMEM((2, 2 * n_slots, page, HEAD_DIM), F32),
                        pltpu.VMEM((2, 2 * n_slots, page, HEAD_DIM), F32),
                        pltpu.SemaphoreType.DMA((2, 2))],
    )
    return pl.pallas_call(
        functools.partial(_ffn_out_ple_attend_kernel, dec_s=dec_s, n_slots=n_slots),
        grid_spec=grid_spec,
        out_shape=[y_shape, jax.ShapeDtypeStruct((dec_b, rows, width), F32)],
        compiler_params=_params("arbitrary", "arbitrary"),
        name="ffn_out_ple_attend",
    )(page_table, sel, *args, qkv, qkv, qkv, cache_k4, cache_v4)


def _tail_front(x, h, attn, gm, w, *, tm, scan=None):
    m, d = x.shape
    hidden = w["w_ffn_out"].shape[0]
    mixin = _gated_mix(attn, gm, h, w["w_a_out"], w["w_b_out"], w["w_gate"], tm=min(m, 1024), tn=512)
    x1, xn = _mix_out(mixin, x, w["w_o"], w["g_post_mix"], w["g_pre_ffn"], tm=tm)
    if scan is None:
        return x1, _ffn_in(xn, w["w_ffn_in"], tm=tm, tn=hidden // 4), None
    hmid, kmean = _ffn_in(xn, w["w_ffn_in"], tm=tm, tn=hidden // 4, scan=scan)
    return x1, hmid, kmean


def _tail_back(x1, hmid, p, w, *, tm, attend=None):
    return _ffn_out_ple(hmid, w["w_ffn_out"], x1, w["g_post_ffn"], w["g_ple"], p, w["w_ple_gate"], w["w_ple"],
                        tm=tm, tk=w["w_ffn_out"].shape[0] // 4, attend=attend)


def _rope_tables(pos):
    lane = jnp.arange(HEAD_DIM, dtype=jnp.int32)
    freqs = jnp.power(jnp.float32(ROPE_THETA), -2.0 * (lane % ROPE_HALF).astype(F32) / ROPE_DIM)
    ang = pos.astype(F32)[:, None] * freqs[None, :]
    cos, sin = jnp.cos(ang), jnp.sin(ang)
    c = jnp.where(lane < ROPE_DIM, cos, 1.0)
    sa = jnp.where(lane < ROPE_HALF, -sin, 0.0)
    sb = jnp.where((lane >= ROPE_HALF) & (lane < ROPE_DIM), sin, 0.0)
    return c, sa, sb


def kernel(x_prompt, x_sample, cache_k, cache_v, page_table, p_prompt, p_sample, g_pre_mix, w_in, g_vnorm, w_spatial, b_spatial, w_a_out, w_b_out, w_gate, w_o, g_post_mix, g_pre_ffn, w_ffn_in, w_ffn_out, g_post_ffn, g_ple, w_ple_gate, w_ple):
    batch, seq, d = x_prompt.shape
    dec_b, dec_s, _ = x_sample.shape
    depth = w_in.shape[0]
    page = cache_k.shape[2]
    past_len = page_table.shape[1] * page
    n_past = past_len // MOBA_BLOCK
    assert depth == 1 and seq % MOBA_BLOCK == 0 and past_len % MOBA_BLOCK == 0 and MOBA_BLOCK == 2 * page
    assert dec_s <= GMLP_CHUNK and n_past >= MOBA_TOPK
    l = 0

    w = {
        "w_ple": w_ple[l].astype(BF16),
        "g_post_mix": g_post_mix[l][None], "g_pre_ffn": g_pre_ffn[l][None],
        "g_post_ffn": g_post_ffn[l][None], "g_ple": g_ple[l][None],
    }
    tail_weights = {"w_a_out": w_a_out[l], "w_b_out": w_b_out[l], "w_gate": w_gate[l], "w_o": w_o[l],
                    "w_ffn_in": w_ffn_in[l], "w_ffn_out": w_ffn_out[l], "w_ple_gate": w_ple_gate[l]}
    w_in_b = w_in[l].astype(BF16)
    g_pre = g_pre_mix[l][None]
    g_vn = g_vnorm[l][None]

    w_tril = jnp.tril(w_spatial[l])
    wsp_p = w_tril.astype(BF16)
    bsp_p = b_spatial[l].T
    eye_b = jnp.eye(dec_b, dtype=F32)
    wsp_s = jnp.einsum("ab,gts->gatbs", eye_b, w_tril[:, :dec_s, :dec_s]).reshape(
        GMLP_GROUPS, dec_b * dec_s, dec_b * dec_s).astype(BF16)
    bsp_s = jnp.tile(b_spatial[l][:, :dec_s].T, (dec_b, 1))

    mp = batch * seq
    xp = x_prompt.reshape(mp, d)
    cp, sap, sbp = _rope_tables(jnp.arange(seq, dtype=jnp.int32))
    hp, qp, kp, vp, gmp = _mixer(xp, g_pre, w_in_b, cp, sap, sbp, g_vn, wsp_p, bsp_p,
                                 tm=MIXER_ROWS, chunk=GMLP_CHUNK, with_vn=False)
    ms = dec_b * dec_s
    xs = x_sample.reshape(ms, d)
    pos_s = past_len + jnp.arange(dec_s, dtype=jnp.int32)
    cs, sas, sbs = _rope_tables(jnp.tile(pos_s, dec_b))
    hs, qs, ks, vs, gms, vns = _mixer(xs, g_pre, w_in_b, cs, sas, sbs, g_vn, wsp_s, bsp_s,
                                      tm=ms, chunk=ms, with_vn=True)

    ap, cast = _moba_prompt(qp, kp, vp, batch=batch, seq=seq, casts=tuple(tail_weights.values()))
    w.update(zip(tail_weights.keys(), cast))
    n_pool = depth * cache_k.shape[1]
    cache_k2 = cache_k.reshape(n_pool, page * N_HEADS, HEAD_DIM)
    x1p, hmid_p, kmean = _tail_front(xp, hp, ap, gmp, w, tm=512, scan=(cache_k2, page_table))
    sel = _select_blocks(qs, kmean, ks, dec_s=dec_s, n_past=n_past)
    qkv = jnp.stack([qs, ks, vs]).reshape(3, dec_b, dec_s, ATTN_WIDTH)
    attend = (qkv, cache_k.reshape(n_pool, page, N_HEADS, HEAD_DIM),
              cache_v.reshape(n_pool, page, N_HEADS, HEAD_DIM), page_table, sel, dec_s)
    yp, a_s = _tail_back(x1p, hmid_p, p_prompt[l].reshape(mp, -1), w, tm=512, attend=attend)

    a_s = a_s.reshape(ms, ATTN_WIDTH).astype(BF16)
    x1s, hmid_s, _ = _tail_front(xs, hs, a_s, gms, w, tm=ms)
    ys = _tail_back(x1s, hmid_s, p_sample[l].reshape(ms, -1), w, tm=ms)

    return (yp.reshape(batch, seq, d), ys.reshape(dec_b, dec_s, d),
            kp.reshape(1, batch, seq, N_HEADS, HEAD_DIM), vp.reshape(1, batch, seq, N_HEADS, HEAD_DIM),
            ks.reshape(1, dec_b, dec_s, N_HEADS, HEAD_DIM), vs.reshape(1, dec_b, dec_s, N_HEADS, HEAD_DIM),
            vns.reshape(1, dec_b, dec_s, GMLP_WIDTH))
```

```python
import functools

import jax
import jax.numpy as jnp
from jax import lax
from jax.experimental import pallas as pl
from jax.experimental.pallas import tpu as pltpu

F32 = jnp.float32
BF16 = jnp.bfloat16

N_HEADS = 8
HEAD_DIM = 128
ATTN_WIDTH = N_HEADS * HEAD_DIM
MOBA_BLOCK = 256
MOBA_TOPK = 3
ROPE_THETA = 500000.0
ROPE_DIM = HEAD_DIM // 4
ROPE_HALF = ROPE_DIM // 2
GMLP_GROUPS = 8
GMLP_CHUNK = 128
GMLP_WIDTH = 1024
GMLP_GROUP_DIM = GMLP_WIDTH // GMLP_GROUPS
NORM_EPS = 1e-6
NEG_INF = -1e30
LOG2_E = 1.4426950408889634
TAKEN = -3e38
MIXER_ROWS = 256
W_IN_CAST_ROWS = 256
NORM_ROW_CHUNK = 256
BF16_SUBLANES = 16

VMEM_LIMIT_BYTES = 56 * 1024 * 1024


def _params(*semantics):
    return pltpu.CompilerParams(dimension_semantics=semantics, vmem_limit_bytes=VMEM_LIMIT_BYTES)


def _rms(x, g):
    return x * lax.rsqrt(jnp.mean(x * x, axis=-1, keepdims=True) + NORM_EPS) * g


def _dot(a, b):
    return jnp.dot(a, b, preferred_element_type=F32)


def _dot_nt(a, b):
    return lax.dot_general(a, b, (((1,), (1,)), ((), ())), preferred_element_type=F32)


def _mixer_outputs(proj, tm, cos_ref, sa_ref, sb_ref, gvn_ref, wsp_ref, bsp_ref,
                   q_out, k_out, v_out, gm_out, vn_out, *, chunk):

    def rope_to(out_ref, z):
        c, sa, sb = cos_ref[...], sa_ref[...], sb_ref[...]
        for hd in range(N_HEADS):
            sl = slice(hd * HEAD_DIM, (hd + 1) * HEAD_DIM)
            zs = z[:, sl]
            out_ref[:, sl] = (zs * c + pltpu.roll(zs, HEAD_DIM - ROPE_HALF, 1) * sa
                              + pltpu.roll(zs, ROPE_HALF, 1) * sb)

    rope_to(q_out, proj(0))
    rope_to(k_out, proj(1))
    v_out[...] = proj(2)
    u = jax.nn.gelu(proj(3))
    vg = jax.nn.gelu(proj(4))
    xc = vg - jnp.mean(vg, axis=-1, keepdims=True)
    vn = xc * lax.rsqrt(jnp.mean(xc * xc, axis=-1, keepdims=True) + NORM_EPS) * gvn_ref[...]
    if vn_out is not None:
        vn_out[...] = vn
    vnb = vn.astype(BF16)
    for c in range(tm // chunk):
        rows = slice(c * chunk, (c + 1) * chunk)
        for g in range(GMLP_GROUPS):
            cols = slice(g * GMLP_GROUP_DIM, (g + 1) * GMLP_GROUP_DIM)
            s = _dot(wsp_ref[g], vnb[rows, cols]) + bsp_ref[:, g:g + 1]
            gm_out[rows, cols] = (u[rows, cols] * s).astype(BF16)


def _mixer_kernel(x_ref, g_ref, w_ref, cos_ref, sa_ref, sb_ref, gvn_ref, wsp_ref, bsp_ref,
                  h_out, q_out, k_out, v_out, gm_out, *, chunk):
    hb = _rms(x_ref[...], g_ref[...]).astype(BF16)
    h_out[...] = hb
    proj = lambda s: _dot(hb, w_ref[:, s * ATTN_WIDTH:(s + 1) * ATTN_WIDTH])
    _mixer_outputs(proj, x_ref.shape[0], cos_ref, sa_ref, sb_ref, gvn_ref, wsp_ref, bsp_ref,
                   q_out, k_out, v_out, gm_out, None, chunk=chunk)


def _mixer_cast_kernel(x_ref, xk_ref, gk_ref, w_ref, cos_ref, sa_ref, sb_ref, gvn_ref, wsp_ref, bsp_ref,
                       h_out, q_out, k_out, v_out, gm_out, vn_out, wb_out, inv_scr, acc_scr, *, chunk):
    k = pl.program_id(0)

    @pl.when(k == 0)
    def _():
        x = x_ref[...]
        inv_scr[...] = lax.rsqrt(jnp.mean(x * x, axis=-1, keepdims=True) + NORM_EPS)
        acc_scr[...] = jnp.zeros(acc_scr.shape, F32)

    wb = w_ref[...].astype(BF16)
    wb_out[...] = wb
    hk = (xk_ref[...] * inv_scr[...] * gk_ref[...]).astype(BF16)
    h_out[...] = hk
    acc_scr[...] += _dot(hk, wb)

    @pl.when(k == pl.num_programs(0) - 1)
    def _():
        proj = lambda s: acc_scr[:, s * ATTN_WIDTH:(s + 1) * ATTN_WIDTH]
        _mixer_outputs(proj, x_ref.shape[0], cos_ref, sa_ref, sb_ref, gvn_ref, wsp_ref, bsp_ref,
                       q_out, k_out, v_out, gm_out, vn_out, chunk=chunk)


def _mixer_shapes(m, d):
    return [jax.ShapeDtypeStruct((m, d), BF16),
            jax.ShapeDtypeStruct((m, ATTN_WIDTH), F32),
            jax.ShapeDtypeStruct((m, ATTN_WIDTH), F32),
            jax.ShapeDtypeStruct((m, ATTN_WIDTH), F32),
            jax.ShapeDtypeStruct((m, GMLP_WIDTH), BF16)]


def _mixer(x, g_pre, w_in, cos_t, sa_t, sb_t, g_vn, wsp, bsp, *, tm, chunk):
    m, d = x.shape
    assert w_in.shape[1] == 3 * ATTN_WIDTH + 2 * GMLP_WIDTH and ATTN_WIDTH == GMLP_WIDTH
    t_blocks = cos_t.shape[0] // tm
    tab_spec = pl.BlockSpec((tm, HEAD_DIM), lambda i: (i % t_blocks, 0))
    row_spec = lambda w: pl.BlockSpec((tm, w), lambda i: (i, 0))
    const = lambda shape: pl.BlockSpec(shape, lambda i: (0,) * len(shape))
    return pl.pallas_call(
        functools.partial(_mixer_kernel, chunk=chunk),
        grid=(m // tm,),
        in_specs=[row_spec(d),
                  const((1, d)),
                  pl.BlockSpec(w_in.shape, lambda i: (0, 0), pipeline_mode=pl.Buffered(1)),
                  tab_spec, tab_spec, tab_spec,
                  const((1, GMLP_WIDTH)),
                  const((GMLP_GROUPS, chunk, chunk)),
                  const((chunk, GMLP_GROUPS))],
        out_specs=[row_spec(d), row_spec(ATTN_WIDTH), row_spec(ATTN_WIDTH), row_spec(ATTN_WIDTH),
                   row_spec(GMLP_WIDTH)],
        out_shape=_mixer_shapes(m, d),
        compiler_params=_params("parallel"),
        name="mixer",
    )(x, g_pre, w_in, cos_t, sa_t, sb_t, g_vn, wsp, bsp)


def _mixer_cast(x, g_pre, w_in_f32, cos_t, sa_t, sb_t, g_vn, wsp, bsp, *, tk):
    m, d = x.shape
    n_out = w_in_f32.shape[1]
    assert n_out == 3 * ATTN_WIDTH + 2 * GMLP_WIDTH and ATTN_WIDTH == GMLP_WIDTH and d % tk == 0
    const = lambda shape: pl.BlockSpec(shape, lambda k: (0,) * len(shape))
    full = lambda w: const((m, w))
    return pl.pallas_call(
        functools.partial(_mixer_cast_kernel, chunk=m),
        grid=(d // tk,),
        in_specs=[full(d),
                  pl.BlockSpec((m, tk), lambda k: (0, k)),
                  pl.BlockSpec((1, tk), lambda k: (0, k)),
                  pl.BlockSpec((tk, n_out), lambda k: (k, 0)),
                  full(HEAD_DIM), full(HEAD_DIM), full(HEAD_DIM),
                  const((1, GMLP_WIDTH)),
                  const((GMLP_GROUPS, m, m)),
                  const((m, GMLP_GROUPS))],
        out_specs=[pl.BlockSpec((m, tk), lambda k: (0, k)),
                   full(ATTN_WIDTH), full(ATTN_WIDTH), full(ATTN_WIDTH), full(GMLP_WIDTH), full(GMLP_WIDTH),
                   pl.BlockSpec((tk, n_out), lambda k: (k, 0))],
        out_shape=_mixer_shapes(m, d) + [jax.ShapeDtypeStruct((m, GMLP_WIDTH), F32),
                                         jax.ShapeDtypeStruct(w_in_f32.shape, BF16)],
        scratch_shapes=[pltpu.VMEM((m, 1), F32), pltpu.VMEM((m, n_out), F32)],
        compiler_params=_params("arbitrary"),
        name="mixer_cast",
    )(x, x, g_pre, w_in_f32, cos_t, sa_t, sb_t, g_vn, wsp, bsp)


def _moba_prompt_kernel(q_ref, k_ref, v_ref, *refs):
    n_cast = (len(refs) - 4) // 2
    o_ref = refs[n_cast]
    kb_scr, vt_scr, s_scr = refs[-3:]
    for src_ref, dst_ref in zip(refs[:n_cast], refs[n_cast + 1:2 * n_cast + 1]):
        dst_ref[...] = src_ref[...].astype(BF16)
    seq = q_ref.shape[0]
    nblk = seq // MOBA_BLOCK
    blk = MOBA_BLOCK
    q_scale = HEAD_DIM ** -0.5 * LOG2_E
    kb_scr[...] = k_ref[...].astype(BF16)
    vt_scr[:HEAD_DIM, :] = v_ref[...].T.astype(BF16)
    vt_scr[HEAD_DIM:, :] = jnp.ones((BF16_SUBLANES, seq), BF16)
    qt = q_ref[...].T
    kt = k_ref[...].T

    blk_id = lax.broadcasted_iota(jnp.int32, (nblk, seq), 0)
    q_blk = lax.broadcasted_iota(jnp.int32, (nblk, seq), 1) // blk
    gate = jnp.zeros((nblk, seq), F32)
    for n in range(nblk - 1):
        kmean_n = jnp.mean(kt[:, n * blk:(n + 1) * blk], axis=1, keepdims=True)
        g_past = jnp.sum(qt[:, (n + 1) * blk:] * kmean_n, axis=0, keepdims=True)
        g_n = jnp.concatenate([jnp.zeros((1, (n + 1) * blk), F32), g_past], axis=1)
        gate = jnp.where(blk_id == n, g_n, gate)
    rank = jnp.zeros((nblk, seq), jnp.int32)
    for m in range(nblk):
        g_m = gate[m:m + 1, :]
        beats = (m < q_blk) & ((g_m > gate) | ((g_m == gate) & (m < blk_id)))
        rank = rank + beats.astype(jnp.int32)
    keep = jnp.where((blk_id < q_blk) & (rank < MOBA_TOPK), 1.0, 0.0)

    key_i = lax.broadcasted_iota(jnp.int32, (blk, blk), 0)
    qry_i = lax.broadcasted_iota(jnp.int32, (blk, blk), 1)
    causal = key_i <= qry_i

    for j in range(nblk):
        cols = slice(j * blk, (j + 1) * blk)
        qb = (q_ref[cols, :] * q_scale).astype(BF16)
        keep_j = keep[:, cols]
        m_run = None
        for n in range(j + 1):
            st = _dot_nt(kb_scr[n * blk:(n + 1) * blk, :], qb)
            if n == j:
                st = jnp.where(causal, st, NEG_INF)
            else:
                st = jnp.where(keep_j[n:n + 1, :] > 0.5, st, NEG_INF)
            s_scr[n] = st
            m_n = jnp.max(st, axis=0, keepdims=True)
            m_run = m_n if m_run is None else jnp.maximum(m_run, m_n)
        acc = jnp.zeros((HEAD_DIM + BF16_SUBLANES, blk), F32)
        for n in range(j + 1):
            p = jnp.exp2(s_scr[n] - m_run)
            acc = acc + _dot(vt_scr[:, n * blk:(n + 1) * blk], p.astype(BF16))
        o_ref[cols, :] = (acc[:HEAD_DIM, :] / acc[HEAD_DIM:HEAD_DIM + 1, :]).T.astype(BF16)


def _moba_prompt(q, k, v, *, batch, seq, casts=()):
    n_steps = batch * N_HEADS
    spec = pl.BlockSpec((seq, HEAD_DIM), lambda b, h: (b, h))

    def slab_spec(wt):
        rows = wt.shape[0] // n_steps
        assert rows * n_steps == wt.shape[0] and rows % BF16_SUBLANES == 0
        return pl.BlockSpec((rows, wt.shape[1]), lambda b, h: (b * N_HEADS + h, 0))

    slabs = [slab_spec(wt) for wt in casts]
    out = pl.pallas_call(
        _moba_prompt_kernel,
        grid=(batch, N_HEADS),
        in_specs=[spec, spec, spec] + slabs,
        out_specs=[spec] + slabs,
        out_shape=[jax.ShapeDtypeStruct(q.shape, BF16)] + [jax.ShapeDtypeStruct(wt.shape, BF16) for wt in casts],
        scratch_shapes=[pltpu.VMEM((seq, HEAD_DIM), BF16), pltpu.VMEM((HEAD_DIM + BF16_SUBLANES, seq), BF16),
                        pltpu.VMEM((seq // MOBA_BLOCK, MOBA_BLOCK, MOBA_BLOCK), F32)],
        compiler_params=_params("parallel", "parallel"),
        name="moba_prompt",
    )(q, k, v, *casts)
    return out[0], tuple(out[1:])


def _scan_key_means(page_refs, km_ref, group):
    n_blk = len(page_refs) // 2
    page_rows = page_refs[0].shape[0] // N_HEADS

    def page_sum(ref):
        return jnp.sum(ref[...].reshape(page_rows, N_HEADS, HEAD_DIM), axis=0)

    for t in range(n_blk):
        tot = (page_sum(page_refs[2 * t]) + page_sum(page_refs[2 * t + 1])) * (1.0 / MOBA_BLOCK)
        for h in range(N_HEADS):
            km_ref[h, pl.ds(group * n_blk + t, 1), :] = tot[h:h + 1, :]


def _select_kernel(q_ref, km_ref, kn_ref, sel_ref, gate_scr, *, dec_s, n_past):
    dec_b = km_ref.shape[0]
    gate_scr[...] = jnp.full(gate_scr.shape, NEG_INF, F32)
    for b in range(dec_b):
        own_mean = jnp.sum(kn_ref[b * dec_s:(b + 1) * dec_s, :], axis=0, keepdims=True) * (1.0 / MOBA_BLOCK)
        for h in range(N_HEADS):
            cols = slice(h * HEAD_DIM, (h + 1) * HEAD_DIM)
            km = km_ref[b, h]
            for i in range(dec_s):
                col = (b * N_HEADS + h) * dec_s + i
                qi = q_ref[b * dec_s + i:b * dec_s + i + 1, cols]
                gate_scr[0:n_past, col:col + 1] = jnp.sum(km * qi, axis=-1, keepdims=True)
                gate_scr[n_past:n_past + 1, col:col + 1] = jnp.sum(qi * own_mean[:, cols], axis=-1,
                                                                   keepdims=True)
    gate = gate_scr[...]
    blk = lax.broadcasted_iota(jnp.int32, gate.shape, 0)
    gate = jnp.where(blk < n_past, gate, NEG_INF)
    out_row = lax.broadcasted_iota(jnp.int32, sel_ref.shape, 0)
    sel = jnp.zeros(sel_ref.shape, jnp.int32)
    for t in range(MOBA_TOPK):
        best = jnp.max(gate, axis=0, keepdims=True)
        idx = jnp.min(jnp.where(gate == best, blk, gate.shape[0]), axis=0, keepdims=True)
        sel = jnp.where(out_row == t, idx, sel)
        gate = jnp.where(blk == idx, TAKEN, gate)
    sel_ref[...] = sel


def _select_blocks(q2, kmean, kn2, *, dec_s, n_past):
    cols = q2.shape[0] * N_HEADS
    gate_rows = 8 * (-(-(n_past + 1) // 8))
    return pl.pallas_call(
        functools.partial(_select_kernel, dec_s=dec_s, n_past=n_past),
        out_shape=jax.ShapeDtypeStruct((8, cols), jnp.int32),
        scratch_shapes=[pltpu.VMEM((gate_rows, cols), F32)],
        compiler_params=pltpu.CompilerParams(vmem_limit_bytes=VMEM_LIMIT_BYTES),
        name="select_blocks",
    )(q2, kmean, kn2)


def _sample_fetch(step, n_steps, pt_ref, sel_ref, ck_hbm, cv_hbm, kbuf, vbuf, sem, *, dec_s, n_slots,
                  maybe_first, maybe_last):
    n_pages = 2 * n_slots

    def page_copies(step_idx, slot):
        bb, hh = step_idx // N_HEADS, step_idx % N_HEADS
        copies = []
        for c in range(n_pages):
            query, pick = (c // 2) // MOBA_TOPK, (c // 2) % MOBA_TOPK
            blk = sel_ref[pick, step_idx * dec_s + query]
            pg = pt_ref[bb, 2 * blk + c % 2]
            src = (pg, slice(None), hh, slice(None))
            copies.append(pltpu.make_async_copy(ck_hbm.at[src], kbuf.at[slot, c], sem.at[0, slot]))
            copies.append(pltpu.make_async_copy(cv_hbm.at[src], vbuf.at[slot, c], sem.at[1, slot]))
        return copies

    def start_next():
        for cp in page_copies(step + 1, (step + 1) % 2):
            cp.start()

    if maybe_first:
        @pl.when(step == 0)
        def _():
            for cp in page_copies(step, 0):
                cp.start()

    if maybe_last:
        pl.when(step + 1 < n_steps)(start_next)
    else:
        start_next()

    slot = step % 2
    for cp in page_copies(step, slot):
        cp.wait()
    return slot


def _sample_attend(slot, q_ref, kn_ref, vn_ref, o_ref, kbuf, vbuf, *, dec_s, n_slots):
    pages_per_query = 2 * MOBA_TOPK
    assert n_slots * 2 == dec_s * pages_per_query
    scale = HEAD_DIM ** -0.5
    kn, vn = kn_ref[...], vn_ref[...]
    new_row = lax.broadcasted_iota(jnp.int32, (dec_s, 1), 0)
    for i in range(dec_s):
        qi = q_ref[i:i + 1, :]
        pages = [i * pages_per_query + c for c in range(pages_per_query)]
        s_pages = [jnp.sum(kbuf[slot, c] * qi, axis=-1, keepdims=True) * scale for c in pages]
        s_own = jnp.where(new_row <= i, jnp.sum(kn * qi, axis=-1, keepdims=True) * scale, NEG_INF)
        m = jnp.max(s_own, axis=0, keepdims=True)
        for s in s_pages:
            m = jnp.maximum(m, jnp.max(s, axis=0, keepdims=True))
        p_own = jnp.exp(s_own - m)
        l = jnp.sum(p_own, axis=0, keepdims=True)
        acc = jnp.sum(p_own * vn, axis=0, keepdims=True)
        for c, s in zip(pages, s_pages):
            p = jnp.exp(s - m)
            l = l + jnp.sum(p, axis=0, keepdims=True)
            acc = acc + jnp.sum(p * vbuf[slot, c], axis=0, keepdims=True)
        o_ref[i:i + 1, :] = acc / l


def _gated_mix_kernel(a_ref, b_ref, h_ref, wa_ref, wb_ref, wga_ref, wgb_ref, o_ref):
    h = h_ref[...]
    ga = jax.nn.sigmoid(_dot(h, wga_ref[...]))
    gb = jax.nn.sigmoid(_dot(h, wgb_ref[...]))
    a = _dot(a_ref[...], wa_ref[...])
    b = _dot(b_ref[...], wb_ref[...])
    o_ref[...] = (ga * a + gb * b).astype(BF16)


def _gated_mix(attn, gm, h, w_a, w_b, w_gate, *, tm, tn):
    m, d = h.shape
    n_col = d // tn
    return pl.pallas_call(
        _gated_mix_kernel,
        grid=(n_col, m // tm),
        in_specs=[pl.BlockSpec((tm, attn.shape[1]), lambda j, i: (i, 0)),
                  pl.BlockSpec((tm, gm.shape[1]), lambda j, i: (i, 0)),
                  pl.BlockSpec((tm, d), lambda j, i: (i, 0)),
                  pl.BlockSpec((w_a.shape[0], tn), lambda j, i: (0, j)),
                  pl.BlockSpec((w_b.shape[0], tn), lambda j, i: (0, j)),
                  pl.BlockSpec((d, tn), lambda j, i: (0, j)),
                  pl.BlockSpec((d, tn), lambda j, i: (0, j + n_col))],
        out_specs=pl.BlockSpec((tm, tn), lambda j, i: (i, j)),
        out_shape=jax.ShapeDtypeStruct((m, d), BF16),
        compiler_params=_params("parallel", "parallel"),
        name="gated_mix",
    )(attn, gm, h, w_a, w_b, w_gate, w_gate)


def _row_chunks(tm, rc):
    rc = min(tm, rc)
    return [slice(r, r + rc) for r in range(0, tm, rc)]


def _mix_out_kernel(mix_ref, x_ref, wo_ref, gpost_ref, gpre_ref, x1_ref, xn_ref):
    for rows in _row_chunks(x_ref.shape[0], NORM_ROW_CHUNK):
        mix = _dot(mix_ref[rows, :], wo_ref[...])
        x1 = x_ref[rows, :] + _rms(mix, gpost_ref[...])
        x1_ref[rows, :] = x1
        xn_ref[rows, :] = _rms(x1, gpre_ref[...]).astype(BF16)


def _mix_out(mixin, x, w_o, g_post, g_pre_ffn, *, tm):
    m, d = x.shape
    row = pl.BlockSpec((tm, d), lambda i: (i, 0))
    vec = pl.BlockSpec((1, d), lambda i: (0, 0))
    return pl.pallas_call(
        _mix_out_kernel,
        grid=(m // tm,),
        in_specs=[row, row, pl.BlockSpec((d, d), lambda i: (0, 0), pipeline_mode=pl.Buffered(1)), vec, vec],
        out_specs=[row, row],
        out_shape=[jax.ShapeDtypeStruct((m, d), F32), jax.ShapeDtypeStruct((m, d), BF16)],
        compiler_params=_params("parallel"),
        name="mix_out",
    )(mixin, x, w_o, g_post, g_pre_ffn)


def _ffn_in_kernel(x_ref, wa_ref, wg_ref, o_ref):
    x = x_ref[...]
    a = _dot(x, wa_ref[...])
    g = _dot(x, wg_ref[...])
    o_ref[...] = (jax.nn.silu(a) * g).astype(BF16)


def _ffn_in_scan_kernel(pt_ref, x_ref, wa_ref, wg_ref, *refs, groups):
    page_refs, o_ref, km_ref = refs[:-2], refs[-2], refs[-1]
    step = pl.program_id(0) * pl.num_programs(1) + pl.program_id(1)
    _scan_key_means(page_refs, km_ref, step % groups)
    _ffn_in_kernel(x_ref, wa_ref, wg_ref, o_ref)


def _ffn_in(xn, w_ffn_in, *, tm, tn, scan=None):
    m, d = xn.shape
    hidden = w_ffn_in.shape[1] // 2
    n_col, n_row = hidden // tn, m // tm
    in_specs = [pl.BlockSpec((tm, d), lambda j, i, *_: (i, 0)),
                pl.BlockSpec((d, tn), lambda j, i, *_: (0, j)),
                pl.BlockSpec((d, tn), lambda j, i, *_: (0, j + n_col))]
    out_spec = pl.BlockSpec((tm, tn), lambda j, i, *_: (i, j))
    out_shape = jax.ShapeDtypeStruct((m, hidden), BF16)
    if scan is None:
        return pl.pallas_call(
            _ffn_in_kernel, grid=(n_col, n_row), in_specs=in_specs, out_specs=out_spec, out_shape=out_shape,
            compiler_params=_params("parallel", "parallel"), name="ffn_in",
        )(xn, w_ffn_in, w_ffn_in)

    cache_k2, page_table = scan
    dec_b, n_pages = page_table.shape
    n_steps = n_col * n_row
    pps = dec_b * n_pages // n_steps
    groups = n_pages // pps
    assert pps * n_steps == dec_b * n_pages and groups * pps == n_pages and pps % 2 == 0
    n_past = n_pages // 2

    def page_spec(c):
        def idx(j, i, pt):
            step = j * n_row + i
            return (pt[step // groups, (step % groups) * pps + c], 0, 0)
        return pl.BlockSpec((None,) + cache_k2.shape[1:], idx)

    grid_spec = pltpu.PrefetchScalarGridSpec(
        num_scalar_prefetch=1,
        grid=(n_col, n_row),
        in_specs=in_specs + [page_spec(c) for c in range(pps)],
        out_specs=[out_spec,
                   pl.BlockSpec((None, N_HEADS, n_past, HEAD_DIM),
                                lambda j, i, pt: ((j * n_row + i) // groups, 0, 0, 0))],
    )
    return pl.pallas_call(
        functools.partial(_ffn_in_scan_kernel, groups=groups),
        grid_spec=grid_spec,
        out_shape=[out_shape, jax.ShapeDtypeStruct((dec_b, N_HEADS, n_past, HEAD_DIM), F32)],
        compiler_params=_params("arbitrary", "arbitrary"),
        name="ffn_in_scan",
    )(page_table, xn, w_ffn_in, w_ffn_in, *([cache_k2] * pps))


def _ffn_out_ple_kernel(hm_ref, w_ref, x1_ref, gpost_ref, gple_ref, p_ref, wpg_ref, wp_ref, y_ref, acc_scr,
                        side_work=lambda maybe_first, maybe_last: None):
    k = pl.program_id(1)
    last = pl.num_programs(1) - 1

    @pl.when(k == 0)
    def _():
        side_work(maybe_first=True, maybe_last=False)
        acc_scr[...] = _dot(hm_ref[...], w_ref[...])

    @pl.when((k > 0) & (k < last))
    def _():
        side_work(maybe_first=False, maybe_last=False)
        acc_scr[...] += _dot(hm_ref[...], w_ref[...])

    @pl.when(k == last)
    def _():
        side_work(maybe_first=False, maybe_last=True)
        for rows in _row_chunks(x1_ref.shape[0], NORM_ROW_CHUNK):
            f = acc_scr[rows, :] + _dot(hm_ref[rows, :], w_ref[...])
            x2 = x1_ref[rows, :] + _rms(f, gpost_ref[...])
            xg = _rms(x2, gple_ref[...]).astype(BF16)
            gate = jax.nn.sigmoid(_dot(xg, wpg_ref[...]))
            y_ref[rows, :] = x2 + _dot(p_ref[rows, :].astype(BF16), wp_ref[...]) * gate


def _ffn_out_ple_attend_kernel(pt_ref, sel_ref, hm_ref, w_ref, x1_ref, gpost_ref, gple_ref, p_ref, wpg_ref,
                               wp_ref, q_ref, kn_ref, vn_ref, ck_hbm, cv_hbm, y_ref, a_ref,
                               acc_scr, kbuf, vbuf, sem, *, dec_s, n_slots):
    step = pl.program_id(0) * pl.num_programs(1) + pl.program_id(1)
    n_steps = pl.num_programs(0) * pl.num_programs(1)

    def attend(maybe_first, maybe_last):
        slot = _sample_fetch(step, n_steps, pt_ref, sel_ref, ck_hbm, cv_hbm, kbuf, vbuf, sem, dec_s=dec_s,
                             n_slots=n_slots, maybe_first=maybe_first, maybe_last=maybe_last)
        _sample_attend(slot, q_ref, kn_ref, vn_ref, a_ref, kbuf, vbuf, dec_s=dec_s, n_slots=n_slots)

    _ffn_out_ple_kernel(hm_ref, w_ref, x1_ref, gpost_ref, gple_ref, p_ref, wpg_ref, wp_ref, y_ref, acc_scr,
                        side_work=attend)


def _ffn_out_ple(hmid, w_ffn_out, x1, g_post_ffn, g_ple, p, w_ple_gate, w_ple, *, tm, tk, attend=None):
    m, d = x1.shape
    hidden = hmid.shape[1]
    n_row, n_k = m // tm, hidden // tk
    assert n_k >= 2
    row = pl.BlockSpec((tm, d), lambda i, k, *_: (i, 0))
    vec = pl.BlockSpec((1, d), lambda i, k, *_: (0, 0))
    const = lambda shape: pl.BlockSpec(shape, lambda i, k, *_: (0, 0), pipeline_mode=pl.Buffered(1))
    in_specs = [pl.BlockSpec((tm, tk), lambda i, k, *_: (i, k)),
                pl.BlockSpec((tk, d), lambda i, k, *_: (k, 0)),
                row, vec, vec,
                pl.BlockSpec((tm, p.shape[1]), lambda i, k, *_: (i, 0)),
                const(w_ple_gate.shape), const(w_ple.shape)]
    args = (hmid, w_ffn_out, x1, g_post_ffn, g_ple, p, w_ple_gate, w_ple)
    y_shape = jax.ShapeDtypeStruct((m, d), F32)
    acc = pltpu.VMEM((tm, d), F32)
    if attend is None:
        return pl.pallas_call(
            _ffn_out_ple_kernel, grid=(n_row, n_k), in_specs=in_specs, out_specs=row, out_shape=y_shape,
            scratch_shapes=[acc], compiler_params=_params("parallel", "arbitrary"), name="ffn_out_ple",
        )(*args)

    qkv, cache_k4, cache_v4, page_table, sel, dec_s = attend
    _, dec_b, rows, width = qkv.shape
    page = cache_k4.shape[1]
    n_slots = dec_s * MOBA_TOPK
    assert n_row * n_k == dec_b * N_HEADS
    pair_idx = lambda i, k: ((i * n_k + k) // N_HEADS, 0, (i * n_k + k) % N_HEADS)
    pair_spec = pl.BlockSpec((None, rows, HEAD_DIM), lambda i, k, *_: pair_idx(i, k))
    part_spec = lambda part: pl.BlockSpec((None, None, rows, HEAD_DIM), lambda i, k, *_: (part,) + pair_idx(i, k))
    hbm_spec = pl.BlockSpec(memory_space=pl.ANY)
    grid_spec = pltpu.PrefetchScalarGridSpec(
        num_scalar_prefetch=2,
        grid=(n_row, n_k),
        in_specs=in_specs + [part_spec(0), part_spec(1), part_spec(2), hbm_spec, hbm_spec],
        out_specs=[row, pair_spec],
        scratch_shapes=[acc,
                        pltpu.VMEM((2, 2 * n_slots, page, HEAD_DIM), F32),
                        pltpu.VMEM((2, 2 * n_slots, page, HEAD_DIM), F32),
                        pltpu.SemaphoreType.DMA((2, 2))],
    )
    return pl.pallas_call(
        functools.partial(_ffn_out_ple_attend_kernel, dec_s=dec_s, n_slots=n_slots),
        grid_spec=grid_spec,
        out_shape=[y_shape, jax.ShapeDtypeStruct((dec_b, rows, width), F32)],
        compiler_params=_params("arbitrary", "arbitrary"),
        name="ffn_out_ple_attend",
    )(page_table, sel, *args, qkv, qkv, qkv, cache_k4, cache_v4)


def _tail_front(x, h, attn, gm, w, *, tm, scan=None):
    m, d = x.shape
    hidden = w["w_ffn_out"].shape[0]
    mixin = _gated_mix(attn, gm, h, w["w_a_out"], w["w_b_out"], w["w_gate"], tm=min(m, 1024), tn=512)
    x1, xn = _mix_out(mixin, x, w["w_o"], w["g_post_mix"], w["g_pre_ffn"], tm=tm)
    if scan is None:
        return x1, _ffn_in(xn, w["w_ffn_in"], tm=tm, tn=hidden // 4), None
    hmid, kmean = _ffn_in(xn, w["w_ffn_in"], tm=tm, tn=hidden // 4, scan=scan)
    return x1, hmid, kmean


def _tail_back(x1, hmid, p, w, *, tm, attend=None):
    return _ffn_out_ple(hmid, w["w_ffn_out"], x1, w["g_post_ffn"], w["g_ple"], p, w["w_ple_gate"], w["w_ple"],
                        tm=tm, tk=w["w_ffn_out"].shape[0] // 4, attend=attend)


def _rope_tables(pos):
    lane = jnp.arange(HEAD_DIM, dtype=jnp.int32)
    freqs = jnp.power(jnp.float32(ROPE_THETA), -2.0 * (lane % ROPE_HALF).astype(F32) / ROPE_DIM)
    ang = pos.astype(F32)[:, None] * freqs[None, :]
    cos, sin = jnp.cos(ang), jnp.sin(ang)
    c = jnp.where(lane < ROPE_DIM, cos, 1.0)
    sa = jnp.where(lane < ROPE_HALF, -sin, 0.0)
    sb = jnp.where((lane >= ROPE_HALF) & (lane < ROPE_DIM), sin, 0.0)
    return c, sa, sb


def kernel(x_prompt, x_sample, cache_k, cache_v, page_table, p_prompt, p_sample, g_pre_mix, w_in, g_vnorm, w_spatial, b_spatial, w_a_out, w_b_out, w_gate, w_o, g_post_mix, g_pre_ffn, w_ffn_in, w_ffn_out, g_post_ffn, g_ple, w_ple_gate, w_ple):
    batch, seq, d = x_prompt.shape
    dec_b, dec_s, _ = x_sample.shape
    depth = w_in.shape[0]
    page = cache_k.shape[2]
    past_len = page_table.shape[1] * page
    n_past = past_len // MOBA_BLOCK
    assert depth == 1 and seq % MOBA_BLOCK == 0 and past_len % MOBA_BLOCK == 0 and MOBA_BLOCK == 2 * page
    assert dec_s <= GMLP_CHUNK and n_past >= MOBA_TOPK
    l = 0

    w = {
        "w_ple": w_ple[l].astype(BF16),
        "g_post_mix": g_post_mix[l][None], "g_pre_ffn": g_pre_ffn[l][None],
        "g_post_ffn": g_post_ffn[l][None], "g_ple": g_ple[l][None],
    }
    tail_weights = {"w_a_out": w_a_out[l], "w_b_out": w_b_out[l], "w_gate": w_gate[l], "w_o": w_o[l],
                    "w_ffn_in": w_ffn_in[l], "w_ffn_out": w_ffn_out[l], "w_ple_gate": w_ple_gate[l]}
    g_pre = g_pre_mix[l][None]
    g_vn = g_vnorm[l][None]

    w_tril = jnp.tril(w_spatial[l])
    wsp_p = w_tril.astype(BF16)
    bsp_p = b_spatial[l].T
    eye_b = jnp.eye(dec_b, dtype=F32)
    wsp_s = jnp.einsum("ab,gts->gatbs", eye_b, w_tril[:, :dec_s, :dec_s]).reshape(
        GMLP_GROUPS, dec_b * dec_s, dec_b * dec_s).astype(BF16)
    bsp_s = jnp.tile(b_spatial[l][:, :dec_s].T, (dec_b, 1))

    ms = dec_b * dec_s
    xs = x_sample.reshape(ms, d)
    pos_s = past_len + jnp.arange(dec_s, dtype=jnp.int32)
    cs, sas, sbs = _rope_tables(jnp.tile(pos_s, dec_b))
    hs, qs, ks, vs, gms, vns, w_in_b = _mixer_cast(xs, g_pre, w_in[l], cs, sas, sbs, g_vn, wsp_s, bsp_s,
                                                   tk=W_IN_CAST_ROWS)
    mp = batch * seq
    xp = x_prompt.reshape(mp, d)
    cp, sap, sbp = _rope_tables(jnp.arange(seq, dtype=jnp.int32))
    hp, qp, kp, vp, gmp = _mixer(xp, g_pre, w_in_b, cp, sap, sbp, g_vn, wsp_p, bsp_p,
                                 tm=MIXER_ROWS, chunk=GMLP_CHUNK)

    ap, cast = _moba_prompt(qp, kp, vp, batch=batch, seq=seq, casts=tuple(tail_weights.values()))
    w.update(zip(tail_weights.keys(), cast))
    n_pool = depth * cache_k.shape[1]
    cache_k2 = cache_k.reshape(n_pool, page * N_HEADS, HEAD_DIM)
    x1p, hmid_p, kmean = _tail_front(xp, hp, ap, gmp, w, tm=512, scan=(cache_k2, page_table))
    sel = _select_blocks(qs, kmean, ks, dec_s=dec_s, n_past=n_past)
    qkv = jnp.stack([qs, ks, vs]).reshape(3, dec_b, dec_s, ATTN_WIDTH)
    attend = (qkv, cache_k.reshape(n_pool, page, N_HEADS, HEAD_DIM),
              cache_v.reshape(n_pool, page, N_HEADS, HEAD_DIM), page_table, sel, dec_s)
    yp, a_s = _tail_back(x1p, hmid_p, p_prompt[l].reshape(mp, -1), w, tm=512, attend=attend)

    a_s = a_s.reshape(ms, ATTN_WIDTH).astype(BF16)
    x1s, hmid_s, _ = _tail_front(xs, hs, a_s, gms, w, tm=ms)
    ys = _tail_back(x1s, hmid_s, p_sample[l].reshape(ms, -1), w, tm=ms)

    return (yp.reshape(batch, seq, d), ys.reshape(dec_b, dec_s, d),
            kp.reshape(1, batch, seq, N_HEADS, HEAD_DIM), vp.reshape(1, batch, seq, N_HEADS, HEAD_DIM),
            ks.reshape(1, dec_b, dec_s, N_HEADS, HEAD_DIM), vs.reshape(1, dec_b, dec_s, N_HEADS, HEAD_DIM),
            vns.reshape(1, dec_b, dec_s, GMLP_WIDTH))
```

```python
import functools

import jax
import jax.numpy as jnp
from jax import lax
from jax.experimental import pallas as pl
from jax.experimental.pallas import tpu as pltpu

F32 = jnp.float32
BF16 = jnp.bfloat16

N_HEADS = 8
HEAD_DIM = 128
ATTN_WIDTH = N_HEADS * HEAD_DIM
MOBA_BLOCK = 256
MOBA_TOPK = 3
ROPE_THETA = 500000.0
ROPE_DIM = HEAD_DIM // 4
ROPE_HALF = ROPE_DIM // 2
GMLP_GROUPS = 8
GMLP_CHUNK = 128
GMLP_WIDTH = 1024
GMLP_GROUP_DIM = GMLP_WIDTH // GMLP_GROUPS
NORM_EPS = 1e-6
NEG_INF = -1e30
LOG2_E = 1.4426950408889634
TAKEN = -3e38
MIXER_ROWS = 256
W_IN_CAST_ROWS = 256
NORM_ROW_CHUNK = 256
BF16_SUBLANES = 16

VMEM_LIMIT_BYTES = 56 * 1024 * 1024


def _params(*semantics):
    return pltpu.CompilerParams(dimension_semantics=semantics, vmem_limit_bytes=VMEM_LIMIT_BYTES)


def _rms(x, g):
    return x * lax.rsqrt(jnp.mean(x * x, axis=-1, keepdims=True) + NORM_EPS) * g


def _dot(a, b):
    return jnp.dot(a, b, preferred_element_type=F32)


def _slab_specs(weights, n_steps, step_of):
    specs = []
    for wt in weights:
        rows = wt.shape[0] // n_steps
        assert rows * n_steps == wt.shape[0] and rows % BF16_SUBLANES == 0
        specs.append(pl.BlockSpec((rows, wt.shape[1]), lambda *g: (step_of(*g), 0)))
    return specs


def _cast_slabs(src_refs, dst_refs):
    for src_ref, dst_ref in zip(src_refs, dst_refs):
        dst_ref[...] = src_ref[...].astype(BF16)


def _dot_nt(a, b):
    return lax.dot_general(a, b, (((1,), (1,)), ((), ())), preferred_element_type=F32)


def _mixer_outputs(proj, tm, cos_ref, sa_ref, sb_ref, gvn_ref, wsp_ref, bsp_ref,
                   q_out, k_out, v_out, gm_out, vn_out, *, chunk):

    def rope_to(out_ref, z):
        c, sa, sb = cos_ref[...], sa_ref[...], sb_ref[...]
        for hd in range(N_HEADS):
            sl = slice(hd * HEAD_DIM, (hd + 1) * HEAD_DIM)
            zs = z[:, sl]
            out_ref[:, sl] = (zs * c + pltpu.roll(zs, HEAD_DIM - ROPE_HALF, 1) * sa
                              + pltpu.roll(zs, ROPE_HALF, 1) * sb)

    rope_to(q_out, proj(0))
    rope_to(k_out, proj(1))
    v_out[...] = proj(2)
    u = jax.nn.gelu(proj(3))
    vg = jax.nn.gelu(proj(4))
    xc = vg - jnp.mean(vg, axis=-1, keepdims=True)
    vn = xc * lax.rsqrt(jnp.mean(xc * xc, axis=-1, keepdims=True) + NORM_EPS) * gvn_ref[...]
    if vn_out is not None:
        vn_out[...] = vn
    vnb = vn.astype(BF16)
    for c in range(tm // chunk):
        rows = slice(c * chunk, (c + 1) * chunk)
        for g in range(GMLP_GROUPS):
            cols = slice(g * GMLP_GROUP_DIM, (g + 1) * GMLP_GROUP_DIM)
            s = _dot(wsp_ref[g], vnb[rows, cols]) + bsp_ref[:, g:g + 1]
            gm_out[rows, cols] = (u[rows, cols] * s).astype(BF16)


def _mixer_kernel(x_ref, g_ref, w_ref, cos_ref, sa_ref, sb_ref, gvn_ref, wsp_ref, bsp_ref,
                  h_out, q_out, k_out, v_out, gm_out, *, chunk):
    hb = _rms(x_ref[...], g_ref[...]).astype(BF16)
    h_out[...] = hb
    proj = lambda s: _dot(hb, w_ref[:, s * ATTN_WIDTH:(s + 1) * ATTN_WIDTH])
    _mixer_outputs(proj, x_ref.shape[0], cos_ref, sa_ref, sb_ref, gvn_ref, wsp_ref, bsp_ref,
                   q_out, k_out, v_out, gm_out, None, chunk=chunk)


def _mixer_cast_kernel(x_ref, xk_ref, gk_ref, w_ref, cos_ref, sa_ref, sb_ref, gvn_ref, wsp_ref, bsp_ref,
                       h_out, q_out, k_out, v_out, gm_out, vn_out, wb_out, inv_scr, acc_scr, *, chunk):
    k = pl.program_id(0)

    @pl.when(k == 0)
    def _():
        x = x_ref[...]
        inv_scr[...] = lax.rsqrt(jnp.mean(x * x, axis=-1, keepdims=True) + NORM_EPS)
        acc_scr[...] = jnp.zeros(acc_scr.shape, F32)

    wb = w_ref[...].astype(BF16)
    wb_out[...] = wb
    hk = (xk_ref[...] * inv_scr[...] * gk_ref[...]).astype(BF16)
    h_out[...] = hk
    acc_scr[...] += _dot(hk, wb)

    @pl.when(k == pl.num_programs(0) - 1)
    def _():
        proj = lambda s: acc_scr[:, s * ATTN_WIDTH:(s + 1) * ATTN_WIDTH]
        _mixer_outputs(proj, x_ref.shape[0], cos_ref, sa_ref, sb_ref, gvn_ref, wsp_ref, bsp_ref,
                       q_out, k_out, v_out, gm_out, vn_out, chunk=chunk)


def _mixer_shapes(m, d):
    return [jax.ShapeDtypeStruct((m, d), BF16),
            jax.ShapeDtypeStruct((m, ATTN_WIDTH), F32),
            jax.ShapeDtypeStruct((m, ATTN_WIDTH), F32),
            jax.ShapeDtypeStruct((m, ATTN_WIDTH), F32),
            jax.ShapeDtypeStruct((m, GMLP_WIDTH), BF16)]


def _mixer(x, g_pre, w_in, cos_t, sa_t, sb_t, g_vn, wsp, bsp, *, tm, chunk):
    m, d = x.shape
    assert w_in.shape[1] == 3 * ATTN_WIDTH + 2 * GMLP_WIDTH and ATTN_WIDTH == GMLP_WIDTH
    t_blocks = cos_t.shape[0] // tm
    tab_spec = pl.BlockSpec((tm, HEAD_DIM), lambda i: (i % t_blocks, 0))
    row_spec = lambda w: pl.BlockSpec((tm, w), lambda i: (i, 0))
    const = lambda shape: pl.BlockSpec(shape, lambda i: (0,) * len(shape))
    return pl.pallas_call(
        functools.partial(_mixer_kernel, chunk=chunk),
        grid=(m // tm,),
        in_specs=[row_spec(d),
                  const((1, d)),
                  pl.BlockSpec(w_in.shape, lambda i: (0, 0), pipeline_mode=pl.Buffered(1)),
                  tab_spec, tab_spec, tab_spec,
                  const((1, GMLP_WIDTH)),
                  const((GMLP_GROUPS, chunk, chunk)),
                  const((chunk, GMLP_GROUPS))],
        out_specs=[row_spec(d), row_spec(ATTN_WIDTH), row_spec(ATTN_WIDTH), row_spec(ATTN_WIDTH),
                   row_spec(GMLP_WIDTH)],
        out_shape=_mixer_shapes(m, d),
        compiler_params=_params("parallel"),
        name="mixer",
    )(x, g_pre, w_in, cos_t, sa_t, sb_t, g_vn, wsp, bsp)


def _mixer_cast(x, g_pre, w_in_f32, cos_t, sa_t, sb_t, g_vn, wsp, bsp, *, tk):
    m, d = x.shape
    n_out = w_in_f32.shape[1]
    assert n_out == 3 * ATTN_WIDTH + 2 * GMLP_WIDTH and ATTN_WIDTH == GMLP_WIDTH and d % tk == 0
    const = lambda shape: pl.BlockSpec(shape, lambda k: (0,) * len(shape))
    full = lambda w: const((m, w))
    return pl.pallas_call(
        functools.partial(_mixer_cast_kernel, chunk=m),
        grid=(d // tk,),
        in_specs=[full(d),
                  pl.BlockSpec((m, tk), lambda k: (0, k)),
                  pl.BlockSpec((1, tk), lambda k: (0, k)),
                  pl.BlockSpec((tk, n_out), lambda k: (k, 0)),
                  full(HEAD_DIM), full(HEAD_DIM), full(HEAD_DIM),
                  const((1, GMLP_WIDTH)),
                  const((GMLP_GROUPS, m, m)),
                  const((m, GMLP_GROUPS))],
        out_specs=[pl.BlockSpec((m, tk), lambda k: (0, k)),
                   full(ATTN_WIDTH), full(ATTN_WIDTH), full(ATTN_WIDTH), full(GMLP_WIDTH), full(GMLP_WIDTH),
                   pl.BlockSpec((tk, n_out), lambda k: (k, 0))],
        out_shape=_mixer_shapes(m, d) + [jax.ShapeDtypeStruct((m, GMLP_WIDTH), F32),
                                         jax.ShapeDtypeStruct(w_in_f32.shape, BF16)],
        scratch_shapes=[pltpu.VMEM((m, 1), F32), pltpu.VMEM((m, n_out), F32)],
        compiler_params=_params("arbitrary"),
        name="mixer_cast",
    )(x, x, g_pre, w_in_f32, cos_t, sa_t, sb_t, g_vn, wsp, bsp)


def _moba_prompt_kernel(q_ref, k_ref, v_ref, *refs):
    n_cast = (len(refs) - 4) // 2
    o_ref = refs[n_cast]
    kb_scr, vt_scr, s_scr = refs[-3:]
    _cast_slabs(refs[:n_cast], refs[n_cast + 1:2 * n_cast + 1])
    seq = q_ref.shape[0]
    nblk = seq // MOBA_BLOCK
    blk = MOBA_BLOCK
    q_scale = HEAD_DIM ** -0.5 * LOG2_E
    kb_scr[...] = k_ref[...].astype(BF16)
    vt_scr[:HEAD_DIM, :] = v_ref[...].T.astype(BF16)
    vt_scr[HEAD_DIM:, :] = jnp.ones((BF16_SUBLANES, seq), BF16)
    qt = q_ref[...].T
    kt = k_ref[...].T

    blk_id = lax.broadcasted_iota(jnp.int32, (nblk, seq), 0)
    q_blk = lax.broadcasted_iota(jnp.int32, (nblk, seq), 1) // blk
    gate = jnp.zeros((nblk, seq), F32)
    for n in range(nblk - 1):
        kmean_n = jnp.mean(kt[:, n * blk:(n + 1) * blk], axis=1, keepdims=True)
        g_past = jnp.sum(qt[:, (n + 1) * blk:] * kmean_n, axis=0, keepdims=True)
        g_n = jnp.concatenate([jnp.zeros((1, (n + 1) * blk), F32), g_past], axis=1)
        gate = jnp.where(blk_id == n, g_n, gate)
    rank = jnp.zeros((nblk, seq), jnp.int32)
    for m in range(nblk):
        g_m = gate[m:m + 1, :]
        beats = (m < q_blk) & ((g_m > gate) | ((g_m == gate) & (m < blk_id)))
        rank = rank + beats.astype(jnp.int32)
    keep = jnp.where((blk_id < q_blk) & (rank < MOBA_TOPK), 1.0, 0.0)

    key_i = lax.broadcasted_iota(jnp.int32, (blk, blk), 0)
    qry_i = lax.broadcasted_iota(jnp.int32, (blk, blk), 1)
    causal = key_i <= qry_i

    for j in range(nblk):
        cols = slice(j * blk, (j + 1) * blk)
        qb = (q_ref[cols, :] * q_scale).astype(BF16)
        keep_j = keep[:, cols]
        m_run = None
        for n in range(j + 1):
            st = _dot_nt(kb_scr[n * blk:(n + 1) * blk, :], qb)
            if n == j:
                st = jnp.where(causal, st, NEG_INF)
            else:
                st = jnp.where(keep_j[n:n + 1, :] > 0.5, st, NEG_INF)
            s_scr[n] = st
            m_n = jnp.max(st, axis=0, keepdims=True)
            m_run = m_n if m_run is None else jnp.maximum(m_run, m_n)
        acc = jnp.zeros((HEAD_DIM + BF16_SUBLANES, blk), F32)
        for n in range(j + 1):
            p = jnp.exp2(s_scr[n] - m_run)
            acc = acc + _dot(vt_scr[:, n * blk:(n + 1) * blk], p.astype(BF16))
        o_ref[cols, :] = (acc[:HEAD_DIM, :] / acc[HEAD_DIM:HEAD_DIM + 1, :]).T.astype(BF16)


def _moba_prompt(q, k, v, *, batch, seq, casts=()):
    spec = pl.BlockSpec((seq, HEAD_DIM), lambda b, h: (b, h))
    slabs = _slab_specs(casts, batch * N_HEADS, lambda b, h: b * N_HEADS + h)
    out = pl.pallas_call(
        _moba_prompt_kernel,
        grid=(batch, N_HEADS),
        in_specs=[spec, spec, spec] + slabs,
        out_specs=[spec] + slabs,
        out_shape=[jax.ShapeDtypeStruct(q.shape, BF16)] + [jax.ShapeDtypeStruct(wt.shape, BF16) for wt in casts],
        scratch_shapes=[pltpu.VMEM((seq, HEAD_DIM), BF16), pltpu.VMEM((HEAD_DIM + BF16_SUBLANES, seq), BF16),
                        pltpu.VMEM((seq // MOBA_BLOCK, MOBA_BLOCK, MOBA_BLOCK), F32)],
        compiler_params=_params("parallel", "parallel"),
        name="moba_prompt",
    )(q, k, v, *casts)
    return out[0], tuple(out[1:])


def _scan_key_means(page_refs, km_ref, group):
    n_blk = len(page_refs) // 2
    page_rows = page_refs[0].shape[0] // N_HEADS

    def page_sum(ref):
        return jnp.sum(ref[...].reshape(page_rows, N_HEADS, HEAD_DIM), axis=0)

    for t in range(n_blk):
        tot = (page_sum(page_refs[2 * t]) + page_sum(page_refs[2 * t + 1])) * (1.0 / MOBA_BLOCK)
        for h in range(N_HEADS):
            km_ref[h, pl.ds(group * n_blk + t, 1), :] = tot[h:h + 1, :]


def _select_kernel(q_ref, km_ref, kn_ref, sel_ref, gate_scr, *, dec_s, n_past):
    dec_b = km_ref.shape[0]
    gate_scr[...] = jnp.full(gate_scr.shape, NEG_INF, F32)
    for b in range(dec_b):
        own_mean = jnp.sum(kn_ref[b * dec_s:(b + 1) * dec_s, :], axis=0, keepdims=True) * (1.0 / MOBA_BLOCK)
        for h in range(N_HEADS):
            cols = slice(h * HEAD_DIM, (h + 1) * HEAD_DIM)
            km = km_ref[b, h]
            for i in range(dec_s):
                col = (b * N_HEADS + h) * dec_s + i
                qi = q_ref[b * dec_s + i:b * dec_s + i + 1, cols]
                gate_scr[0:n_past, col:col + 1] = jnp.sum(km * qi, axis=-1, keepdims=True)
                gate_scr[n_past:n_past + 1, col:col + 1] = jnp.sum(qi * own_mean[:, cols], axis=-1,
                                                                   keepdims=True)
    gate = gate_scr[...]
    blk = lax.broadcasted_iota(jnp.int32, gate.shape, 0)
    gate = jnp.where(blk < n_past, gate, NEG_INF)
    out_row = lax.broadcasted_iota(jnp.int32, sel_ref.shape, 0)
    sel = jnp.zeros(sel_ref.shape, jnp.int32)
    for t in range(MOBA_TOPK):
        best = jnp.max(gate, axis=0, keepdims=True)
        idx = jnp.min(jnp.where(gate == best, blk, gate.shape[0]), axis=0, keepdims=True)
        sel = jnp.where(out_row == t, idx, sel)
        gate = jnp.where(blk == idx, TAKEN, gate)
    sel_ref[...] = sel


def _select_blocks(q2, kmean, kn2, *, dec_s, n_past):
    cols = q2.shape[0] * N_HEADS
    gate_rows = 8 * (-(-(n_past + 1) // 8))
    return pl.pallas_call(
        functools.partial(_select_kernel, dec_s=dec_s, n_past=n_past),
        out_shape=jax.ShapeDtypeStruct((8, cols), jnp.int32),
        scratch_shapes=[pltpu.VMEM((gate_rows, cols), F32)],
        compiler_params=pltpu.CompilerParams(vmem_limit_bytes=VMEM_LIMIT_BYTES),
        name="select_blocks",
    )(q2, kmean, kn2)


def _sample_fetch(step, n_steps, pt_ref, sel_ref, ck_hbm, cv_hbm, kbuf, vbuf, sem, *, dec_s, n_slots,
                  maybe_first, maybe_last):
    n_pages = 2 * n_slots

    def page_copies(step_idx, slot):
        bb, hh = step_idx // N_HEADS, step_idx % N_HEADS
        copies = []
        for c in range(n_pages):
            query, pick = (c // 2) // MOBA_TOPK, (c // 2) % MOBA_TOPK
            blk = sel_ref[pick, step_idx * dec_s + query]
            pg = pt_ref[bb, 2 * blk + c % 2]
            src = (pg, slice(None), hh, slice(None))
            copies.append(pltpu.make_async_copy(ck_hbm.at[src], kbuf.at[slot, c], sem.at[0, slot]))
            copies.append(pltpu.make_async_copy(cv_hbm.at[src], vbuf.at[slot, c], sem.at[1, slot]))
        return copies

    def start_next():
        for cp in page_copies(step + 1, (step + 1) % 2):
            cp.start()

    if maybe_first:
        @pl.when(step == 0)
        def _():
            for cp in page_copies(step, 0):
                cp.start()

    if maybe_last:
        pl.when(step + 1 < n_steps)(start_next)
    else:
        start_next()

    slot = step % 2
    for cp in page_copies(step, slot):
        cp.wait()
    return slot


def _sample_attend(slot, q_ref, kn_ref, vn_ref, o_ref, kbuf, vbuf, *, dec_s, n_slots):
    pages_per_query = 2 * MOBA_TOPK
    assert n_slots * 2 == dec_s * pages_per_query
    scale = HEAD_DIM ** -0.5
    kn, vn = kn_ref[...], vn_ref[...]
    new_row = lax.broadcasted_iota(jnp.int32, (dec_s, 1), 0)
    for i in range(dec_s):
        qi = q_ref[i:i + 1, :]
        pages = [i * pages_per_query + c for c in range(pages_per_query)]
        s_pages = [jnp.sum(kbuf[slot, c] * qi, axis=-1, keepdims=True) * scale for c in pages]
        s_own = jnp.where(new_row <= i, jnp.sum(kn * qi, axis=-1, keepdims=True) * scale, NEG_INF)
        m = jnp.max(s_own, axis=0, keepdims=True)
        for s in s_pages:
            m = jnp.maximum(m, jnp.max(s, axis=0, keepdims=True))
        p_own = jnp.exp(s_own - m)
        l = jnp.sum(p_own, axis=0, keepdims=True)
        acc = jnp.sum(p_own * vn, axis=0, keepdims=True)
        for c, s in zip(pages, s_pages):
            p = jnp.exp(s - m)
            l = l + jnp.sum(p, axis=0, keepdims=True)
            acc = acc + jnp.sum(p * vbuf[slot, c], axis=0, keepdims=True)
        o_ref[i:i + 1, :] = acc / l


def _gated_mix_kernel(a_ref, b_ref, h_ref, wa_ref, wb_ref, wga_ref, wgb_ref, *refs):
    n_cast = len(refs) // 2
    o_ref = refs[n_cast]
    _cast_slabs(refs[:n_cast], refs[n_cast + 1:])
    h = h_ref[...]
    ga = jax.nn.sigmoid(_dot(h, wga_ref[...]))
    gb = jax.nn.sigmoid(_dot(h, wgb_ref[...]))
    a = _dot(a_ref[...], wa_ref[...])
    b = _dot(b_ref[...], wb_ref[...])
    o_ref[...] = (ga * a + gb * b).astype(BF16)


def _gated_mix(attn, gm, h, w_a, w_b, w_gate, *, tm, tn, casts=()):
    m, d = h.shape
    n_col, n_row = d // tn, m // tm
    slabs = _slab_specs(casts, n_col * n_row, lambda j, i: j * n_row + i)
    out = pl.pallas_call(
        _gated_mix_kernel,
        grid=(n_col, n_row),
        in_specs=[pl.BlockSpec((tm, attn.shape[1]), lambda j, i: (i, 0)),
                  pl.BlockSpec((tm, gm.shape[1]), lambda j, i: (i, 0)),
                  pl.BlockSpec((tm, d), lambda j, i: (i, 0)),
                  pl.BlockSpec((w_a.shape[0], tn), lambda j, i: (0, j)),
                  pl.BlockSpec((w_b.shape[0], tn), lambda j, i: (0, j)),
                  pl.BlockSpec((d, tn), lambda j, i: (0, j)),
                  pl.BlockSpec((d, tn), lambda j, i: (0, j + n_col))] + slabs,
        out_specs=[pl.BlockSpec((tm, tn), lambda j, i: (i, j))] + slabs,
        out_shape=[jax.ShapeDtypeStruct((m, d), BF16)] + [jax.ShapeDtypeStruct(wt.shape, BF16) for wt in casts],
        compiler_params=_params("parallel", "parallel"),
        name="gated_mix",
    )(attn, gm, h, w_a, w_b, w_gate, w_gate, *casts)
    return out[0], tuple(out[1:])


def _row_chunks(tm, rc):
    rc = min(tm, rc)
    return [slice(r, r + rc) for r in range(0, tm, rc)]


def _mix_out_kernel(mix_ref, x_ref, wo_ref, gpost_ref, gpre_ref, x1_ref, xn_ref):
    for rows in _row_chunks(x_ref.shape[0], NORM_ROW_CHUNK):
        mix = _dot(mix_ref[rows, :], wo_ref[...])
        x1 = x_ref[rows, :] + _rms(mix, gpost_ref[...])
        x1_ref[rows, :] = x1
        xn_ref[rows, :] = _rms(x1, gpre_ref[...]).astype(BF16)


def _mix_out(mixin, x, w_o, g_post, g_pre_ffn, *, tm):
    m, d = x.shape
    row = pl.BlockSpec((tm, d), lambda i: (i, 0))
    vec = pl.BlockSpec((1, d), lambda i: (0, 0))
    return pl.pallas_call(
        _mix_out_kernel,
        grid=(m // tm,),
        in_specs=[row, row, pl.BlockSpec((d, d), lambda i: (0, 0), pipeline_mode=pl.Buffered(1)), vec, vec],
        out_specs=[row, row],
        out_shape=[jax.ShapeDtypeStruct((m, d), F32), jax.ShapeDtypeStruct((m, d), BF16)],
        compiler_params=_params("parallel"),
        name="mix_out",
    )(mixin, x, w_o, g_post, g_pre_ffn)


def _ffn_in_kernel(x_ref, wa_ref, wg_ref, o_ref):
    x = x_ref[...]
    a = _dot(x, wa_ref[...])
    g = _dot(x, wg_ref[...])
    o_ref[...] = (jax.nn.silu(a) * g).astype(BF16)


def _ffn_in_scan_kernel(pt_ref, x_ref, wa_ref, wg_ref, *refs, groups):
    page_refs, o_ref, km_ref = refs[:-2], refs[-2], refs[-1]
    step = pl.program_id(0) * pl.num_programs(1) + pl.program_id(1)
    _scan_key_means(page_refs, km_ref, step % groups)
    _ffn_in_kernel(x_ref, wa_ref, wg_ref, o_ref)


def _ffn_in(xn, w_ffn_in, *, tm, tn, scan=None):
    m, d = xn.shape
    hidden = w_ffn_in.shape[1] // 2
    n_col, n_row = hidden // tn, m // tm
    in_specs = [pl.BlockSpec((tm, d), lambda j, i, *_: (i, 0)),
                pl.BlockSpec((d, tn), lambda j, i, *_: (0, j)),
                pl.BlockSpec((d, tn), lambda j, i, *_: (0, j + n_col))]
    out_spec = pl.BlockSpec((tm, tn), lambda j, i, *_: (i, j))
    out_shape = jax.ShapeDtypeStruct((m, hidden), BF16)
    if scan is None:
        return pl.pallas_call(
            _ffn_in_kernel, grid=(n_col, n_row), in_specs=in_specs, out_specs=out_spec, out_shape=out_shape,
            compiler_params=_params("parallel", "parallel"), name="ffn_in",
        )(xn, w_ffn_in, w_ffn_in)

    cache_k2, page_table = scan
    dec_b, n_pages = page_table.shape
    n_steps = n_col * n_row
    pps = dec_b * n_pages // n_steps
    groups = n_pages // pps
    assert pps * n_steps == dec_b * n_pages and groups * pps == n_pages and pps % 2 == 0
    n_past = n_pages // 2

    def page_spec(c):
        def idx(j, i, pt):
            step = j * n_row + i
            return (pt[step // groups, (step % groups) * pps + c], 0, 0)
        return pl.BlockSpec((None,) + cache_k2.shape[1:], idx)

    grid_spec = pltpu.PrefetchScalarGridSpec(
        num_scalar_prefetch=1,
        grid=(n_col, n_row),
        in_specs=in_specs + [page_spec(c) for c in range(pps)],
        out_specs=[out_spec,
                   pl.BlockSpec((None, N_HEADS, n_past, HEAD_DIM),
                                lambda j, i, pt: ((j * n_row + i) // groups, 0, 0, 0))],
    )
    return pl.pallas_call(
        functools.partial(_ffn_in_scan_kernel, groups=groups),
        grid_spec=grid_spec,
        out_shape=[out_shape, jax.ShapeDtypeStruct((dec_b, N_HEADS, n_past, HEAD_DIM), F32)],
        compiler_params=_params("arbitrary", "arbitrary"),
        name="ffn_in_scan",
    )(page_table, xn, w_ffn_in, w_ffn_in, *([cache_k2] * pps))


def _ffn_out_ple_kernel(hm_ref, w_ref, x1_ref, gpost_ref, gple_ref, p_ref, wpg_ref, wp_ref, y_ref, acc_scr,
                        side_work=lambda maybe_first, maybe_last: None):
    k = pl.program_id(1)
    last = pl.num_programs(1) - 1

    @pl.when(k == 0)
    def _():
        side_work(maybe_first=True, maybe_last=False)
        acc_scr[...] = _dot(hm_ref[...], w_ref[...])

    @pl.when((k > 0) & (k < last))
    def _():
        side_work(maybe_first=False, maybe_last=False)
        acc_scr[...] += _dot(hm_ref[...], w_ref[...])

    @pl.when(k == last)
    def _():
        side_work(maybe_first=False, maybe_last=True)
        for rows in _row_chunks(x1_ref.shape[0], NORM_ROW_CHUNK):
            f = acc_scr[rows, :] + _dot(hm_ref[rows, :], w_ref[...])
            x2 = x1_ref[rows, :] + _rms(f, gpost_ref[...])
            xg = _rms(x2, gple_ref[...]).astype(BF16)
            gate = jax.nn.sigmoid(_dot(xg, wpg_ref[...]))
            y_ref[rows, :] = x2 + _dot(p_ref[rows, :].astype(BF16), wp_ref[...]) * gate


def _ffn_out_ple_attend_kernel(pt_ref, sel_ref, hm_ref, w_ref, x1_ref, gpost_ref, gple_ref, p_ref, wpg_ref,
                               wp_ref, q_ref, kn_ref, vn_ref, ck_hbm, cv_hbm, y_ref, a_ref,
                               acc_scr, kbuf, vbuf, sem, *, dec_s, n_slots):
    step = pl.program_id(0) * pl.num_programs(1) + pl.program_id(1)
    n_steps = pl.num_programs(0) * pl.num_programs(1)

    def attend(maybe_first, maybe_last):
        slot = _sample_fetch(step, n_steps, pt_ref, sel_ref, ck_hbm, cv_hbm, kbuf, vbuf, sem, dec_s=dec_s,
                             n_slots=n_slots, maybe_first=maybe_first, maybe_last=maybe_last)
        _sample_attend(slot, q_ref, kn_ref, vn_ref, a_ref, kbuf, vbuf, dec_s=dec_s, n_slots=n_slots)

    _ffn_out_ple_kernel(hm_ref, w_ref, x1_ref, gpost_ref, gple_ref, p_ref, wpg_ref, wp_ref, y_ref, acc_scr,
                        side_work=attend)


def _ffn_out_ple(hmid, w_ffn_out, x1, g_post_ffn, g_ple, p, w_ple_gate, w_ple, *, tm, tk, attend=None):
    m, d = x1.shape
    hidden = hmid.shape[1]
    n_row, n_k = m // tm, hidden // tk
    assert n_k >= 2
    row = pl.BlockSpec((tm, d), lambda i, k, *_: (i, 0))
    vec = pl.BlockSpec((1, d), lambda i, k, *_: (0, 0))
    const = lambda shape: pl.BlockSpec(shape, lambda i, k, *_: (0, 0), pipeline_mode=pl.Buffered(1))
    in_specs = [pl.BlockSpec((tm, tk), lambda i, k, *_: (i, k)),
                pl.BlockSpec((tk, d), lambda i, k, *_: (k, 0)),
                row, vec, vec,
                pl.BlockSpec((tm, p.shape[1]), lambda i, k, *_: (i, 0)),
                const(w_ple_gate.shape), const(w_ple.shape)]
    args = (hmid, w_ffn_out, x1, g_post_ffn, g_ple, p, w_ple_gate, w_ple)
    y_shape = jax.ShapeDtypeStruct((m, d), F32)
    acc = pltpu.VMEM((tm, d), F32)
    if attend is None:
        return pl.pallas_call(
            _ffn_out_ple_kernel, grid=(n_row, n_k), in_specs=in_specs, out_specs=row, out_shape=y_shape,
            scratch_shapes=[acc], compiler_params=_params("parallel", "arbitrary"), name="ffn_out_ple",
        )(*args)

    qkv, cache_k4, cache_v4, page_table, sel, dec_s = attend
    _, dec_b, rows, width = qkv.shape
    page = cache_k4.shape[1]
    n_slots = dec_s * MOBA_TOPK
    assert n_row * n_k == dec_b * N_HEADS
    pair_idx = lambda i, k: ((i * n_k + k) // N_HEADS, 0, (i * n_k + k) % N_HEADS)
    pair_spec = pl.BlockSpec((None, rows, HEAD_DIM), lambda i, k, *_: pair_idx(i, k))
    part_spec = lambda part: pl.BlockSpec((None, None, rows, HEAD_DIM), lambda i, k, *_: (part,) + pair_idx(i, k))
    hbm_spec = pl.BlockSpec(memory_space=pl.ANY)
    grid_spec = pltpu.PrefetchScalarGridSpec(
        num_scalar_prefetch=2,
        grid=(n_row, n_k),
        in_specs=in_specs + [part_spec(0), part_spec(1), part_spec(2), hbm_spec, hbm_spec],
        out_specs=[row, pair_spec],
        scratch_shapes=[acc,
                        pltpu.VMEM((2, 2 * n_slots, page, HEAD_DIM), F32),
                        pltpu.VMEM((2, 2 * n_slots, page, HEAD_DIM), F32),
                        pltpu.SemaphoreType.DMA((2, 2))],
    )
    return pl.pallas_call(
        functools.partial(_ffn_out_ple_attend_kernel, dec_s=dec_s, n_slots=n_slots),
        grid_spec=grid_spec,
        out_shape=[y_shape, jax.ShapeDtypeStruct((dec_b, rows, width), F32)],
        compiler_params=_params("arbitrary", "arbitrary"),
        name="ffn_out_ple_attend",
    )(page_table, sel, *args, qkv, qkv, qkv, cache_k4, cache_v4)


def _tail_front(x, h, attn, gm, w, *, tm, scan=None, f32_weights=None):
    m, d = x.shape
    f32_weights = f32_weights or {}
    mixin, cast = _gated_mix(attn, gm, h, w["w_a_out"], w["w_b_out"], w["w_gate"], tm=min(m, 1024), tn=512,
                             casts=tuple(f32_weights.values()))
    w.update(zip(f32_weights.keys(), cast))
    hidden = w["w_ffn_out"].shape[0]
    x1, xn = _mix_out(mixin, x, w["w_o"], w["g_post_mix"], w["g_pre_ffn"], tm=tm)
    if scan is None:
        return x1, _ffn_in(xn, w["w_ffn_in"], tm=tm, tn=hidden // 4), None
    hmid, kmean = _ffn_in(xn, w["w_ffn_in"], tm=tm, tn=hidden // 4, scan=scan)
    return x1, hmid, kmean


def _tail_back(x1, hmid, p, w, *, tm, attend=None):
    return _ffn_out_ple(hmid, w["w_ffn_out"], x1, w["g_post_ffn"], w["g_ple"], p, w["w_ple_gate"], w["w_ple"],
                        tm=tm, tk=w["w_ffn_out"].shape[0] // 4, attend=attend)


def _rope_tables(pos):
    lane = jnp.arange(HEAD_DIM, dtype=jnp.int32)
    freqs = jnp.power(jnp.float32(ROPE_THETA), -2.0 * (lane % ROPE_HALF).astype(F32) / ROPE_DIM)
    ang = pos.astype(F32)[:, None] * freqs[None, :]
    cos, sin = jnp.cos(ang), jnp.sin(ang)
    c = jnp.where(lane < ROPE_DIM, cos, 1.0)
    sa = jnp.where(lane < ROPE_HALF, -sin, 0.0)
    sb = jnp.where((lane >= ROPE_HALF) & (lane < ROPE_DIM), sin, 0.0)
    return c, sa, sb


def kernel(x_prompt, x_sample, cache_k, cache_v, page_table, p_prompt, p_sample, g_pre_mix, w_in, g_vnorm, w_spatial, b_spatial, w_a_out, w_b_out, w_gate, w_o, g_post_mix, g_pre_ffn, w_ffn_in, w_ffn_out, g_post_ffn, g_ple, w_ple_gate, w_ple):
    batch, seq, d = x_prompt.shape
    dec_b, dec_s, _ = x_sample.shape
    depth = w_in.shape[0]
    page = cache_k.shape[2]
    past_len = page_table.shape[1] * page
    n_past = past_len // MOBA_BLOCK
    assert depth == 1 and seq % MOBA_BLOCK == 0 and past_len % MOBA_BLOCK == 0 and MOBA_BLOCK == 2 * page
    assert dec_s <= GMLP_CHUNK and n_past >= MOBA_TOPK
    l = 0

    w = {
        "w_ple": w_ple[l].astype(BF16),
        "g_post_mix": g_post_mix[l][None], "g_pre_ffn": g_pre_ffn[l][None],
        "g_post_ffn": g_post_ffn[l][None], "g_ple": g_ple[l][None],
    }
    tail_weights = {"w_a_out": w_a_out[l], "w_b_out": w_b_out[l], "w_gate": w_gate[l], "w_o": w_o[l],
                    "w_ffn_out": w_ffn_out[l], "w_ple_gate": w_ple_gate[l]}
    ffn_in_weight = {"w_ffn_in": w_ffn_in[l]}
    g_pre = g_pre_mix[l][None]
    g_vn = g_vnorm[l][None]

    w_tril = jnp.tril(w_spatial[l])
    wsp_p = w_tril.astype(BF16)
    bsp_p = b_spatial[l].T
    eye_b = jnp.eye(dec_b, dtype=F32)
    wsp_s = jnp.einsum("ab,gts->gatbs", eye_b, w_tril[:, :dec_s, :dec_s]).reshape(
        GMLP_GROUPS, dec_b * dec_s, dec_b * dec_s).astype(BF16)
    bsp_s = jnp.tile(b_spatial[l][:, :dec_s].T, (dec_b, 1))

    ms = dec_b * dec_s
    xs = x_sample.reshape(ms, d)
    pos_s = past_len + jnp.arange(dec_s, dtype=jnp.int32)
    cs, sas, sbs = _rope_tables(jnp.tile(pos_s, dec_b))
    hs, qs, ks, vs, gms, vns, w_in_b = _mixer_cast(xs, g_pre, w_in[l], cs, sas, sbs, g_vn, wsp_s, bsp_s,
                                                   tk=W_IN_CAST_ROWS)
    mp = batch * seq
    xp = x_prompt.reshape(mp, d)
    cp, sap, sbp = _rope_tables(jnp.arange(seq, dtype=jnp.int32))
    hp, qp, kp, vp, gmp = _mixer(xp, g_pre, w_in_b, cp, sap, sbp, g_vn, wsp_p, bsp_p,
                                 tm=MIXER_ROWS, chunk=GMLP_CHUNK)

    ap, cast = _moba_prompt(qp, kp, vp, batch=batch, seq=seq, casts=tuple(tail_weights.values()))
    w.update(zip(tail_weights.keys(), cast))
    n_pool = depth * cache_k.shape[1]
    cache_k2 = cache_k.reshape(n_pool, page * N_HEADS, HEAD_DIM)
    x1p, hmid_p, kmean = _tail_front(xp, hp, ap, gmp, w, tm=512, scan=(cache_k2, page_table),
                                     f32_weights=ffn_in_weight)
    sel = _select_blocks(qs, kmean, ks, dec_s=dec_s, n_past=n_past)
    qkv = jnp.stack([qs, ks, vs]).reshape(3, dec_b, dec_s, ATTN_WIDTH)
    attend = (qkv, cache_k.reshape(n_pool, page, N_HEADS, HEAD_DIM),
              cache_v.reshape(n_pool, page, N_HEADS, HEAD_DIM), page_table, sel, dec_s)
    yp, a_s = _tail_back(x1p, hmid_p, p_prompt[l].reshape(mp, -1), w, tm=512, attend=attend)

    a_s = a_s.reshape(ms, ATTN_WIDTH).astype(BF16)
    x1s, hmid_s, _ = _tail_front(xs, hs, a_s, gms, w, tm=ms)
    ys = _tail_back(x1s, hmid_s, p_sample[l].reshape(ms, -1), w, tm=ms)

    return (yp.reshape(batch, seq, d), ys.reshape(dec_b, dec_s, d),
            kp.reshape(1, batch, seq, N_HEADS, HEAD_DIM), vp.reshape(1, batch, seq, N_HEADS, HEAD_DIM),
            ks.reshape(1, dec_b, dec_s, N_HEADS, HEAD_DIM), vs.reshape(1, dec_b, dec_s, N_HEADS, HEAD_DIM),
            vns.reshape(1, dec_b, dec_s, GMLP_WIDTH))
```

```python
import functools

import jax
import jax.numpy as jnp
from jax import lax
from jax.experimental import pallas as pl
from jax.experimental.pallas import tpu as pltpu

F32 = jnp.float32
BF16 = jnp.bfloat16

N_HEADS = 8
HEAD_DIM = 128
ATTN_WIDTH = N_HEADS * HEAD_DIM
MOBA_BLOCK = 256
MOBA_TOPK = 3
ROPE_THETA = 500000.0
ROPE_DIM = HEAD_DIM // 4
ROPE_HALF = ROPE_DIM // 2
GMLP_GROUPS = 8
GMLP_CHUNK = 128
GMLP_WIDTH = 1024
GMLP_GROUP_DIM = GMLP_WIDTH // GMLP_GROUPS
NORM_EPS = 1e-6
NEG_INF = -1e30
LOG2_E = 1.4426950408889634
TAKEN = -3e38
MIXER_ROWS = 256
W_IN_CAST_ROWS = 256
FFN_COL_BLOCKS = 4
NORM_ROW_CHUNK = 256
BF16_SUBLANES = 16

VMEM_LIMIT_BYTES = 56 * 1024 * 1024


def _params(*semantics):
    return pltpu.CompilerParams(dimension_semantics=semantics, vmem_limit_bytes=VMEM_LIMIT_BYTES)


def _rms(x, g):
    return x * lax.rsqrt(jnp.mean(x * x, axis=-1, keepdims=True) + NORM_EPS) * g


def _dot(a, b):
    return jnp.dot(a, b, preferred_element_type=F32)


def _slab_specs(weights, n_steps, step_of):
    specs = []
    for wt in weights:
        rows = wt.shape[0] // n_steps
        assert rows * n_steps == wt.shape[0] and rows % BF16_SUBLANES == 0
        specs.append(pl.BlockSpec((rows, wt.shape[1]), lambda *g: (step_of(*g), 0)))
    return specs


def _cast_slabs(src_refs, dst_refs, pair_width=None):
    for src_ref, dst_ref in zip(src_refs, dst_refs):
        if pair_width is None:
            dst_ref[...] = src_ref[...].astype(BF16)
            continue
        half = src_ref.shape[1] // 2
        for j in range(half // pair_width):
            for part in range(2):
                src_cols = slice(part * half + j * pair_width, part * half + (j + 1) * pair_width)
                dst_cols = slice((2 * j + part) * pair_width, (2 * j + part + 1) * pair_width)
                dst_ref[:, dst_cols] = src_ref[:, src_cols].astype(BF16)


def _dot_nt(a, b):
    return lax.dot_general(a, b, (((1,), (1,)), ((), ())), preferred_element_type=F32)


def _mixer_outputs(proj, tm, cos_ref, sa_ref, sb_ref, gvn_ref, wsp_ref, bsp_ref,
                   q_out, k_out, v_out, gm_out, vn_out, *, chunk):

    def rope_to(out_ref, z):
        c, sa, sb = cos_ref[...], sa_ref[...], sb_ref[...]
        for hd in range(N_HEADS):
            sl = slice(hd * HEAD_DIM, (hd + 1) * HEAD_DIM)
            zs = z[:, sl]
            out_ref[:, sl] = (zs * c + pltpu.roll(zs, HEAD_DIM - ROPE_HALF, 1) * sa
                              + pltpu.roll(zs, ROPE_HALF, 1) * sb)

    rope_to(q_out, proj(0))
    rope_to(k_out, proj(1))
    v_out[...] = proj(2)
    u = jax.nn.gelu(proj(3))
    vg = jax.nn.gelu(proj(4))
    xc = vg - jnp.mean(vg, axis=-1, keepdims=True)
    vn = xc * lax.rsqrt(jnp.mean(xc * xc, axis=-1, keepdims=True) + NORM_EPS) * gvn_ref[...]
    if vn_out is not None:
        vn_out[...] = vn
    vnb = vn.astype(BF16)
    for c in range(tm // chunk):
        rows = slice(c * chunk, (c + 1) * chunk)
        for g in range(GMLP_GROUPS):
            cols = slice(g * GMLP_GROUP_DIM, (g + 1) * GMLP_GROUP_DIM)
            s = _dot(wsp_ref[g], vnb[rows, cols]) + bsp_ref[:, g:g + 1]
            gm_out[rows, cols] = (u[rows, cols] * s).astype(BF16)


def _mixer_kernel(x_ref, g_ref, w_ref, cos_ref, sa_ref, sb_ref, gvn_ref, wsp_ref, bsp_ref,
                  h_out, q_out, k_out, v_out, gm_out, *, chunk):
    hb = _rms(x_ref[...], g_ref[...]).astype(BF16)
    h_out[...] = hb
    proj = lambda s: _dot(hb, w_ref[:, s * ATTN_WIDTH:(s + 1) * ATTN_WIDTH])
    _mixer_outputs(proj, x_ref.shape[0], cos_ref, sa_ref, sb_ref, gvn_ref, wsp_ref, bsp_ref,
                   q_out, k_out, v_out, gm_out, None, chunk=chunk)


def _mixer_cast_kernel(x_ref, xk_ref, gk_ref, w_ref, cos_ref, sa_ref, sb_ref, gvn_ref, wsp_ref, bsp_ref,
                       h_out, q_out, k_out, v_out, gm_out, vn_out, wb_out, inv_scr, acc_scr, *, chunk):
    k = pl.program_id(0)

    @pl.when(k == 0)
    def _():
        x = x_ref[...]
        inv_scr[...] = lax.rsqrt(jnp.mean(x * x, axis=-1, keepdims=True) + NORM_EPS)
        acc_scr[...] = jnp.zeros(acc_scr.shape, F32)

    wb = w_ref[...].astype(BF16)
    wb_out[...] = wb
    hk = (xk_ref[...] * inv_scr[...] * gk_ref[...]).astype(BF16)
    h_out[...] = hk
    acc_scr[...] += _dot(hk, wb)

    @pl.when(k == pl.num_programs(0) - 1)
    def _():
        proj = lambda s: acc_scr[:, s * ATTN_WIDTH:(s + 1) * ATTN_WIDTH]
        _mixer_outputs(proj, x_ref.shape[0], cos_ref, sa_ref, sb_ref, gvn_ref, wsp_ref, bsp_ref,
                       q_out, k_out, v_out, gm_out, vn_out, chunk=chunk)


def _mixer_shapes(m, d):
    return [jax.ShapeDtypeStruct((m, d), BF16),
            jax.ShapeDtypeStruct((m, ATTN_WIDTH), F32),
            jax.ShapeDtypeStruct((m, ATTN_WIDTH), F32),
            jax.ShapeDtypeStruct((m, ATTN_WIDTH), F32),
            jax.ShapeDtypeStruct((m, GMLP_WIDTH), BF16)]


def _mixer(x, g_pre, w_in, cos_t, sa_t, sb_t, g_vn, wsp, bsp, *, tm, chunk):
    m, d = x.shape
    assert w_in.shape[1] == 3 * ATTN_WIDTH + 2 * GMLP_WIDTH and ATTN_WIDTH == GMLP_WIDTH
    t_blocks = cos_t.shape[0] // tm
    tab_spec = pl.BlockSpec((tm, HEAD_DIM), lambda i: (i % t_blocks, 0))
    row_spec = lambda w: pl.BlockSpec((tm, w), lambda i: (i, 0))
    const = lambda shape: pl.BlockSpec(shape, lambda i: (0,) * len(shape))
    return pl.pallas_call(
        functools.partial(_mixer_kernel, chunk=chunk),
        grid=(m // tm,),
        in_specs=[row_spec(d),
                  const((1, d)),
                  pl.BlockSpec(w_in.shape, lambda i: (0, 0), pipeline_mode=pl.Buffered(1)),
                  tab_spec, tab_spec, tab_spec,
                  const((1, GMLP_WIDTH)),
                  const((GMLP_GROUPS, chunk, chunk)),
                  const((chunk, GMLP_GROUPS))],
        out_specs=[row_spec(d), row_spec(ATTN_WIDTH), row_spec(ATTN_WIDTH), row_spec(ATTN_WIDTH),
                   row_spec(GMLP_WIDTH)],
        out_shape=_mixer_shapes(m, d),
        compiler_params=_params("parallel"),
        name="mixer",
    )(x, g_pre, w_in, cos_t, sa_t, sb_t, g_vn, wsp, bsp)


def _mixer_cast(x, g_pre, w_in_f32, cos_t, sa_t, sb_t, g_vn, wsp, bsp, *, tk):
    m, d = x.shape
    n_out = w_in_f32.shape[1]
    assert n_out == 3 * ATTN_WIDTH + 2 * GMLP_WIDTH and ATTN_WIDTH == GMLP_WIDTH and d % tk == 0
    const = lambda shape: pl.BlockSpec(shape, lambda k: (0,) * len(shape))
    full = lambda w: const((m, w))
    return pl.pallas_call(
        functools.partial(_mixer_cast_kernel, chunk=m),
        grid=(d // tk,),
        in_specs=[full(d),
                  pl.BlockSpec((m, tk), lambda k: (0, k)),
                  pl.BlockSpec((1, tk), lambda k: (0, k)),
                  pl.BlockSpec((tk, n_out), lambda k: (k, 0)),
                  full(HEAD_DIM), full(HEAD_DIM), full(HEAD_DIM),
                  const((1, GMLP_WIDTH)),
                  const((GMLP_GROUPS, m, m)),
                  const((m, GMLP_GROUPS))],
        out_specs=[pl.BlockSpec((m, tk), lambda k: (0, k)),
                   full(ATTN_WIDTH), full(ATTN_WIDTH), full(ATTN_WIDTH), full(GMLP_WIDTH), full(GMLP_WIDTH),
                   pl.BlockSpec((tk, n_out), lambda k: (k, 0))],
        out_shape=_mixer_shapes(m, d) + [jax.ShapeDtypeStruct((m, GMLP_WIDTH), F32),
                                         jax.ShapeDtypeStruct(w_in_f32.shape, BF16)],
        scratch_shapes=[pltpu.VMEM((m, 1), F32), pltpu.VMEM((m, n_out), F32)],
        compiler_params=_params("arbitrary"),
        name="mixer_cast",
    )(x, x, g_pre, w_in_f32, cos_t, sa_t, sb_t, g_vn, wsp, bsp)


def _moba_prompt_kernel(q_ref, k_ref, v_ref, *refs):
    n_cast = (len(refs) - 4) // 2
    o_ref = refs[n_cast]
    kb_scr, vt_scr, s_scr = refs[-3:]
    _cast_slabs(refs[:n_cast], refs[n_cast + 1:2 * n_cast + 1])
    seq = q_ref.shape[0]
    nblk = seq // MOBA_BLOCK
    blk = MOBA_BLOCK
    q_scale = HEAD_DIM ** -0.5 * LOG2_E
    kb_scr[...] = k_ref[...].astype(BF16)
    vt_scr[:HEAD_DIM, :] = v_ref[...].T.astype(BF16)
    vt_scr[HEAD_DIM:, :] = jnp.ones((BF16_SUBLANES, seq), BF16)
    qt = q_ref[...].T
    kt = k_ref[...].T

    blk_id = lax.broadcasted_iota(jnp.int32, (nblk, seq), 0)
    q_blk = lax.broadcasted_iota(jnp.int32, (nblk, seq), 1) // blk
    gate = jnp.zeros((nblk, seq), F32)
    for n in range(nblk - 1):
        kmean_n = jnp.mean(kt[:, n * blk:(n + 1) * blk], axis=1, keepdims=True)
        g_past = jnp.sum(qt[:, (n + 1) * blk:] * kmean_n, axis=0, keepdims=True)
        g_n = jnp.concatenate([jnp.zeros((1, (n + 1) * blk), F32), g_past], axis=1)
        gate = jnp.where(blk_id == n, g_n, gate)
    rank = jnp.zeros((nblk, seq), jnp.int32)
    for m in range(nblk):
        g_m = gate[m:m + 1, :]
        beats = (m < q_blk) & ((g_m > gate) | ((g_m == gate) & (m < blk_id)))
        rank = rank + beats.astype(jnp.int32)
    keep = jnp.where((blk_id < q_blk) & (rank < MOBA_TOPK), 1.0, 0.0)

    key_i = lax.broadcasted_iota(jnp.int32, (blk, blk), 0)
    qry_i = lax.broadcasted_iota(jnp.int32, (blk, blk), 1)
    causal = key_i <= qry_i

    for j in range(nblk):
        cols = slice(j * blk, (j + 1) * blk)
        qb = (q_ref[cols, :] * q_scale).astype(BF16)
        keep_j = keep[:, cols]
        m_run = None
        for n in range(j + 1):
            st = _dot_nt(kb_scr[n * blk:(n + 1) * blk, :], qb)
            if n == j:
                st = jnp.where(causal, st, NEG_INF)
            else:
                st = jnp.where(keep_j[n:n + 1, :] > 0.5, st, NEG_INF)
            s_scr[n] = st
            m_n = jnp.max(st, axis=0, keepdims=True)
            m_run = m_n if m_run is None else jnp.maximum(m_run, m_n)
        acc = jnp.zeros((HEAD_DIM + BF16_SUBLANES, blk), F32)
        for n in range(j + 1):
            p = jnp.exp2(s_scr[n] - m_run)
            acc = acc + _dot(vt_scr[:, n * blk:(n + 1) * blk], p.astype(BF16))
        o_ref[cols, :] = (acc[:HEAD_DIM, :] / acc[HEAD_DIM:HEAD_DIM + 1, :]).T.astype(BF16)


def _moba_prompt(q, k, v, *, batch, seq, casts=()):
    spec = pl.BlockSpec((seq, HEAD_DIM), lambda b, h: (b, h))
    slabs = _slab_specs(casts, batch * N_HEADS, lambda b, h: b * N_HEADS + h)
    out = pl.pallas_call(
        _moba_prompt_kernel,
        grid=(batch, N_HEADS),
        in_specs=[spec, spec, spec] + slabs,
        out_specs=[spec] + slabs,
        out_shape=[jax.ShapeDtypeStruct(q.shape, BF16)] + [jax.ShapeDtypeStruct(wt.shape, BF16) for wt in casts],
        scratch_shapes=[pltpu.VMEM((seq, HEAD_DIM), BF16), pltpu.VMEM((HEAD_DIM + BF16_SUBLANES, seq), BF16),
                        pltpu.VMEM((seq // MOBA_BLOCK, MOBA_BLOCK, MOBA_BLOCK), F32)],
        compiler_params=_params("parallel", "parallel"),
        name="moba_prompt",
    )(q, k, v, *casts)
    return out[0], tuple(out[1:])


def _scan_key_means(page_refs, km_ref, group):
    n_blk = len(page_refs) // 2
    page_rows = page_refs[0].shape[0] // N_HEADS

    def page_sum(ref):
        return jnp.sum(ref[...].reshape(page_rows, N_HEADS, HEAD_DIM), axis=0)

    for t in range(n_blk):
        tot = (page_sum(page_refs[2 * t]) + page_sum(page_refs[2 * t + 1])) * (1.0 / MOBA_BLOCK)
        for h in range(N_HEADS):
            km_ref[h, pl.ds(group * n_blk + t, 1), :] = tot[h:h + 1, :]


def _select_kernel(q_ref, km_ref, kn_ref, sel_ref, gate_scr, *, dec_s, n_past):
    dec_b = km_ref.shape[0]
    gate_scr[...] = jnp.full(gate_scr.shape, NEG_INF, F32)
    for b in range(dec_b):
        own_mean = jnp.sum(kn_ref[b * dec_s:(b + 1) * dec_s, :], axis=0, keepdims=True) * (1.0 / MOBA_BLOCK)
        for h in range(N_HEADS):
            cols = slice(h * HEAD_DIM, (h + 1) * HEAD_DIM)
            km = km_ref[b, h]
            for i in range(dec_s):
                col = (b * N_HEADS + h) * dec_s + i
                qi = q_ref[b * dec_s + i:b * dec_s + i + 1, cols]
                gate_scr[0:n_past, col:col + 1] = jnp.sum(km * qi, axis=-1, keepdims=True)
                gate_scr[n_past:n_past + 1, col:col + 1] = jnp.sum(qi * own_mean[:, cols], axis=-1,
                                                                   keepdims=True)
    gate = gate_scr[...]
    blk = lax.broadcasted_iota(jnp.int32, gate.shape, 0)
    gate = jnp.where(blk < n_past, gate, NEG_INF)
    out_row = lax.broadcasted_iota(jnp.int32, sel_ref.shape, 0)
    sel = jnp.zeros(sel_ref.shape, jnp.int32)
    for t in range(MOBA_TOPK):
        best = jnp.max(gate, axis=0, keepdims=True)
        idx = jnp.min(jnp.where(gate == best, blk, gate.shape[0]), axis=0, keepdims=True)
        sel = jnp.where(out_row == t, idx, sel)
        gate = jnp.where(blk == idx, TAKEN, gate)
    sel_ref[...] = sel


def _select_blocks(q2, kmean, kn2, *, dec_s, n_past):
    cols = q2.shape[0] * N_HEADS
    gate_rows = 8 * (-(-(n_past + 1) // 8))
    return pl.pallas_call(
        functools.partial(_select_kernel, dec_s=dec_s, n_past=n_past),
        out_shape=jax.ShapeDtypeStruct((8, cols), jnp.int32),
        scratch_shapes=[pltpu.VMEM((gate_rows, cols), F32)],
        compiler_params=pltpu.CompilerParams(vmem_limit_bytes=VMEM_LIMIT_BYTES),
        name="select_blocks",
    )(q2, kmean, kn2)


def _sample_fetch(step, n_steps, pt_ref, sel_ref, ck_hbm, cv_hbm, kbuf, vbuf, sem, *, dec_s, n_slots,
                  maybe_first, maybe_last):
    n_pages = 2 * n_slots

    def page_copies(step_idx, slot):
        bb, hh = step_idx // N_HEADS, step_idx % N_HEADS
        copies = []
        for c in range(n_pages):
            query, pick = (c // 2) // MOBA_TOPK, (c // 2) % MOBA_TOPK
            blk = sel_ref[pick, step_idx * dec_s + query]
            pg = pt_ref[bb, 2 * blk + c % 2]
            src = (pg, slice(None), hh, slice(None))
            copies.append(pltpu.make_async_copy(ck_hbm.at[src], kbuf.at[slot, c], sem.at[0, slot]))
            copies.append(pltpu.make_async_copy(cv_hbm.at[src], vbuf.at[slot, c], sem.at[1, slot]))
        return copies

    def start_next():
        for cp in page_copies(step + 1, (step + 1) % 2):
            cp.start()

    if maybe_first:
        @pl.when(step == 0)
        def _():
            for cp in page_copies(step, 0):
                cp.start()

    if maybe_last:
        pl.when(step + 1 < n_steps)(start_next)
    else:
        start_next()

    slot = step % 2
    for cp in page_copies(step, slot):
        cp.wait()
    return slot


def _sample_attend(slot, q_ref, kn_ref, vn_ref, o_ref, kbuf, vbuf, *, dec_s, n_slots):
    pages_per_query = 2 * MOBA_TOPK
    assert n_slots * 2 == dec_s * pages_per_query
    scale = HEAD_DIM ** -0.5
    kn, vn = kn_ref[...], vn_ref[...]
    new_row = lax.broadcasted_iota(jnp.int32, (dec_s, 1), 0)
    for i in range(dec_s):
        qi = q_ref[i:i + 1, :]
        pages = [i * pages_per_query + c for c in range(pages_per_query)]
        s_pages = [jnp.sum(kbuf[slot, c] * qi, axis=-1, keepdims=True) * scale for c in pages]
        s_own = jnp.where(new_row <= i, jnp.sum(kn * qi, axis=-1, keepdims=True) * scale, NEG_INF)
        m = jnp.max(s_own, axis=0, keepdims=True)
        for s in s_pages:
            m = jnp.maximum(m, jnp.max(s, axis=0, keepdims=True))
        p_own = jnp.exp(s_own - m)
        l = jnp.sum(p_own, axis=0, keepdims=True)
        acc = jnp.sum(p_own * vn, axis=0, keepdims=True)
        for c, s in zip(pages, s_pages):
            p = jnp.exp(s - m)
            l = l + jnp.sum(p, axis=0, keepdims=True)
            acc = acc + jnp.sum(p * vbuf[slot, c], axis=0, keepdims=True)
        o_ref[i:i + 1, :] = acc / l


def _gated_mix_kernel(a_ref, b_ref, h_ref, wa_ref, wb_ref, wga_ref, wgb_ref, *refs, pair_width):
    n_cast = len(refs) // 2
    o_ref = refs[n_cast]
    _cast_slabs(refs[:n_cast], refs[n_cast + 1:], pair_width)
    h = h_ref[...]
    ga = jax.nn.sigmoid(_dot(h, wga_ref[...]))
    gb = jax.nn.sigmoid(_dot(h, wgb_ref[...]))
    a = _dot(a_ref[...], wa_ref[...])
    b = _dot(b_ref[...], wb_ref[...])
    o_ref[...] = (ga * a + gb * b).astype(BF16)


def _gated_mix(attn, gm, h, w_a, w_b, w_gate, *, tm, tn, casts=(), pair_width=None):
    m, d = h.shape
    n_col, n_row = d // tn, m // tm
    slabs = _slab_specs(casts, n_col * n_row, lambda j, i: j * n_row + i)
    out = pl.pallas_call(
        functools.partial(_gated_mix_kernel, pair_width=pair_width),
        grid=(n_col, n_row),
        in_specs=[pl.BlockSpec((tm, attn.shape[1]), lambda j, i: (i, 0)),
                  pl.BlockSpec((tm, gm.shape[1]), lambda j, i: (i, 0)),
                  pl.BlockSpec((tm, d), lambda j, i: (i, 0)),
                  pl.BlockSpec((w_a.shape[0], tn), lambda j, i: (0, j)),
                  pl.BlockSpec((w_b.shape[0], tn), lambda j, i: (0, j)),
                  pl.BlockSpec((d, tn), lambda j, i: (0, j)),
                  pl.BlockSpec((d, tn), lambda j, i: (0, j + n_col))] + slabs,
        out_specs=[pl.BlockSpec((tm, tn), lambda j, i: (i, j))] + slabs,
        out_shape=[jax.ShapeDtypeStruct((m, d), BF16)] + [jax.ShapeDtypeStruct(wt.shape, BF16) for wt in casts],
        compiler_params=_params("parallel", "parallel"),
        name="gated_mix",
    )(attn, gm, h, w_a, w_b, w_gate, w_gate, *casts)
    return out[0], tuple(out[1:])


def _row_chunks(tm, rc):
    rc = min(tm, rc)
    return [slice(r, r + rc) for r in range(0, tm, rc)]


def _mix_out_kernel(mix_ref, x_ref, wo_ref, gpost_ref, gpre_ref, x1_ref, xn_ref):
    for rows in _row_chunks(x_ref.shape[0], NORM_ROW_CHUNK):
        mix = _dot(mix_ref[rows, :], wo_ref[...])
        x1 = x_ref[rows, :] + _rms(mix, gpost_ref[...])
        x1_ref[rows, :] = x1
        xn_ref[rows, :] = _rms(x1, gpre_ref[...]).astype(BF16)


def _mix_out(mixin, x, w_o, g_post, g_pre_ffn, *, tm):
    m, d = x.shape
    row = pl.BlockSpec((tm, d), lambda i: (i, 0))
    vec = pl.BlockSpec((1, d), lambda i: (0, 0))
    return pl.pallas_call(
        _mix_out_kernel,
        grid=(m // tm,),
        in_specs=[row, row, pl.BlockSpec((d, d), lambda i: (0, 0), pipeline_mode=pl.Buffered(1)), vec, vec],
        out_specs=[row, row],
        out_shape=[jax.ShapeDtypeStruct((m, d), F32), jax.ShapeDtypeStruct((m, d), BF16)],
        compiler_params=_params("parallel"),
        name="mix_out",
    )(mixin, x, w_o, g_post, g_pre_ffn)


def _ffn_in_kernel(x_ref, w_ref, o_ref):
    tn = o_ref.shape[1]
    z = _dot(x_ref[...], w_ref[...])
    o_ref[...] = (jax.nn.silu(z[:, :tn]) * z[:, tn:]).astype(BF16)


def _ffn_in_scan_kernel(pt_ref, x_ref, w_ref, *refs, groups):
    page_refs, o_ref, km_ref = refs[:-2], refs[-2], refs[-1]
    step = pl.program_id(0) * pl.num_programs(1) + pl.program_id(1)
    _scan_key_means(page_refs, km_ref, step % groups)
    _ffn_in_kernel(x_ref, w_ref, o_ref)


def _ffn_in(xn, w_ffn_in, *, tm, tn, scan=None):
    m, d = xn.shape
    hidden = w_ffn_in.shape[1] // 2
    n_col, n_row = hidden // tn, m // tm
    in_specs = [pl.BlockSpec((tm, d), lambda j, i, *_: (i, 0)),
                pl.BlockSpec((d, 2 * tn), lambda j, i, *_: (0, j))]
    out_spec = pl.BlockSpec((tm, tn), lambda j, i, *_: (i, j))
    out_shape = jax.ShapeDtypeStruct((m, hidden), BF16)
    if scan is None:
        return pl.pallas_call(
            _ffn_in_kernel, grid=(n_col, n_row), in_specs=in_specs, out_specs=out_spec, out_shape=out_shape,
            compiler_params=_params("parallel", "parallel"), name="ffn_in",
        )(xn, w_ffn_in)

    cache_k2, page_table = scan
    dec_b, n_pages = page_table.shape
    n_steps = n_col * n_row
    pps = dec_b * n_pages // n_steps
    groups = n_pages // pps
    assert pps * n_steps == dec_b * n_pages and groups * pps == n_pages and pps % 2 == 0
    n_past = n_pages // 2

    def page_spec(c):
        def idx(j, i, pt):
            step = j * n_row + i
            return (pt[step // groups, (step % groups) * pps + c], 0, 0)
        return pl.BlockSpec((None,) + cache_k2.shape[1:], idx)

    grid_spec = pltpu.PrefetchScalarGridSpec(
        num_scalar_prefetch=1,
        grid=(n_col, n_row),
        in_specs=in_specs + [page_spec(c) for c in range(pps)],
        out_specs=[out_spec,
                   pl.BlockSpec((None, N_HEADS, n_past, HEAD_DIM),
                                lambda j, i, pt: ((j * n_row + i) // groups, 0, 0, 0))],
    )
    return pl.pallas_call(
        functools.partial(_ffn_in_scan_kernel, groups=groups),
        grid_spec=grid_spec,
        out_shape=[out_shape, jax.ShapeDtypeStruct((dec_b, N_HEADS, n_past, HEAD_DIM), F32)],
        compiler_params=_params("arbitrary", "arbitrary"),
        name="ffn_in_scan",
    )(page_table, xn, w_ffn_in, *([cache_k2] * pps))


def _ffn_out_ple_kernel(hm_ref, w_ref, x1_ref, gpost_ref, gple_ref, p_ref, wpg_ref, wp_ref, y_ref, acc_scr,
                        side_work=lambda maybe_first, maybe_last: None):
    k = pl.program_id(1)
    last = pl.num_programs(1) - 1

    @pl.when(k == 0)
    def _():
        side_work(maybe_first=True, maybe_last=False)
        acc_scr[...] = _dot(hm_ref[...], w_ref[...])

    @pl.when((k > 0) & (k < last))
    def _():
        side_work(maybe_first=False, maybe_last=False)
        acc_scr[...] += _dot(hm_ref[...], w_ref[...])

    @pl.when(k == last)
    def _():
        side_work(maybe_first=False, maybe_last=True)
        for rows in _row_chunks(x1_ref.shape[0], NORM_ROW_CHUNK):
            f = acc_scr[rows, :] + _dot(hm_ref[rows, :], w_ref[...])
            x2 = x1_ref[rows, :] + _rms(f, gpost_ref[...])
            xg = _rms(x2, gple_ref[...]).astype(BF16)
            gate = jax.nn.sigmoid(_dot(xg, wpg_ref[...]))
            y_ref[rows, :] = x2 + _dot(p_ref[rows, :].astype(BF16), wp_ref[...]) * gate


def _ffn_out_ple_attend_kernel(pt_ref, sel_ref, hm_ref, w_ref, x1_ref, gpost_ref, gple_ref, p_ref, wpg_ref,
                               wp_ref, q_ref, kn_ref, vn_ref, ck_hbm, cv_hbm, y_ref, a_ref,
                               acc_scr, kbuf, vbuf, sem, *, dec_s, n_slots):
    step = pl.program_id(0) * pl.num_programs(1) + pl.program_id(1)
    n_steps = pl.num_programs(0) * pl.num_programs(1)

    def attend(maybe_first, maybe_last):
        slot = _sample_fetch(step, n_steps, pt_ref, sel_ref, ck_hbm, cv_hbm, kbuf, vbuf, sem, dec_s=dec_s,
                             n_slots=n_slots, maybe_first=maybe_first, maybe_last=maybe_last)
        _sample_attend(slot, q_ref, kn_ref, vn_ref, a_ref, kbuf, vbuf, dec_s=dec_s, n_slots=n_slots)

    _ffn_out_ple_kernel(hm_ref, w_ref, x1_ref, gpost_ref, gple_ref, p_ref, wpg_ref, wp_ref, y_ref, acc_scr,
                        side_work=attend)


def _ffn_out_ple(hmid, w_ffn_out, x1, g_post_ffn, g_ple, p, w_ple_gate, w_ple, *, tm, tk, attend=None):
    m, d = x1.shape
    hidden = hmid.shape[1]
    n_row, n_k = m // tm, hidden // tk
    assert n_k >= 2
    row = pl.BlockSpec((tm, d), lambda i, k, *_: (i, 0))
    vec = pl.BlockSpec((1, d), lambda i, k, *_: (0, 0))
    const = lambda shape: pl.BlockSpec(shape, lambda i, k, *_: (0, 0), pipeline_mode=pl.Buffered(1))
    in_specs = [pl.BlockSpec((tm, tk), lambda i, k, *_: (i, k)),
                pl.BlockSpec((tk, d), lambda i, k, *_: (k, 0)),
                row, vec, vec,
                pl.BlockSpec((tm, p.shape[1]), lambda i, k, *_: (i, 0)),
                const(w_ple_gate.shape), const(w_ple.shape)]
    args = (hmid, w_ffn_out, x1, g_post_ffn, g_ple, p, w_ple_gate, w_ple)
    y_shape = jax.ShapeDtypeStruct((m, d), F32)
    acc = pltpu.VMEM((tm, d), F32)
    if attend is None:
        return pl.pallas_call(
            _ffn_out_ple_kernel, grid=(n_row, n_k), in_specs=in_specs, out_specs=row, out_shape=y_shape,
            scratch_shapes=[acc], compiler_params=_params("parallel", "arbitrary"), name="ffn_out_ple",
        )(*args)

    qkv, cache_k4, cache_v4, page_table, sel, dec_s = attend
    _, dec_b, rows, width = qkv.shape
    page = cache_k4.shape[1]
    n_slots = dec_s * MOBA_TOPK
    assert n_row * n_k == dec_b * N_HEADS
    pair_idx = lambda i, k: ((i * n_k + k) // N_HEADS, 0, (i * n_k + k) % N_HEADS)
    pair_spec = pl.BlockSpec((None, rows, HEAD_DIM), lambda i, k, *_: pair_idx(i, k))
    part_spec = lambda part: pl.BlockSpec((None, None, rows, HEAD_DIM), lambda i, k, *_: (part,) + pair_idx(i, k))
    hbm_spec = pl.BlockSpec(memory_space=pl.ANY)
    grid_spec = pltpu.PrefetchScalarGridSpec(
        num_scalar_prefetch=2,
        grid=(n_row, n_k),
        in_specs=in_specs + [part_spec(0), part_spec(1), part_spec(2), hbm_spec, hbm_spec],
        out_specs=[row, pair_spec],
        scratch_shapes=[acc,
                        pltpu.VMEM((2, 2 * n_slots, page, HEAD_DIM), F32),
                        pltpu.VMEM((2, 2 * n_slots, page, HEAD_DIM), F32),
                        pltpu.SemaphoreType.DMA((2, 2))],
    )
    return pl.pallas_call(
        functools.partial(_ffn_out_ple_attend_kernel, dec_s=dec_s, n_slots=n_slots),
        grid_spec=grid_spec,
        out_shape=[y_shape, jax.ShapeDtypeStruct((dec_b, rows, width), F32)],
        compiler_params=_params("arbitrary", "arbitrary"),
        name="ffn_out_ple_attend",
    )(page_table, sel, *args, qkv, qkv, qkv, cache_k4, cache_v4)


def _tail_front(x, h, attn, gm, w, *, tm, scan=None, f32_weights=None):
    m, d = x.shape
    f32_weights = f32_weights or {}
    ffn_tn = w["w_ffn_out"].shape[0] // FFN_COL_BLOCKS
    mixin, cast = _gated_mix(attn, gm, h, w["w_a_out"], w["w_b_out"], w["w_gate"], tm=min(m, 1024), tn=512,
                             casts=tuple(f32_weights.values()), pair_width=ffn_tn)
    w.update(zip(f32_weights.keys(), cast))
    x1, xn = _mix_out(mixin, x, w["w_o"], w["g_post_mix"], w["g_pre_ffn"], tm=tm)
    if scan is None:
        return x1, _ffn_in(xn, w["w_ffn_in"], tm=tm, tn=ffn_tn), None
    hmid, kmean = _ffn_in(xn, w["w_ffn_in"], tm=tm, tn=ffn_tn, scan=scan)
    return x1, hmid, kmean


def _tail_back(x1, hmid, p, w, *, tm, attend=None):
    return _ffn_out_ple(hmid, w["w_ffn_out"], x1, w["g_post_ffn"], w["g_ple"], p, w["w_ple_gate"], w["w_ple"],
                        tm=tm, tk=w["w_ffn_out"].shape[0] // FFN_COL_BLOCKS, attend=attend)


def _rope_tables(pos):
    lane = jnp.arange(HEAD_DIM, dtype=jnp.int32)
    freqs = jnp.power(jnp.float32(ROPE_THETA), -2.0 * (lane % ROPE_HALF).astype(F32) / ROPE_DIM)
    ang = pos.astype(F32)[:, None] * freqs[None, :]
    cos, sin = jnp.cos(ang), jnp.sin(ang)
    c = jnp.where(lane < ROPE_DIM, cos, 1.0)
    sa = jnp.where(lane < ROPE_HALF, -sin, 0.0)
    sb = jnp.where((lane >= ROPE_HALF) & (lane < ROPE_DIM), sin, 0.0)
    return c, sa, sb


def kernel(x_prompt, x_sample, cache_k, cache_v, page_table, p_prompt, p_sample, g_pre_mix, w_in, g_vnorm, w_spatial, b_spatial, w_a_out, w_b_out, w_gate, w_o, g_post_mix, g_pre_ffn, w_ffn_in, w_ffn_out, g_post_ffn, g_ple, w_ple_gate, w_ple):
    batch, seq, d = x_prompt.shape
    dec_b, dec_s, _ = x_sample.shape
    depth = w_in.shape[0]
    page = cache_k.shape[2]
    past_len = page_table.shape[1] * page
    n_past = past_len // MOBA_BLOCK
    assert depth == 1 and seq % MOBA_BLOCK == 0 and past_len % MOBA_BLOCK == 0 and MOBA_BLOCK == 2 * page
    assert dec_s <= GMLP_CHUNK and n_past >= MOBA_TOPK
    l = 0

    w = {
        "w_ple": w_ple[l].astype(BF16),
        "g_post_mix": g_post_mix[l][None], "g_pre_ffn": g_pre_ffn[l][None],
        "g_post_ffn": g_post_ffn[l][None], "g_ple": g_ple[l][None],
    }
    tail_weights = {"w_a_out": w_a_out[l], "w_b_out": w_b_out[l], "w_gate": w_gate[l], "w_o": w_o[l],
                    "w_ffn_out": w_ffn_out[l], "w_ple_gate": w_ple_gate[l]}
    ffn_in_weight = {"w_ffn_in": w_ffn_in[l]}
    g_pre = g_pre_mix[l][None]
    g_vn = g_vnorm[l][None]

    w_tril = jnp.tril(w_spatial[l])
    wsp_p = w_tril.astype(BF16)
    bsp_p = b_spatial[l].T
    eye_b = jnp.eye(dec_b, dtype=F32)
    wsp_s = jnp.einsum("ab,gts->gatbs", eye_b, w_tril[:, :dec_s, :dec_s]).reshape(
        GMLP_GROUPS, dec_b * dec_s, dec_b * dec_s).astype(BF16)
    bsp_s = jnp.tile(b_spatial[l][:, :dec_s].T, (dec_b, 1))

    ms = dec_b * dec_s
    xs = x_sample.reshape(ms, d)
    pos_s = past_len + jnp.arange(dec_s, dtype=jnp.int32)
    cs, sas, sbs = _rope_tables(jnp.tile(pos_s, dec_b))
    hs, qs, ks, vs, gms, vns, w_in_b = _mixer_cast(xs, g_pre, w_in[l], cs, sas, sbs, g_vn, wsp_s, bsp_s,
                                                   tk=W_IN_CAST_ROWS)
    mp = batch * seq
    xp = x_prompt.reshape(mp, d)
    cp, sap, sbp = _rope_tables(jnp.arange(seq, dtype=jnp.int32))
    hp, qp, kp, vp, gmp = _mixer(xp, g_pre, w_in_b, cp, sap, sbp, g_vn, wsp_p, bsp_p,
                                 tm=MIXER_ROWS, chunk=GMLP_CHUNK)

    ap, cast = _moba_prompt(qp, kp, vp, batch=batch, seq=seq, casts=tuple(tail_weights.values()))
    w.update(zip(tail_weights.keys(), cast))
    n_pool = depth * cache_k.shape[1]
    cache_k2 = cache_k.reshape(n_pool, page * N_HEADS, HEAD_DIM)
    x1p, hmid_p, kmean = _tail_front(xp, hp, ap, gmp, w, tm=512, scan=(cache_k2, page_table),
                                     f32_weights=ffn_in_weight)
    sel = _select_blocks(qs, kmean, ks, dec_s=dec_s, n_past=n_past)
    qkv = jnp.stack([qs, ks, vs]).reshape(3, dec_b, dec_s, ATTN_WIDTH)
    attend = (qkv, cache_k.reshape(n_pool, page, N_HEADS, HEAD_DIM),
              cache_v.reshape(n_pool, page, N_HEADS, HEAD_DIM), page_table, sel, dec_s)
    yp, a_s = _tail_back(x1p, hmid_p, p_prompt[l].reshape(mp, -1), w, tm=512, attend=attend)

    a_s = a_s.reshape(ms, ATTN_WIDTH).astype(BF16)
    x1s, hmid_s, _ = _tail_front(xs, hs, a_s, gms, w, tm=ms)
    ys = _tail_back(x1s, hmid_s, p_sample[l].reshape(ms, -1), w, tm=ms)

    return (yp.reshape(batch, seq, d), ys.reshape(dec_b, dec_s, d),
            kp.reshape(1, batch, seq, N_HEADS, HEAD_DIM), vp.reshape(1, batch, seq, N_HEADS, HEAD_DIM),
            ks.reshape(1, dec_b, dec_s, N_HEADS, HEAD_DIM), vs.reshape(1, dec_b, dec_s, N_HEADS, HEAD_DIM),
            vns.reshape(1, dec_b, dec_s, GMLP_WIDTH))
```

```python
import functools

import jax
import jax.numpy as jnp
from jax import lax
from jax.experimental import pallas as pl
from jax.experimental.pallas import tpu as pltpu

F32 = jnp.float32
BF16 = jnp.bfloat16

N_HEADS = 8
HEAD_DIM = 128
ATTN_WIDTH = N_HEADS * HEAD_DIM
MOBA_BLOCK = 256
MOBA_TOPK = 3
ROPE_THETA = 500000.0
ROPE_DIM = HEAD_DIM // 4
ROPE_HALF = ROPE_DIM // 2
GMLP_GROUPS = 8
GMLP_CHUNK = 128
GMLP_WIDTH = 1024
GMLP_GROUP_DIM = GMLP_WIDTH // GMLP_GROUPS
NORM_EPS = 1e-6
NEG_INF = -1e30
LOG2_E = 1.4426950408889634
TAKEN = -3e38
MIXER_ROWS = 256
W_IN_CAST_ROWS = 256
FFN_COL_BLOCKS = 4
TAIL_ROWS = 512
GATED_MIX_ROWS = 1024
GATED_MIX_COLS = 512
NORM_ROW_CHUNK = 256
BF16_SUBLANES = 16

VMEM_LIMIT_BYTES = 56 * 1024 * 1024


def _params(*semantics):
    return pltpu.CompilerParams(dimension_semantics=semantics, vmem_limit_bytes=VMEM_LIMIT_BYTES)


def _rms(x, g):
    return x * lax.rsqrt(jnp.mean(x * x, axis=-1, keepdims=True) + NORM_EPS) * g


def _dot(a, b):
    return jnp.dot(a, b, preferred_element_type=F32)


def _slab_specs(weights, n_steps, step_of):
    specs = []
    for wt in weights:
        rows = wt.shape[0] // n_steps
        assert rows * n_steps == wt.shape[0] and rows % BF16_SUBLANES == 0
        specs.append(pl.BlockSpec((rows, wt.shape[1]), lambda *g: (step_of(*g), 0)))
    return specs


def _cast_slabs(src_refs, dst_refs, pair_width=None):
    for src_ref, dst_ref in zip(src_refs, dst_refs):
        if pair_width is None:
            dst_ref[...] = src_ref[...].astype(BF16)
            continue
        half = src_ref.shape[1] // 2
        for j in range(half // pair_width):
            for part in range(2):
                src_cols = slice(part * half + j * pair_width, part * half + (j + 1) * pair_width)
                dst_cols = slice((2 * j + part) * pair_width, (2 * j + part + 1) * pair_width)
                dst_ref[:, dst_cols] = src_ref[:, src_cols].astype(BF16)


def _dot_nt(a, b):
    return lax.dot_general(a, b, (((1,), (1,)), ((), ())), preferred_element_type=F32)


def _mixer_outputs(proj, tm, cos_ref, sa_ref, sb_ref, gvn_ref, wsp_ref, bsp_ref,
                   q_out, k_out, v_out, gm_out, vn_out, *, chunk):

    def rope_to(out_ref, z):
        c, sa, sb = cos_ref[...], sa_ref[...], sb_ref[...]
        for hd in range(N_HEADS):
            sl = slice(hd * HEAD_DIM, (hd + 1) * HEAD_DIM)
            zs = z[:, sl]
            out_ref[:, sl] = (zs * c + pltpu.roll(zs, HEAD_DIM - ROPE_HALF, 1) * sa
                              + pltpu.roll(zs, ROPE_HALF, 1) * sb)

    rope_to(q_out, proj(0))
    rope_to(k_out, proj(1))
    v_out[...] = proj(2)
    u = jax.nn.gelu(proj(3))
    vg = jax.nn.gelu(proj(4))
    xc = vg - jnp.mean(vg, axis=-1, keepdims=True)
    vn = xc * lax.rsqrt(jnp.mean(xc * xc, axis=-1, keepdims=True) + NORM_EPS) * gvn_ref[...]
    if vn_out is not None:
        vn_out[...] = vn
    vnb = vn.astype(BF16)
    for c in range(tm // chunk):
        rows = slice(c * chunk, (c + 1) * chunk)
        for g in range(GMLP_GROUPS):
            cols = slice(g * GMLP_GROUP_DIM, (g + 1) * GMLP_GROUP_DIM)
            s = _dot(wsp_ref[g], vnb[rows, cols]) + bsp_ref[:, g:g + 1]
            gm_out[rows, cols] = (u[rows, cols] * s).astype(BF16)


def _mixer_kernel(x_ref, g_ref, w_ref, cos_ref, sa_ref, sb_ref, gvn_ref, wsp_ref, bsp_ref,
                  h_out, q_out, k_out, v_out, gm_out, *, chunk):
    hb = _rms(x_ref[...], g_ref[...]).astype(BF16)
    h_out[...] = hb
    proj = lambda s: _dot(hb, w_ref[:, s * ATTN_WIDTH:(s + 1) * ATTN_WIDTH])
    _mixer_outputs(proj, x_ref.shape[0], cos_ref, sa_ref, sb_ref, gvn_ref, wsp_ref, bsp_ref,
                   q_out, k_out, v_out, gm_out, None, chunk=chunk)


def _mixer_cast_kernel(x_ref, xk_ref, gk_ref, w_ref, cos_ref, sa_ref, sb_ref, gvn_ref, wsp_ref, bsp_ref,
                       h_out, q_out, k_out, v_out, gm_out, vn_out, wb_out, inv_scr, acc_scr, *, chunk):
    k = pl.program_id(0)

    @pl.when(k == 0)
    def _():
        x = x_ref[...]
        inv_scr[...] = lax.rsqrt(jnp.mean(x * x, axis=-1, keepdims=True) + NORM_EPS)
        acc_scr[...] = jnp.zeros(acc_scr.shape, F32)

    wb = w_ref[...].astype(BF16)
    wb_out[...] = wb
    hk = (xk_ref[...] * inv_scr[...] * gk_ref[...]).astype(BF16)
    h_out[...] = hk
    acc_scr[...] += _dot(hk, wb)

    @pl.when(k == pl.num_programs(0) - 1)
    def _():
        proj = lambda s: acc_scr[:, s * ATTN_WIDTH:(s + 1) * ATTN_WIDTH]
        _mixer_outputs(proj, x_ref.shape[0], cos_ref, sa_ref, sb_ref, gvn_ref, wsp_ref, bsp_ref,
                       q_out, k_out, v_out, gm_out, vn_out, chunk=chunk)


def _mixer_shapes(m, d):
    return [jax.ShapeDtypeStruct((m, d), BF16),
            jax.ShapeDtypeStruct((m, ATTN_WIDTH), F32),
            jax.ShapeDtypeStruct((m, ATTN_WIDTH), F32),
            jax.ShapeDtypeStruct((m, ATTN_WIDTH), F32),
            jax.ShapeDtypeStruct((m, GMLP_WIDTH), BF16)]


def _mixer(x, g_pre, w_in, cos_t, sa_t, sb_t, g_vn, wsp, bsp, *, tm, chunk):
    m, d = x.shape
    assert w_in.shape[1] == 3 * ATTN_WIDTH + 2 * GMLP_WIDTH and ATTN_WIDTH == GMLP_WIDTH
    t_blocks = cos_t.shape[0] // tm
    tab_spec = pl.BlockSpec((tm, HEAD_DIM), lambda i: (i % t_blocks, 0))
    row_spec = lambda w: pl.BlockSpec((tm, w), lambda i: (i, 0))
    const = lambda shape: pl.BlockSpec(shape, lambda i: (0,) * len(shape))
    return pl.pallas_call(
        functools.partial(_mixer_kernel, chunk=chunk),
        grid=(m // tm,),
        in_specs=[row_spec(d),
                  const((1, d)),
                  pl.BlockSpec(w_in.shape, lambda i: (0, 0), pipeline_mode=pl.Buffered(1)),
                  tab_spec, tab_spec, tab_spec,
                  const((1, GMLP_WIDTH)),
                  const((GMLP_GROUPS, chunk, chunk)),
                  const((chunk, GMLP_GROUPS))],
        out_specs=[row_spec(d), row_spec(ATTN_WIDTH), row_spec(ATTN_WIDTH), row_spec(ATTN_WIDTH),
                   row_spec(GMLP_WIDTH)],
        out_shape=_mixer_shapes(m, d),
        compiler_params=_params("parallel"),
        name="mixer",
    )(x, g_pre, w_in, cos_t, sa_t, sb_t, g_vn, wsp, bsp)


def _mixer_cast(x, g_pre, w_in_f32, cos_t, sa_t, sb_t, g_vn, wsp, bsp, *, tk):
    m, d = x.shape
    n_out = w_in_f32.shape[1]
    assert n_out == 3 * ATTN_WIDTH + 2 * GMLP_WIDTH and ATTN_WIDTH == GMLP_WIDTH and d % tk == 0
    const = lambda shape: pl.BlockSpec(shape, lambda k: (0,) * len(shape))
    full = lambda w: const((m, w))
    return pl.pallas_call(
        functools.partial(_mixer_cast_kernel, chunk=m),
        grid=(d // tk,),
        in_specs=[full(d),
                  pl.BlockSpec((m, tk), lambda k: (0, k)),
                  pl.BlockSpec((1, tk), lambda k: (0, k)),
                  pl.BlockSpec((tk, n_out), lambda k: (k, 0)),
                  full(HEAD_DIM), full(HEAD_DIM), full(HEAD_DIM),
                  const((1, GMLP_WIDTH)),
                  const((GMLP_GROUPS, m, m)),
                  const((m, GMLP_GROUPS))],
        out_specs=[pl.BlockSpec((m, tk), lambda k: (0, k)),
                   full(ATTN_WIDTH), full(ATTN_WIDTH), full(ATTN_WIDTH), full(GMLP_WIDTH), full(GMLP_WIDTH),
                   pl.BlockSpec((tk, n_out), lambda k: (k, 0))],
        out_shape=_mixer_shapes(m, d) + [jax.ShapeDtypeStruct((m, GMLP_WIDTH), F32),
                                         jax.ShapeDtypeStruct(w_in_f32.shape, BF16)],
        scratch_shapes=[pltpu.VMEM((m, 1), F32), pltpu.VMEM((m, n_out), F32)],
        compiler_params=_params("arbitrary"),
        name="mixer_cast",
    )(x, x, g_pre, w_in_f32, cos_t, sa_t, sb_t, g_vn, wsp, bsp)


def _moba_prompt_kernel(q_ref, k_ref, v_ref, *refs):
    n_cast = (len(refs) - 4) // 2
    o_ref = refs[n_cast]
    kb_scr, vt_scr, s_scr = refs[-3:]
    _cast_slabs(refs[:n_cast], refs[n_cast + 1:2 * n_cast + 1])
    seq = q_ref.shape[0]
    nblk = seq // MOBA_BLOCK
    blk = MOBA_BLOCK
    q_scale = HEAD_DIM ** -0.5 * LOG2_E
    kb_scr[...] = k_ref[...].astype(BF16)
    vt_scr[:HEAD_DIM, :] = v_ref[...].T.astype(BF16)
    vt_scr[HEAD_DIM:, :] = jnp.ones((BF16_SUBLANES, seq), BF16)
    qt = q_ref[...].T
    kt = k_ref[...].T

    blk_id = lax.broadcasted_iota(jnp.int32, (nblk, seq), 0)
    q_blk = lax.broadcasted_iota(jnp.int32, (nblk, seq), 1) // blk
    gate = jnp.zeros((nblk, seq), F32)
    for n in range(nblk - 1):
        kmean_n = jnp.mean(kt[:, n * blk:(n + 1) * blk], axis=1, keepdims=True)
        g_past = jnp.sum(qt[:, (n + 1) * blk:] * kmean_n, axis=0, keepdims=True)
        g_n = jnp.concatenate([jnp.zeros((1, (n + 1) * blk), F32), g_past], axis=1)
        gate = jnp.where(blk_id == n, g_n, gate)
    rank = jnp.zeros((nblk, seq), jnp.int32)
    for m in range(nblk):
        g_m = gate[m:m + 1, :]
        beats = (m < q_blk) & ((g_m > gate) | ((g_m == gate) & (m < blk_id)))
        rank = rank + beats.astype(jnp.int32)
    keep = jnp.where((blk_id < q_blk) & (rank < MOBA_TOPK), 1.0, 0.0)

    key_i = lax.broadcasted_iota(jnp.int32, (blk, blk), 0)
    qry_i = lax.broadcasted_iota(jnp.int32, (blk, blk), 1)
    causal = key_i <= qry_i

    for j in range(nblk):
        cols = slice(j * blk, (j + 1) * blk)
        qb = (q_ref[cols, :] * q_scale).astype(BF16)
        keep_j = keep[:, cols]
        m_run = None
        for n in range(j + 1):
            st = _dot_nt(kb_scr[n * blk:(n + 1) * blk, :], qb)
            if n == j:
                st = jnp.where(causal, st, NEG_INF)
            else:
                st = jnp.where(keep_j[n:n + 1, :] > 0.5, st, NEG_INF)
            s_scr[n] = st
            m_n = jnp.max(st, axis=0, keepdims=True)
            m_run = m_n if m_run is None else jnp.maximum(m_run, m_n)
        acc = jnp.zeros((HEAD_DIM + BF16_SUBLANES, blk), F32)
        for n in range(j + 1):
            p = jnp.exp2(s_scr[n] - m_run)
            acc = acc + _dot(vt_scr[:, n * blk:(n + 1) * blk], p.astype(BF16))
        o_ref[cols, :] = (acc[:HEAD_DIM, :] / acc[HEAD_DIM:HEAD_DIM + 1, :]).T.astype(BF16)


def _moba_prompt(q, k, v, *, batch, seq, casts=()):
    spec = pl.BlockSpec((seq, HEAD_DIM), lambda b, h: (b, h))
    slabs = _slab_specs(casts, batch * N_HEADS, lambda b, h: b * N_HEADS + h)
    out = pl.pallas_call(
        _moba_prompt_kernel,
        grid=(batch, N_HEADS),
        in_specs=[spec, spec, spec] + slabs,
        out_specs=[spec] + slabs,
        out_shape=[jax.ShapeDtypeStruct(q.shape, BF16)] + [jax.ShapeDtypeStruct(wt.shape, BF16) for wt in casts],
        scratch_shapes=[pltpu.VMEM((seq, HEAD_DIM), BF16), pltpu.VMEM((HEAD_DIM + BF16_SUBLANES, seq), BF16),
                        pltpu.VMEM((seq // MOBA_BLOCK, MOBA_BLOCK, MOBA_BLOCK), F32)],
        compiler_params=_params("parallel", "parallel"),
        name="moba_prompt",
    )(q, k, v, *casts)
    return out[0], tuple(out[1:])


def _scan_key_means(page_refs, km_ref, group):
    n_blk = len(page_refs) // 2
    page_rows = page_refs[0].shape[0] // N_HEADS

    def page_sum(ref):
        return jnp.sum(ref[...].reshape(page_rows, N_HEADS, HEAD_DIM), axis=0)

    for t in range(n_blk):
        tot = (page_sum(page_refs[2 * t]) + page_sum(page_refs[2 * t + 1])) * (1.0 / MOBA_BLOCK)
        for h in range(N_HEADS):
            km_ref[h, pl.ds(group * n_blk + t, 1), :] = tot[h:h + 1, :]


def _select_kernel(q_ref, km_ref, kn_ref, sel_ref, gate_scr, *, dec_s, n_past):
    dec_b = km_ref.shape[0]
    gate_scr[...] = jnp.full(gate_scr.shape, NEG_INF, F32)
    for b in range(dec_b):
        own_mean = jnp.sum(kn_ref[b * dec_s:(b + 1) * dec_s, :], axis=0, keepdims=True) * (1.0 / MOBA_BLOCK)
        for h in range(N_HEADS):
            cols = slice(h * HEAD_DIM, (h + 1) * HEAD_DIM)
            km = km_ref[b, h]
            for i in range(dec_s):
                col = (b * N_HEADS + h) * dec_s + i
                qi = q_ref[b * dec_s + i:b * dec_s + i + 1, cols]
                gate_scr[0:n_past, col:col + 1] = jnp.sum(km * qi, axis=-1, keepdims=True)
                gate_scr[n_past:n_past + 1, col:col + 1] = jnp.sum(qi * own_mean[:, cols], axis=-1,
                                                                   keepdims=True)
    gate = gate_scr[...]
    blk = lax.broadcasted_iota(jnp.int32, gate.shape, 0)
    gate = jnp.where(blk < n_past, gate, NEG_INF)
    out_row = lax.broadcasted_iota(jnp.int32, sel_ref.shape, 0)
    sel = jnp.zeros(sel_ref.shape, jnp.int32)
    for t in range(MOBA_TOPK):
        best = jnp.max(gate, axis=0, keepdims=True)
        idx = jnp.min(jnp.where(gate == best, blk, gate.shape[0]), axis=0, keepdims=True)
        sel = jnp.where(out_row == t, idx, sel)
        gate = jnp.where(blk == idx, TAKEN, gate)
    sel_ref[...] = sel


def _select_blocks(q2, kmean, kn2, *, dec_s, n_past):
    cols = q2.shape[0] * N_HEADS
    gate_rows = 8 * (-(-(n_past + 1) // 8))
    return pl.pallas_call(
        functools.partial(_select_kernel, dec_s=dec_s, n_past=n_past),
        out_shape=jax.ShapeDtypeStruct((8, cols), jnp.int32),
        scratch_shapes=[pltpu.VMEM((gate_rows, cols), F32)],
        compiler_params=pltpu.CompilerParams(vmem_limit_bytes=VMEM_LIMIT_BYTES),
        name="select_blocks",
    )(q2, kmean, kn2)


def _sample_fetch(step, n_steps, pt_ref, sel_ref, ck_hbm, cv_hbm, kbuf, vbuf, sem, *, dec_s, n_slots,
                  maybe_first, maybe_last):
    n_pages = 2 * n_slots

    def page_copies(step_idx, slot):
        bb, hh = step_idx // N_HEADS, step_idx % N_HEADS
        copies = []
        for c in range(n_pages):
            query, pick = (c // 2) // MOBA_TOPK, (c // 2) % MOBA_TOPK
            blk = sel_ref[pick, step_idx * dec_s + query]
            pg = pt_ref[bb, 2 * blk + c % 2]
            src = (pg, slice(None), hh, slice(None))
            copies.append(pltpu.make_async_copy(ck_hbm.at[src], kbuf.at[slot, c], sem.at[0, slot]))
            copies.append(pltpu.make_async_copy(cv_hbm.at[src], vbuf.at[slot, c], sem.at[1, slot]))
        return copies

    def start_next():
        for cp in page_copies(step + 1, (step + 1) % 2):
            cp.start()

    if maybe_first:
        @pl.when(step == 0)
        def _():
            for cp in page_copies(step, 0):
                cp.start()

    if maybe_last:
        pl.when(step + 1 < n_steps)(start_next)
    else:
        start_next()

    slot = step % 2
    for cp in page_copies(step, slot):
        cp.wait()
    return slot


def _sample_attend(slot, q_ref, kn_ref, vn_ref, o_ref, kbuf, vbuf, *, dec_s, n_slots):
    pages_per_query = 2 * MOBA_TOPK
    assert n_slots * 2 == dec_s * pages_per_query
    scale = HEAD_DIM ** -0.5
    kn, vn = kn_ref[...], vn_ref[...]
    new_row = lax.broadcasted_iota(jnp.int32, (dec_s, 1), 0)
    for i in range(dec_s):
        qi = q_ref[i:i + 1, :]
        pages = [i * pages_per_query + c for c in range(pages_per_query)]
        s_pages = [jnp.sum(kbuf[slot, c] * qi, axis=-1, keepdims=True) * scale for c in pages]
        s_own = jnp.where(new_row <= i, jnp.sum(kn * qi, axis=-1, keepdims=True) * scale, NEG_INF)
        m = jnp.max(s_own, axis=0, keepdims=True)
        for s in s_pages:
            m = jnp.maximum(m, jnp.max(s, axis=0, keepdims=True))
        p_own = jnp.exp(s_own - m)
        l = jnp.sum(p_own, axis=0, keepdims=True)
        acc = jnp.sum(p_own * vn, axis=0, keepdims=True)
        for c, s in zip(pages, s_pages):
            p = jnp.exp(s - m)
            l = l + jnp.sum(p, axis=0, keepdims=True)
            acc = acc + jnp.sum(p * vbuf[slot, c], axis=0, keepdims=True)
        o_ref[i:i + 1, :] = acc / l


def _gated_mix_kernel(a_ref, b_ref, h_ref, wa_ref, wb_ref, wga_ref, wgb_ref, *refs, pair_width):
    n_cast = len(refs) // 2
    o_ref = refs[n_cast]
    _cast_slabs(refs[:n_cast], refs[n_cast + 1:], pair_width)
    h = h_ref[...]
    ga = jax.nn.sigmoid(_dot(h, wga_ref[...]))
    gb = jax.nn.sigmoid(_dot(h, wgb_ref[...]))
    a = _dot(a_ref[...], wa_ref[...])
    b = _dot(b_ref[...], wb_ref[...])
    o_ref[...] = (ga * a + gb * b).astype(BF16)


def _gated_mix(attn, gm, h, w_a, w_b, w_gate, *, tm, tn, casts=(), pair_width=None):
    m, d = h.shape
    n_col, n_row = d // tn, m // tm
    slabs = _slab_specs(casts, n_col * n_row, lambda j, i: j * n_row + i)
    out = pl.pallas_call(
        functools.partial(_gated_mix_kernel, pair_width=pair_width),
        grid=(n_col, n_row),
        in_specs=[pl.BlockSpec((tm, attn.shape[1]), lambda j, i: (i, 0)),
                  pl.BlockSpec((tm, gm.shape[1]), lambda j, i: (i, 0)),
                  pl.BlockSpec((tm, d), lambda j, i: (i, 0)),
                  pl.BlockSpec((w_a.shape[0], tn), lambda j, i: (0, j)),
                  pl.BlockSpec((w_b.shape[0], tn), lambda j, i: (0, j)),
                  pl.BlockSpec((d, tn), lambda j, i: (0, j)),
                  pl.BlockSpec((d, tn), lambda j, i: (0, j + n_col))] + slabs,
        out_specs=[pl.BlockSpec((tm, tn), lambda j, i: (i, j))] + slabs,
        out_shape=[jax.ShapeDtypeStruct((m, d), BF16)] + [jax.ShapeDtypeStruct(wt.shape, BF16) for wt in casts],
        compiler_params=_params("parallel", "parallel"),
        name="gated_mix",
    )(attn, gm, h, w_a, w_b, w_gate, w_gate, *casts)
    return out[0], tuple(out[1:])


def _row_chunks(tm, rc):
    rc = min(tm, rc)
    return [slice(r, r + rc) for r in range(0, tm, rc)]


def _mix_out_kernel(mix_ref, x_ref, wo_ref, gpost_ref, gpre_ref, x1_ref, xn_ref):
    for rows in _row_chunks(x_ref.shape[0], NORM_ROW_CHUNK):
        mix = _dot(mix_ref[rows, :], wo_ref[...])
        x1 = x_ref[rows, :] + _rms(mix, gpost_ref[...])
        x1_ref[rows, :] = x1
        xn_ref[rows, :] = _rms(x1, gpre_ref[...]).astype(BF16)


def _mix_out(mixin, x, w_o, g_post, g_pre_ffn, *, tm):
    m, d = x.shape
    row = pl.BlockSpec((tm, d), lambda i: (i, 0))
    vec = pl.BlockSpec((1, d), lambda i: (0, 0))
    return pl.pallas_call(
        _mix_out_kernel,
        grid=(m // tm,),
        in_specs=[row, row, pl.BlockSpec((d, d), lambda i: (0, 0), pipeline_mode=pl.Buffered(1)), vec, vec],
        out_specs=[row, row],
        out_shape=[jax.ShapeDtypeStruct((m, d), F32), jax.ShapeDtypeStruct((m, d), BF16)],
        compiler_params=_params("parallel"),
        name="mix_out",
    )(mixin, x, w_o, g_post, g_pre_ffn)


def _ffn_in_kernel(x_ref, w_ref, o_ref):
    tn = o_ref.shape[1]
    z = _dot(x_ref[...], w_ref[...])
    o_ref[...] = (jax.nn.silu(z[:, :tn]) * z[:, tn:]).astype(BF16)


def _ffn_in_scan_kernel(pt_ref, x_ref, w_ref, *refs, groups):
    page_refs, o_ref, km_ref = refs[:-2], refs[-2], refs[-1]
    step = pl.program_id(0) * pl.num_programs(1) + pl.program_id(1)
    _scan_key_means(page_refs, km_ref, step % groups)
    _ffn_in_kernel(x_ref, w_ref, o_ref)


def _ffn_in(xn, w_ffn_in, *, tm, tn, scan=None):
    m, d = xn.shape
    hidden = w_ffn_in.shape[1] // 2
    n_col, n_row = hidden // tn, m // tm
    in_specs = [pl.BlockSpec((tm, d), lambda j, i, *_: (i, 0)),
                pl.BlockSpec((d, 2 * tn), lambda j, i, *_: (0, j))]
    out_spec = pl.BlockSpec((tm, tn), lambda j, i, *_: (i, j))
    out_shape = jax.ShapeDtypeStruct((m, hidden), BF16)
    if scan is None:
        return pl.pallas_call(
            _ffn_in_kernel, grid=(n_col, n_row), in_specs=in_specs, out_specs=out_spec, out_shape=out_shape,
            compiler_params=_params("parallel", "parallel"), name="ffn_in",
        )(xn, w_ffn_in)

    cache_k2, page_table = scan
    dec_b, n_pages = page_table.shape
    n_steps = n_col * n_row
    pps = dec_b * n_pages // n_steps
    groups = n_pages // pps
    assert pps * n_steps == dec_b * n_pages and groups * pps == n_pages and pps % 2 == 0
    n_past = n_pages // 2

    def page_spec(c):
        def idx(j, i, pt):
            step = j * n_row + i
            return (pt[step // groups, (step % groups) * pps + c], 0, 0)
        return pl.BlockSpec((None,) + cache_k2.shape[1:], idx)

    grid_spec = pltpu.PrefetchScalarGridSpec(
        num_scalar_prefetch=1,
        grid=(n_col, n_row),
        in_specs=in_specs + [page_spec(c) for c in range(pps)],
        out_specs=[out_spec,
                   pl.BlockSpec((None, N_HEADS, n_past, HEAD_DIM),
                                lambda j, i, pt: ((j * n_row + i) // groups, 0, 0, 0))],
    )
    return pl.pallas_call(
        functools.partial(_ffn_in_scan_kernel, groups=groups),
        grid_spec=grid_spec,
        out_shape=[out_shape, jax.ShapeDtypeStruct((dec_b, N_HEADS, n_past, HEAD_DIM), F32)],
        compiler_params=_params("arbitrary", "arbitrary"),
        name="ffn_in_scan",
    )(page_table, xn, w_ffn_in, *([cache_k2] * pps))


def _ffn_out_ple_kernel(hm_ref, w_ref, x1_ref, gpost_ref, gple_ref, p_ref, wpg_ref, wp_ref, y_ref, acc_scr,
                        side_work=lambda maybe_first, maybe_last: None):
    k = pl.program_id(1)
    last = pl.num_programs(1) - 1

    @pl.when(k == 0)
    def _():
        side_work(maybe_first=True, maybe_last=False)
        acc_scr[...] = _dot(hm_ref[...], w_ref[...])

    @pl.when((k > 0) & (k < last))
    def _():
        side_work(maybe_first=False, maybe_last=False)
        acc_scr[...] += _dot(hm_ref[...], w_ref[...])

    @pl.when(k == last)
    def _():
        side_work(maybe_first=False, maybe_last=True)
        for rows in _row_chunks(x1_ref.shape[0], NORM_ROW_CHUNK):
            f = acc_scr[rows, :] + _dot(hm_ref[rows, :], w_ref[...])
            x2 = x1_ref[rows, :] + _rms(f, gpost_ref[...])
            xg = _rms(x2, gple_ref[...]).astype(BF16)
            gate = jax.nn.sigmoid(_dot(xg, wpg_ref[...]))
            y_ref[rows, :] = x2 + _dot(p_ref[rows, :].astype(BF16), wp_ref[...]) * gate


def _ffn_out_ple_attend_kernel(pt_ref, sel_ref, hm_ref, w_ref, x1_ref, gpost_ref, gple_ref, p_ref, wpg_ref,
                               wp_ref, q_ref, kn_ref, vn_ref, ck_hbm, cv_hbm, y_ref, a_ref,
                               acc_scr, kbuf, vbuf, sem, *, dec_s, n_slots):
    step = pl.program_id(0) * pl.num_programs(1) + pl.program_id(1)
    n_steps = pl.num_programs(0) * pl.num_programs(1)

    def attend(maybe_first, maybe_last):
        slot = _sample_fetch(step, n_steps, pt_ref, sel_ref, ck_hbm, cv_hbm, kbuf, vbuf, sem, dec_s=dec_s,
                             n_slots=n_slots, maybe_first=maybe_first, maybe_last=maybe_last)
        _sample_attend(slot, q_ref, kn_ref, vn_ref, a_ref, kbuf, vbuf, dec_s=dec_s, n_slots=n_slots)

    _ffn_out_ple_kernel(hm_ref, w_ref, x1_ref, gpost_ref, gple_ref, p_ref, wpg_ref, wp_ref, y_ref, acc_scr,
                        side_work=attend)


def _ffn_out_ple(hmid, w_ffn_out, x1, g_post_ffn, g_ple, p, w_ple_gate, w_ple, *, tm, tk, attend=None):
    m, d = x1.shape
    hidden = hmid.shape[1]
    n_row, n_k = m // tm, hidden // tk
    assert n_k >= 2
    row = pl.BlockSpec((tm, d), lambda i, k, *_: (i, 0))
    vec = pl.BlockSpec((1, d), lambda i, k, *_: (0, 0))
    const = lambda shape: pl.BlockSpec(shape, lambda i, k, *_: (0, 0), pipeline_mode=pl.Buffered(1))
    in_specs = [pl.BlockSpec((tm, tk), lambda i, k, *_: (i, k)),
                pl.BlockSpec((tk, d), lambda i, k, *_: (k, 0)),
                row, vec, vec,
                pl.BlockSpec((tm, p.shape[1]), lambda i, k, *_: (i, 0)),
                const(w_ple_gate.shape), const(w_ple.shape)]
    args = (hmid, w_ffn_out, x1, g_post_ffn, g_ple, p, w_ple_gate, w_ple)
    y_shape = jax.ShapeDtypeStruct((m, d), F32)
    acc = pltpu.VMEM((tm, d), F32)
    if attend is None:
        return pl.pallas_call(
            _ffn_out_ple_kernel, grid=(n_row, n_k), in_specs=in_specs, out_specs=row, out_shape=y_shape,
            scratch_shapes=[acc], compiler_params=_params("parallel", "arbitrary"), name="ffn_out_ple",
        )(*args)

    qkv, cache_k4, cache_v4, page_table, sel, dec_s = attend
    _, dec_b, rows, width = qkv.shape
    page = cache_k4.shape[1]
    n_slots = dec_s * MOBA_TOPK
    assert n_row * n_k == dec_b * N_HEADS
    pair_idx = lambda i, k: ((i * n_k + k) // N_HEADS, 0, (i * n_k + k) % N_HEADS)
    pair_spec = pl.BlockSpec((None, rows, HEAD_DIM), lambda i, k, *_: pair_idx(i, k))
    part_spec = lambda part: pl.BlockSpec((None, None, rows, HEAD_DIM), lambda i, k, *_: (part,) + pair_idx(i, k))
    hbm_spec = pl.BlockSpec(memory_space=pl.ANY)
    grid_spec = pltpu.PrefetchScalarGridSpec(
        num_scalar_prefetch=2,
        grid=(n_row, n_k),
        in_specs=in_specs + [part_spec(0), part_spec(1), part_spec(2), hbm_spec, hbm_spec],
        out_specs=[row, pair_spec],
        scratch_shapes=[acc,
                        pltpu.VMEM((2, 2 * n_slots, page, HEAD_DIM), F32),
                        pltpu.VMEM((2, 2 * n_slots, page, HEAD_DIM), F32),
                        pltpu.SemaphoreType.DMA((2, 2))],
    )
    return pl.pallas_call(
        functools.partial(_ffn_out_ple_attend_kernel, dec_s=dec_s, n_slots=n_slots),
        grid_spec=grid_spec,
        out_shape=[y_shape, jax.ShapeDtypeStruct((dec_b, rows, width), F32)],
        compiler_params=_params("arbitrary", "arbitrary"),
        name="ffn_out_ple_attend",
    )(page_table, sel, *args, qkv, qkv, qkv, cache_k4, cache_v4)


def _tail_front(x, h, attn, gm, w, *, tm, scan=None, f32_weights=None):
    m, d = x.shape
    f32_weights = f32_weights or {}
    ffn_tn = w["w_ffn_out"].shape[0] // FFN_COL_BLOCKS
    mixin, cast = _gated_mix(attn, gm, h, w["w_a_out"], w["w_b_out"], w["w_gate"],
                             tm=min(m, GATED_MIX_ROWS), tn=GATED_MIX_COLS,
                             casts=tuple(f32_weights.values()), pair_width=ffn_tn)
    w.update(zip(f32_weights.keys(), cast))
    x1, xn = _mix_out(mixin, x, w["w_o"], w["g_post_mix"], w["g_pre_ffn"], tm=tm)
    if scan is None:
        return x1, _ffn_in(xn, w["w_ffn_in"], tm=tm, tn=ffn_tn), None
    hmid, kmean = _ffn_in(xn, w["w_ffn_in"], tm=tm, tn=ffn_tn, scan=scan)
    return x1, hmid, kmean


def _tail_back(x1, hmid, p, w, *, tm, attend=None):
    return _ffn_out_ple(hmid, w["w_ffn_out"], x1, w["g_post_ffn"], w["g_ple"], p, w["w_ple_gate"], w["w_ple"],
                        tm=tm, tk=w["w_ffn_out"].shape[0] // FFN_COL_BLOCKS, attend=attend)


def _rope_tables(pos):
    lane = jnp.arange(HEAD_DIM, dtype=jnp.int32)
    freqs = jnp.power(jnp.float32(ROPE_THETA), -2.0 * (lane % ROPE_HALF).astype(F32) / ROPE_DIM)
    ang = pos.astype(F32)[:, None] * freqs[None, :]
    cos, sin = jnp.cos(ang), jnp.sin(ang)
    c = jnp.where(lane < ROPE_DIM, cos, 1.0)
    sa = jnp.where(lane < ROPE_HALF, -sin, 0.0)
    sb = jnp.where((lane >= ROPE_HALF) & (lane < ROPE_DIM), sin, 0.0)
    return c, sa, sb


def kernel(x_prompt, x_sample, cache_k, cache_v, page_table, p_prompt, p_sample, g_pre_mix, w_in, g_vnorm, w_spatial, b_spatial, w_a_out, w_b_out, w_gate, w_o, g_post_mix, g_pre_ffn, w_ffn_in, w_ffn_out, g_post_ffn, g_ple, w_ple_gate, w_ple):
    batch, seq, d = x_prompt.shape
    dec_b, dec_s, _ = x_sample.shape
    depth = w_in.shape[0]
    page = cache_k.shape[2]
    past_len = page_table.shape[1] * page
    n_past = past_len // MOBA_BLOCK
    assert depth == 1 and seq % MOBA_BLOCK == 0 and past_len % MOBA_BLOCK == 0 and MOBA_BLOCK == 2 * page
    assert dec_s <= GMLP_CHUNK and n_past >= MOBA_TOPK
    l = 0

    w = {
        "w_ple": w_ple[l].astype(BF16),
        "g_post_mix": g_post_mix[l][None], "g_pre_ffn": g_pre_ffn[l][None],
        "g_post_ffn": g_post_ffn[l][None], "g_ple": g_ple[l][None],
    }
    tail_weights = {"w_a_out": w_a_out[l], "w_b_out": w_b_out[l], "w_gate": w_gate[l], "w_o": w_o[l],
                    "w_ffn_out": w_ffn_out[l], "w_ple_gate": w_ple_gate[l]}
    ffn_in_weight = {"w_ffn_in": w_ffn_in[l]}
    g_pre = g_pre_mix[l][None]
    g_vn = g_vnorm[l][None]

    w_tril = jnp.tril(w_spatial[l])
    wsp_p = w_tril.astype(BF16)
    bsp_p = b_spatial[l].T
    eye_b = jnp.eye(dec_b, dtype=F32)
    wsp_s = jnp.einsum("ab,gts->gatbs", eye_b, w_tril[:, :dec_s, :dec_s]).reshape(
        GMLP_GROUPS, dec_b * dec_s, dec_b * dec_s).astype(BF16)
    bsp_s = jnp.tile(b_spatial[l][:, :dec_s].T, (dec_b, 1))

    ms = dec_b * dec_s
    xs = x_sample.reshape(ms, d)
    pos_s = past_len + jnp.arange(dec_s, dtype=jnp.int32)
    cs, sas, sbs = _rope_tables(jnp.tile(pos_s, dec_b))
    hs, qs, ks, vs, gms, vns, w_in_b = _mixer_cast(xs, g_pre, w_in[l], cs, sas, sbs, g_vn, wsp_s, bsp_s,
                                                   tk=W_IN_CAST_ROWS)
    mp = batch * seq
    xp = x_prompt.reshape(mp, d)
    cp, sap, sbp = _rope_tables(jnp.arange(seq, dtype=jnp.int32))
    hp, qp, kp, vp, gmp = _mixer(xp, g_pre, w_in_b, cp, sap, sbp, g_vn, wsp_p, bsp_p,
                                 tm=MIXER_ROWS, chunk=GMLP_CHUNK)

    ap, cast = _moba_prompt(qp, kp, vp, batch=batch, seq=seq, casts=tuple(tail_weights.values()))
    w.update(zip(tail_weights.keys(), cast))
    n_pool = depth * cache_k.shape[1]
    cache_k2 = cache_k.reshape(n_pool, page * N_HEADS, HEAD_DIM)
    x1p, hmid_p, kmean = _tail_front(xp, hp, ap, gmp, w, tm=TAIL_ROWS, scan=(cache_k2, page_table),
                                     f32_weights=ffn_in_weight)
    sel = _select_blocks(qs, kmean, ks, dec_s=dec_s, n_past=n_past)
    qkv = jnp.stack([qs, ks, vs]).reshape(3, dec_b, dec_s, ATTN_WIDTH)
    attend = (qkv, cache_k.reshape(n_pool, page, N_HEADS, HEAD_DIM),
              cache_v.reshape(n_pool, page, N_HEADS, HEAD_DIM), page_table, sel, dec_s)
    yp, a_s = _tail_back(x1p, hmid_p, p_prompt[l].reshape(mp, -1), w, tm=TAIL_ROWS, attend=attend)

    a_s = a_s.reshape(ms, ATTN_WIDTH).astype(BF16)
    x1s, hmid_s, _ = _tail_front(xs, hs, a_s, gms, w, tm=ms)
    ys = _tail_back(x1s, hmid_s, p_sample[l].reshape(ms, -1), w, tm=ms)

    return (yp.reshape(batch, seq, d), ys.reshape(dec_b, dec_s, d),
            kp.reshape(1, batch, seq, N_HEADS, HEAD_DIM), vp.reshape(1, batch, seq, N_HEADS, HEAD_DIM),
            ks.reshape(1, dec_b, dec_s, N_HEADS, HEAD_DIM), vs.reshape(1, dec_b, dec_s, N_HEADS, HEAD_DIM),
            vns.reshape(1, dec_b, dec_s, GMLP_WIDTH))
```
